```python
import math
import jax, jax.numpy as jnp
from jax import lax
import numpy as np

D_MODEL = 4096
BATCH = 1
SEQ = 8192
DEPTH = 1

MIX_WIDTH = D_MODEL
ATTN_WIDTH = MIX_WIDTH // 2
POOL_WIDTH = MIX_WIDTH - ATTN_WIDTH
DIFF_HEAD_DIM = 128
DIFF_V_DIM = 2 * DIFF_HEAD_DIM
N_DIFF_HEADS = ATTN_WIDTH // DIFF_V_DIM
ATTN_QK_WIDTH = N_DIFF_HEADS * 2 * DIFF_HEAD_DIM
POOL_WINDOWS = (2, 4, 8, 16)
N_POOL_GROUPS = len(POOL_WINDOWS)
POOL_GROUP_WIDTH = POOL_WIDTH // N_POOL_GROUPS
IN_PROJ_WIDTH = 2 * ATTN_QK_WIDTH + ATTN_WIDTH + POOL_WIDTH
N_EXPERT_GROUPS = 4
EXPERTS_PER_GROUP = 8
N_EXPERTS = N_EXPERT_GROUPS * EXPERTS_PER_GROUP
TOP_K_INNER = 2
EXPERT_FF = 1536
MOE_ROWS = 256
Q_BLOCK = 128
RMS_EPS = 1e-6

kernel_name = "hymba_style_diffattn_pool_hmoe_layer"


def _rmsnorm(x, g):
    xf = x.astype(jnp.float32)
    y = xf * lax.rsqrt(jnp.mean(xf * xf, axis=-1, keepdims=True) + RMS_EPS)
    return (y * g.astype(jnp.float32)).astype(x.dtype)


def _alibi_slopes(n_heads):
    return 2.0 ** (-8.0 * jnp.arange(1, n_heads + 1, dtype=jnp.float32) / n_heads)


def _diff_attention(q, k, v, lam, slopes):
    S_ = q.shape[1]
    dh = q.shape[-1]
    scale = dh ** -0.5
    outs = []
    for i in range(S_ // Q_BLOCK):
        q0, q1 = i * Q_BLOCK, (i + 1) * Q_BLOCK
        qb = q[:, q0:q1]
        kb = k[:, :q1]
        vb = v[:, :q1]
        s = jnp.einsum('bqhmd,bkhmd->bhmqk', qb, kb).astype(jnp.float32) * scale
        dist = jnp.arange(q0, q1)[:, None] - jnp.arange(q1)[None, :]
        bias = -slopes[:, None, None] * dist.astype(jnp.float32)
        s = jnp.where(dist >= 0, s + bias[None, :, None], -jnp.inf)
        p = jax.nn.softmax(s, axis=-1)
        a = p[:, :, 0] - lam * p[:, :, 1]
        outs.append(jnp.einsum('bhqk,bkhd->bqhd', a.astype(vb.dtype), vb))
    return jnp.concatenate(outs, axis=1)


def _multiscale_pool(u, pool_w, pool_scale):
    B_, S_, _ = u.shape
    ug = u.reshape(B_, S_, N_POOL_GROUPS, POOL_GROUP_WIDTH).astype(jnp.float32)
    cs = jnp.concatenate([jnp.zeros_like(ug[:, :1]), jnp.cumsum(ug, axis=1)], axis=1)
    win = jnp.array(POOL_WINDOWS, jnp.int32)
    t = jnp.arange(S_, dtype=jnp.int32)[:, None]
    start = jnp.maximum(t + 1 - win[None, :], 0)
    count = (t + 1 - start).astype(jnp.float32)
    g_idx = jnp.arange(N_POOL_GROUPS)[None, :]
    window_sum = cs[:, 1:] - cs[:, start, g_idx]
    pooled = (window_sum / count[None, :, :, None] - ug).astype(u.dtype)
    y = jnp.einsum('bsgc,gcd->bsgd', pooled, pool_w)
    return y.reshape(B_, S_, POOL_WIDTH) * pool_scale


def _hier_moe(h, rg_w, rg_b, re_w, re_b, w_gate, w_up, w_down):
    B_, S_, D = h.shape
    T = B_ * S_
    xt = h.reshape(T, D)
    gl = (xt @ rg_w).astype(jnp.float32) + rg_b
    gp = jax.nn.softmax(gl, axis=-1)
    p_g, g_sel = lax.top_k(gp, 1)
    el = (xt @ re_w).astype(jnp.float32).reshape(T, N_EXPERT_GROUPS, EXPERTS_PER_GROUP) + re_b
    el_sel = jnp.take_along_axis(el, g_sel[:, :, None], axis=1)[:, 0]
    top_v, top_j = lax.top_k(el_sel, TOP_K_INNER)
    gates = p_g * jax.nn.softmax(top_v, axis=-1)
    eid = g_sel * EXPERTS_PER_GROUP + top_j

    A = T * TOP_K_INNER
    flat_e = eid.reshape(A)
    flat_tok = jnp.arange(A) // TOP_K_INNER
    flat_gate = gates.reshape(A)
    order = jnp.argsort(flat_e)
    se = flat_e[order]
    stok = flat_tok[order]
    counts = jnp.bincount(flat_e, length=N_EXPERTS)
    starts = jnp.cumsum(counts) - counts
    padded = ((counts + MOE_ROWS - 1) // MOE_ROWS) * MOE_ROWS
    pends = jnp.cumsum(padded)
    pstarts = pends - padded
    dest = pstarts[se] + (jnp.arange(A) - starts[se])
    n_blocks = -(-A // MOE_ROWS) + N_EXPERTS
    rows = n_blocks * MOE_ROWS
    x_pad = jnp.zeros((rows, D), xt.dtype).at[dest].set(xt[stok])
    block_e = jnp.minimum(
        jnp.searchsorted(pends, jnp.arange(n_blocks) * MOE_ROWS, side='right'), N_EXPERTS - 1)

    def expert_block(args):
        xb, e = args
        return (jax.nn.silu(xb @ w_gate[e]) * (xb @ w_up[e])) @ w_down[e]

    y_pad = lax.map(expert_block, (x_pad.reshape(n_blocks, MOE_ROWS, D), block_e)).reshape(rows, D)
    y_sorted = y_pad[dest] * flat_gate[order][:, None].astype(y_pad.dtype)
    out = jax.ops.segment_sum(y_sorted, stok, num_segments=T)
    return out.reshape(B_, S_, D)


def setup_inputs(seed: int = 0) -> dict:
    key = jax.random.key(seed)
    ks = jax.random.split(key, 24)
    L, D = DEPTH, D_MODEL

    def nrm(k, shape, s):
        return jax.random.normal(k, shape, jnp.float32) * s

    return {
        "x": nrm(ks[0], (BATCH, SEQ, D), 1.0),
        "c": nrm(ks[1], (BATCH, D), 1.0),
        "ada_w": nrm(ks[2], (L, D, 6 * D), D ** -0.5),
        "ada_b": nrm(ks[3], (L, 6 * D), 0.02),
        "norm1_g": 1.0 + nrm(ks[4], (L, D), 0.02),
        "w_in": nrm(ks[5], (L, D, IN_PROJ_WIDTH), D ** -0.5),
        "lambda_q1": nrm(ks[6], (L, DIFF_HEAD_DIM), 0.1),
        "lambda_k1": nrm(ks[7], (L, DIFF_HEAD_DIM), 0.1),
        "lambda_q2": nrm(ks[8], (L, DIFF_HEAD_DIM), 0.1),
        "lambda_k2": nrm(ks[9], (L, DIFF_HEAD_DIM), 0.1),
        "subln_g": 1.0 + nrm(ks[10], (L, DIFF_V_DIM), 0.02),
        "pool_w": nrm(ks[11], (L, N_POOL_GROUPS, POOL_GROUP_WIDTH, POOL_GROUP_WIDTH), POOL_GROUP_WIDTH ** -0.5),
        "pool_scale": 1.0 + nrm(ks[12], (L, POOL_WIDTH), 0.02),
        "w_out": nrm(ks[13], (L, MIX_WIDTH, D), MIX_WIDTH ** -0.5),
        "norm2_g": 1.0 + nrm(ks[14], (L, D), 0.02),
        "router_group_w": nrm(ks[15], (L, D, N_EXPERT_GROUPS), D ** -0.5),
        "router_group_b": nrm(ks[16], (L, N_EXPERT_GROUPS), 0.01),
        "router_expert_w": nrm(ks[17], (L, D, N_EXPERTS), D ** -0.5),
        "router_expert_b": nrm(ks[18], (L, N_EXPERT_GROUPS, EXPERTS_PER_GROUP), 0.01),
        "expert_w_gate": nrm(ks[19], (L, N_EXPERTS, D, EXPERT_FF), D ** -0.5),
        "expert_w_up": nrm(ks[20], (L, N_EXPERTS, D, EXPERT_FF), D ** -0.5),
        "expert_w_down": nrm(ks[21], (L, N_EXPERTS, EXPERT_FF, D), EXPERT_FF ** -0.5),
        "final_norm_g": 1.0 + nrm(ks[22], (D,), 0.02),
    }


def reference(x, c, ada_w, ada_b, norm1_g, w_in, lambda_q1, lambda_k1, lambda_q2, lambda_k2,
              subln_g, pool_w, pool_scale, w_out, norm2_g, router_group_w, router_group_b,
              router_expert_w, router_expert_b, expert_w_gate, expert_w_up, expert_w_down,
              final_norm_g):
    B_, S_, D = x.shape
    slopes = _alibi_slopes(N_DIFF_HEADS)
    c_act = jax.nn.silu(c)
    for l in range(DEPTH):
        lambda_init = 0.8 - 0.6 * math.exp(-0.3 * l)
        mod = (c_act @ ada_w[l] + ada_b[l])[:, None, :]
        sh1, sc1, g1, sh2, sc2, g2 = jnp.split(mod, 6, axis=-1)

        h = _rmsnorm(x, norm1_g[l]) * (1 + sc1) + sh1
        proj = h @ w_in[l]
        q, k, v, u = jnp.split(
            proj, [ATTN_QK_WIDTH, 2 * ATTN_QK_WIDTH, 2 * ATTN_QK_WIDTH + ATTN_WIDTH], axis=-1)
        q = q.reshape(B_, S_, N_DIFF_HEADS, 2, DIFF_HEAD_DIM)
        k = k.reshape(B_, S_, N_DIFF_HEADS, 2, DIFF_HEAD_DIM)
        v = v.reshape(B_, S_, N_DIFF_HEADS, DIFF_V_DIM)
        lam = (jnp.exp(jnp.sum(lambda_q1[l].astype(jnp.float32) * lambda_k1[l].astype(jnp.float32)))
               - jnp.exp(jnp.sum(lambda_q2[l].astype(jnp.float32) * lambda_k2[l].astype(jnp.float32)))
               + lambda_init)
        o_attn = _diff_attention(q, k, v, lam, slopes)
        o_attn = (_rmsnorm(o_attn, subln_g[l]) * (1.0 - lambda_init)).reshape(B_, S_, ATTN_WIDTH)
        o_pool = _multiscale_pool(u, pool_w[l], pool_scale[l])
        mixed = jnp.concatenate([o_attn, o_pool], axis=-1) @ w_out[l]
        x = x + g1 * mixed

        h2 = _rmsnorm(x, norm2_g[l]) * (1 + sc2) + sh2
        y = _hier_moe(h2, router_group_w[l], router_group_b[l], router_expert_w[l],
                      router_expert_b[l], expert_w_gate[l], expert_w_up[l], expert_w_down[l])
        x = x + g2 * y
    return _rmsnorm(x, final_norm_g)
```

```python
import functools
import math

import jax
import jax.numpy as jnp
from jax import lax
from jax.experimental import pallas as pl
from jax.experimental.pallas import tpu as pltpu

F32 = jnp.float32
BF16 = jnp.bfloat16

D_MODEL = 4096
ATTN_WIDTH = 2048
POOL_WIDTH = 2048
HEAD_DIM = 128
V_DIM = 2 * HEAD_DIM
N_HEADS = ATTN_WIDTH // V_DIM
QK_WIDTH = N_HEADS * 2 * HEAD_DIM
POOL_WINDOWS = (2, 4, 8, 16)
N_POOL_GROUPS = len(POOL_WINDOWS)
POOL_GROUP_WIDTH = POOL_WIDTH // N_POOL_GROUPS
IN_PROJ_WIDTH = 2 * QK_WIDTH + ATTN_WIDTH + POOL_WIDTH
N_EXPERT_GROUPS = 4
EXPERTS_PER_GROUP = 8
N_EXPERTS = N_EXPERT_GROUPS * EXPERTS_PER_GROUP
TOP_K = 2
EXPERT_FF = 1536
RMS_EPS = 1e-6
LAMBDA_INIT = 0.8 - 0.6 * math.exp(-0.3 * 0)

LANES = 128
VMEM_LIMIT = 56 * 1024 * 1024
NEG_BIG = -1e30

ADA_TN = 512
NORM_TM = 256
MM_TM = 1024
MM_TN = 1024
ATT_T = 512
POOL_TM = 512
POOL_HALO = 16
OUT_TM = 512
OUT_TN = 1024
ROUTE_LANES = LANES
EXPERT_LANE0 = N_EXPERT_GROUPS
MOE_RC = 512
MOE_SB = 256
MOE_TF = 256
MOE_NF = EXPERT_FF // MOE_TF
MOE_TN = 512
MOE_NN = D_MODEL // MOE_TN
COMB_TM = 256


def _cparams(sem):
    return pltpu.CompilerParams(dimension_semantics=sem, vmem_limit_bytes=VMEM_LIMIT)


def _ada_kernel(c_ref, w_ref, b_ref, o_ref):
    d, tn = w_ref.shape
    ch = 256
    acc = jnp.zeros((8, tn), F32)
    for r in range(d // ch):
        cc = c_ref[r * ch:(r + 1) * ch, :]
        cc = cc * jax.nn.sigmoid(cc)
        w = w_ref[r * ch:(r + 1) * ch, :]
        acc = acc + (w * cc).reshape(ch // 8, 8, tn).sum(axis=0)
    o_ref[...] = acc.sum(axis=0, keepdims=True) + b_ref[...]


def _ada_mod(c, ada_w, ada_b):
    d, n = ada_w.shape
    return pl.pallas_call(
        _ada_kernel,
        grid=(n // ADA_TN,),
        in_specs=[
            pl.BlockSpec((d, 1), lambda j: (0, 0)),
            pl.BlockSpec((d, ADA_TN), lambda j: (0, j)),
            pl.BlockSpec((1, ADA_TN), lambda j: (0, j)),
        ],
        out_specs=pl.BlockSpec((1, ADA_TN), lambda j: (0, j)),
        out_shape=jax.ShapeDtypeStruct((1, n), F32),
        compiler_params=_cparams(("arbitrary",)),
        name="ada_mod",
    )(c.reshape(d, 1), ada_w, ada_b.reshape(1, n))


def _norm_mod(x, g, sc, sh):
    ms = jnp.mean(x * x, axis=-1, keepdims=True)
    return (x * lax.rsqrt(ms + RMS_EPS)) * g * (1.0 + sc) + sh


def _norm1_kernel(x_ref, g_ref, sc_ref, sh_ref, o_ref):
    o_ref[...] = _norm_mod(x_ref[...], g_ref[...], sc_ref[...], sh_ref[...]).astype(o_ref.dtype)


def _mod_spec(row):
    return pl.BlockSpec((None, 1, D_MODEL), lambda i, row=row: (row, 0, 0))


def _norm1(x2d, g, mod3):
    t, d = x2d.shape
    return pl.pallas_call(
        _norm1_kernel,
        grid=(t // NORM_TM,),
        in_specs=[
            pl.BlockSpec((NORM_TM, d), lambda i: (i, 0)),
            pl.BlockSpec((1, d), lambda i: (0, 0)),
            _mod_spec(1),
            _mod_spec(0),
        ],
        out_specs=pl.BlockSpec((NORM_TM, d), lambda i: (i, 0)),
        out_shape=jax.ShapeDtypeStruct((t, d), BF16),
        compiler_params=_cparams(("arbitrary",)),
        name="norm1_mod",
    )(x2d, g.reshape(1, d), mod3, mod3)


def _inproj_kernel(h_ref, w_ref, o_ref, *, n_q_tiles, scale):
    acc = jnp.dot(h_ref[...], w_ref[...], preferred_element_type=F32)
    s = jnp.where(pl.program_id(1) < n_q_tiles, scale, 1.0).astype(F32)
    o_ref[...] = (acc * s).astype(o_ref.dtype)


def _inproj(h, w_bf):
    t, d = h.shape
    n = w_bf.shape[1]
    kern = functools.partial(_inproj_kernel, n_q_tiles=QK_WIDTH // MM_TN, scale=HEAD_DIM ** -0.5)
    return pl.pallas_call(
        kern,
        grid=(t // MM_TM, n // MM_TN),
        in_specs=[
            pl.BlockSpec((MM_TM, d), lambda i, j: (i, 0)),
            pl.BlockSpec((d, MM_TN), lambda i, j: (0, j)),
        ],
        out_specs=pl.BlockSpec((MM_TM, MM_TN), lambda i, j: (i, j)),
        out_shape=jax.ShapeDtypeStruct((t, n), BF16),
        compiler_params=_cparams(("arbitrary", "arbitrary")),
        name="in_proj",
    )(h, w_bf)


def _attn_kernel(slopes_ref, q_ref, k_ref, v_ref, lq1_ref, lk1_ref, lq2_ref, lk2_ref, sg_ref,
                 o_ref, m_scr, l_scr, acc_scr):
    h = pl.program_id(0)
    i = pl.program_id(1)
    t = ATT_T
    slope = slopes_ref[h]
    q0 = i * t

    m_scr[...] = jnp.full(m_scr.shape, NEG_BIG, F32)
    l_scr[...] = jnp.zeros(l_scr.shape, F32)
    acc_scr[...] = jnp.zeros(acc_scr.shape, F32)

    def block(j, masked):
        k0 = pl.multiple_of(j * t, t)
        kblk = k_ref[pl.ds(k0, t), :]
        vblk = v_ref[pl.ds(k0, t), :]
        col = k0 + lax.broadcasted_iota(jnp.int32, (1, t), 1)
        bias = slope * (col - q0).astype(F32)
        if masked:
            row = q0 + lax.broadcasted_iota(jnp.int32, (t, 1), 0)
            keep = row >= col
        for mp in range(2):
            qm = q_ref[:, mp * HEAD_DIM:(mp + 1) * HEAD_DIM]
            km = kblk[:, mp * HEAD_DIM:(mp + 1) * HEAD_DIM]
            s = lax.dot_general(qm, km, (((1,), (1,)), ((), ())), preferred_element_type=F32) + bias
            if masked:
                s = jnp.where(keep, s, NEG_BIG)
            m_prev = m_scr[mp]
            m_new = jnp.maximum(m_prev, jnp.max(s, axis=1, keepdims=True))
            alpha = jnp.exp(m_prev - m_new)
            p = jnp.exp(s - m_new[:, :1])
            l_scr[mp] = alpha * l_scr[mp] + jnp.sum(p, axis=1, keepdims=True)
            m_scr[mp] = m_new
            acc_scr[mp] = acc_scr[mp] * alpha[:, :1] + jnp.dot(
                p.astype(BF16), vblk, preferred_element_type=F32)

    def body(j, c):
        block(j, False)
        return c

    lax.fori_loop(0, i, body, 0)
    block(i, True)

    lam = (jnp.exp(jnp.sum(lq1_ref[...] * lk1_ref[...], axis=1, keepdims=True))
           - jnp.exp(jnp.sum(lq2_ref[...] * lk2_ref[...], axis=1, keepdims=True))
           + LAMBDA_INIT)
    o1 = acc_scr[0] / l_scr[0][:, :1]
    o2 = acc_scr[1] / l_scr[1][:, :1]
    o = o1 - lam * o2
    ms = jnp.mean(o * o, axis=-1, keepdims=True)
    o = (o * lax.rsqrt(ms + RMS_EPS)) * sg_ref[...] * (1.0 - LAMBDA_INIT)
    o_ref[...] = o.astype(o_ref.dtype)


def _attention(proj, slopes, lq1, lk1, lq2, lk2, subln_g):
    s_len = proj.shape[0]
    t = ATT_T
    kcol0 = QK_WIDTH // V_DIM
    vcol0 = 2 * QK_WIDTH // V_DIM
    vec = lambda: pl.BlockSpec((1, HEAD_DIM), lambda h, i, sl: (0, 0))
    grid_spec = pltpu.PrefetchScalarGridSpec(
        num_scalar_prefetch=1,
        grid=(N_HEADS, s_len // t),
        in_specs=[
            pl.BlockSpec((t, V_DIM), lambda h, i, sl: (i, h)),
            pl.BlockSpec((s_len, V_DIM), lambda h, i, sl: (0, kcol0 + h)),
            pl.BlockSpec((s_len, V_DIM), lambda h, i, sl: (0, vcol0 + h)),
            vec(), vec(), vec(), vec(),
            pl.BlockSpec((1, V_DIM), lambda h, i, sl: (0, 0)),
        ],
        out_specs=pl.BlockSpec((t, V_DIM), lambda h, i, sl: (i, h)),
        scratch_shapes=[
            pltpu.VMEM((2, t, LANES), F32),
            pltpu.VMEM((2, t, LANES), F32),
            pltpu.VMEM((2, t, V_DIM), F32),
        ],
    )
    return pl.pallas_call(
        _attn_kernel,
        grid_spec=grid_spec,
        out_shape=jax.ShapeDtypeStruct((s_len, ATTN_WIDTH), BF16),
        compiler_params=_cparams(("arbitrary", "arbitrary")),
        name="diff_attn",
    )(slopes, proj, proj, proj, lq1, lk1, lq2, lk2, subln_g)


def _pool_kernel(u_ref, halo_ref, w_ref, sc_ref, o_ref, ext_scr):
    i = pl.program_id(0)
    tm = POOL_TM
    hl = POOL_HALO
    halo = halo_ref[...].astype(F32)
    ext_scr[0:hl, :] = jnp.where(i > 0, halo, 0.0)
    ext_scr[hl:hl + tm, :] = u_ref[...].astype(F32)
    pos = i * tm + lax.broadcasted_iota(jnp.int32, (tm, 1), 0)
    for g, win in enumerate(POOL_WINDOWS):
        c0, c1 = g * POOL_GROUP_WIDTH, (g + 1) * POOL_GROUP_WIDTH
        tok = ext_scr[hl:hl + tm, c0:c1]
        wsum = tok
        for dlt in range(1, win):
            wsum = wsum + ext_scr[hl - dlt:hl - dlt + tm, c0:c1]
        count = jnp.minimum(pos + 1, win).astype(F32)
        pooled = wsum / count - tok
        y = jnp.dot(pooled.astype(BF16), w_ref[g], preferred_element_type=F32)
        o_ref[:, c0:c1] = (y * sc_ref[:, c0:c1]).astype(o_ref.dtype)


def _pool(proj, pool_w_bf, pool_scale):
    s_len = proj.shape[0]
    ucol = (2 * QK_WIDTH + ATTN_WIDTH) // POOL_WIDTH
    rb = POOL_TM // POOL_HALO
    return pl.pallas_call(
        _pool_kernel,
        grid=(s_len // POOL_TM,),
        in_specs=[
            pl.BlockSpec((POOL_TM, POOL_WIDTH), lambda i: (i, ucol)),
            pl.BlockSpec((POOL_HALO, POOL_WIDTH), lambda i: (jnp.maximum(i * rb - 1, 0), ucol)),
            pl.BlockSpec((N_POOL_GROUPS, POOL_GROUP_WIDTH, POOL_GROUP_WIDTH), lambda i: (0, 0, 0)),
            pl.BlockSpec((1, POOL_WIDTH), lambda i: (0, 0)),
        ],
        out_specs=pl.BlockSpec((POOL_TM, POOL_WIDTH), lambda i: (i, 0)),
        out_shape=jax.ShapeDtypeStruct((s_len, POOL_WIDTH), BF16),
        scratch_shapes=[pltpu.VMEM((POOL_HALO + POOL_TM, POOL_WIDTH), F32)],
        compiler_params=_cparams(("arbitrary",)),
        name="pool_mixer",
    )(proj, proj, pool_w_bf, pool_scale.reshape(1, POOL_WIDTH))


def _outproj_kernel(a_ref, p_ref, wa_ref, wp_ref, x_ref, g_ref, o_ref):
    acc = jnp.dot(a_ref[...], wa_ref[...], preferred_element_type=F32)
    acc = acc + jnp.dot(p_ref[...], wp_ref[...], preferred_element_type=F32)
    o_ref[...] = x_ref[...] + g_ref[...] * acc


def _outproj(o_attn, o_pool, w_out_bf, x2d, mod3):
    t, d = x2d.shape
    return pl.pallas_call(
        _outproj_kernel,
        grid=(t // OUT_TM, d // OUT_TN),
        in_specs=[
            pl.BlockSpec((OUT_TM, ATTN_WIDTH), lambda i, j: (i, 0)),
            pl.BlockSpec((OUT_TM, POOL_WIDTH), lambda i, j: (i, 0)),
            pl.BlockSpec((ATTN_WIDTH, OUT_TN), lambda i, j: (0, j)),
            pl.BlockSpec((POOL_WIDTH, OUT_TN), lambda i, j: (1, j)),
            pl.BlockSpec((OUT_TM, OUT_TN), lambda i, j: (i, j)),
            pl.BlockSpec((None, 1, OUT_TN), lambda i, j: (2, 0, j)),
        ],
        out_specs=pl.BlockSpec((OUT_TM, OUT_TN), lambda i, j: (i, j)),
        out_shape=jax.ShapeDtypeStruct((t, d), F32),
        compiler_params=_cparams(("arbitrary", "arbitrary")),
        name="out_proj",
    )(o_attn, o_pool, w_out_bf, w_out_bf, x2d, mod3)


def _route_kernel(x_ref, g_ref, sc_ref, sh_ref, whi_ref, wlo_ref, rb_ref,
                  h_ref, route_ref, cnt_ref, base_scr):
    step = pl.program_id(0)
    tm = NORM_TM

    @pl.when(step == 0)
    def _():
        base_scr[...] = jnp.zeros(base_scr.shape, F32)

    h2 = _norm_mod(x_ref[...], g_ref[...], sc_ref[...], sh_ref[...])
    h_ref[...] = h2

    h_hi = h2.astype(BF16)
    h_lo = (h2 - h_hi.astype(F32)).astype(BF16)
    w_hi = whi_ref[...]
    lg = (jnp.dot(h_hi, w_hi, preferred_element_type=F32)
          + jnp.dot(h_lo, w_hi, preferred_element_type=F32)
          + jnp.dot(h_hi, wlo_ref[...], preferred_element_type=F32)) + rb_ref[...]

    lane = lax.broadcasted_iota(jnp.int32, (tm, ROUTE_LANES), 1)

    def first_max(vals):
        v = jnp.max(vals, axis=1, keepdims=True)
        idx = jnp.min(jnp.where(vals == v, lane, ROUTE_LANES), axis=1, keepdims=True)
        return v, idx

    gl = jnp.where(lane < N_EXPERT_GROUPS, lg, NEG_BIG)
    gmax, g_sel = first_max(gl)
    p_g = 1.0 / jnp.sum(jnp.exp(gl - gmax), axis=1, keepdims=True)

    e_lo = EXPERT_LANE0 + g_sel * EXPERTS_PER_GROUP
    el = jnp.where(jnp.logical_and(lane >= e_lo, lane < e_lo + EXPERTS_PER_GROUP), lg, NEG_BIG)
    v1, j1 = first_max(el)
    el2 = jnp.where(lane == j1, NEG_BIG, el)
    v2, j2 = first_max(el2)
    e2 = jnp.exp(v2 - v1)
    gate1 = p_g / (1.0 + e2)
    gate2 = p_g * e2 / (1.0 + e2)

    oh1 = (lane == j1).astype(BF16)
    oh2 = (lane == j2).astype(BF16)
    r_i = lax.broadcasted_iota(jnp.int32, (tm, tm), 0)
    c_i = lax.broadcasted_iota(jnp.int32, (tm, tm), 1)
    lower = (c_i < r_i).astype(BF16)
    before1 = jnp.dot(lower, oh1, preferred_element_type=F32)
    before2 = jnp.dot(lower, oh2, preferred_element_type=F32)
    oh1f = oh1.astype(F32)
    oh2f = oh2.astype(F32)
    tot1 = jnp.sum(oh1f, axis=0, keepdims=True)
    tot2 = jnp.sum(oh2f, axis=0, keepdims=True)
    base = base_scr[0:1, :]
    rank1 = jnp.sum((base + before1) * oh1f, axis=1, keepdims=True)
    rank2 = jnp.sum((base + tot1 + before2) * oh2f, axis=1, keepdims=True)
    new_base = base + tot1 + tot2
    base_scr[0:1, :] = new_base
    cnt_ref[...] = new_base

    eid1 = (j1 - EXPERT_LANE0).astype(F32)
    eid2 = (j2 - EXPERT_LANE0).astype(F32)
    packed = jnp.zeros((tm, ROUTE_LANES), F32)
    for k, val in enumerate((eid1, eid2, rank1, rank2, gate1, gate2)):
        packed = jnp.where(lane == k, val, packed)
    route_ref[...] = packed


def _norm2_route(x1, g, mod3, w_hi, w_lo, rbias):
    t, d = x1.shape
    const = lambda shape: pl.BlockSpec(shape, lambda i: (0,) * len(shape))
    return pl.pallas_call(
        _route_kernel,
        grid=(t // NORM_TM,),
        in_specs=[
            pl.BlockSpec((NORM_TM, d), lambda i: (i, 0)),
            const((1, d)),
            _mod_spec(4),
            _mod_spec(3),
            const((d, ROUTE_LANES)),
            const((d, ROUTE_LANES)),
            const((1, ROUTE_LANES)),
        ],
        out_specs=[
            pl.BlockSpec((NORM_TM, d), lambda i: (i, 0)),
            pl.BlockSpec((NORM_TM, ROUTE_LANES), lambda i: (i, 0)),
            const((1, ROUTE_LANES)),
        ],
        out_shape=[
            jax.ShapeDtypeStruct((t, d), F32),
            jax.ShapeDtypeStruct((t, ROUTE_LANES), F32),
            jax.ShapeDtypeStruct((1, ROUTE_LANES), F32),
        ],
        scratch_shapes=[pltpu.VMEM((8, ROUTE_LANES), F32)],
        compiler_params=_cparams(("arbitrary",)),
        name="norm2_route",
    )(x1, g.reshape(1, d), mod3, mod3, w_hi, w_lo, rbias)


def _row_gather(idx_ref, n_rows, src_hbm, dst, sem):
    def issue(r, c):
        pltpu.make_async_copy(src_hbm.at[pl.ds(idx_ref[0, r], 1), :], dst.at[pl.ds(r, 1), :], sem).start()
        return c

    lax.fori_loop(0, n_rows, issue, 0)
    pltpu.make_async_copy(src_hbm.at[pl.ds(0, n_rows), :], dst, sem).wait()


def _gather_kernel(nact_ref, idx_ref, src_hbm, o_ref, buf, sem):
    active = pl.program_id(0) < nact_ref[0]

    @pl.when(active)
    def _():
        _row_gather(idx_ref, MOE_RC, src_hbm, buf, sem)
        o_ref[...] = buf[...].astype(o_ref.dtype)

    @pl.when(jnp.logical_not(active))
    def _():
        o_ref[...] = jnp.zeros(o_ref.shape, o_ref.dtype)


def _gather_rows(nact, src_tok3, h2):
    nb = src_tok3.shape[0]
    d = h2.shape[1]
    clamp = lambda b, na: jnp.minimum(b, na[0] - 1)
    grid_spec = pltpu.PrefetchScalarGridSpec(
        num_scalar_prefetch=1,
        grid=(nb,),
        in_specs=[
            pl.BlockSpec((None, 1, MOE_RC), lambda b, na: (clamp(b, na), 0, 0), memory_space=pltpu.SMEM),
            pl.BlockSpec(memory_space=pl.ANY),
        ],
        out_specs=pl.BlockSpec((MOE_RC, d), lambda b, na: (b, 0)),
        scratch_shapes=[pltpu.VMEM((MOE_RC, d), F32), pltpu.SemaphoreType.DMA(())],
    )
    return pl.pallas_call(
        _gather_kernel,
        grid_spec=grid_spec,
        out_shape=jax.ShapeDtypeStruct((nb * MOE_RC, d), BF16),
        compiler_params=_cparams(("arbitrary",)),
        name="moe_gather",
    )(nact, src_tok3, h2)


def _ffn_kernel(be_ref, br_ref, nact_ref, x_ref, wg_ref, wu_ref, wd_ref, o_ref,
                h_scr, wg_bf, wu_bf, wd_bf):
    b = pl.program_id(0)
    s = pl.program_id(1)
    active = b < nact_ref[0]
    nsub = lax.shift_right_logical(br_ref[b] + (MOE_SB - 1), int(math.log2(MOE_SB)))

    @pl.when(jnp.logical_and(active, s < MOE_NF))
    def _():
        wg_bf[...] = wg_ref[...].astype(BF16)
        wu_bf[...] = wu_ref[...].astype(BF16)

        def body(i, c):
            r0 = pl.multiple_of(i * MOE_SB, MOE_SB)
            xs = x_ref[pl.ds(r0, MOE_SB), :]
            gt = jnp.dot(xs, wg_bf[...], preferred_element_type=F32)
            up = jnp.dot(xs, wu_bf[...], preferred_element_type=F32)
            h_scr[s, pl.ds(r0, MOE_SB), :] = (jax.nn.silu(gt) * up).astype(BF16)
            return c

        lax.fori_loop(0, nsub, body, 0)

    @pl.when(jnp.logical_and(active, s >= MOE_NF))
    def _():
        wd_bf[...] = wd_ref[...].astype(BF16)

        def body(i, c):
            r0 = pl.multiple_of(i * MOE_SB, MOE_SB)
            hs = jnp.concatenate([h_scr[f, pl.ds(r0, MOE_SB), :] for f in range(MOE_NF)], axis=1)
            o_ref[pl.ds(r0, MOE_SB), :] = jnp.dot(hs, wd_bf[...], preferred_element_type=F32)
            return c

        lax.fori_loop(0, nsub, body, 0)

        def zero(i, c):
            r0 = pl.multiple_of(i * MOE_SB, MOE_SB)
            o_ref[pl.ds(r0, MOE_SB), :] = jnp.zeros((MOE_SB, MOE_TN), F32)
            return c

        lax.fori_loop(nsub, MOE_RC // MOE_SB, zero, 0)

    @pl.when(jnp.logical_and(jnp.logical_not(active), s >= MOE_NF))
    def _():
        o_ref[...] = jnp.zeros(o_ref.shape, o_ref.dtype)


def _ffn(blk_e, blk_rows, nact, x_sorted, w_gate, w_up, w_down):
    rows, d = x_sorted.shape
    nb = rows // MOE_RC
    last = MOE_NF + MOE_NN - 1

    def eff(b, s, na):
        act = b < na[0]
        return jnp.where(act, b, na[0] - 1), jnp.where(act, s, last)

    def x_map(b, s, be, br, na):
        bb, _ = eff(b, s, na)
        return (bb, 0)

    def w1_map(b, s, be, br, na):
        bb, ss = eff(b, s, na)
        return (be[bb], 0, jnp.minimum(ss, MOE_NF - 1))

    def wd_map(b, s, be, br, na):
        bb, ss = eff(b, s, na)
        return (be[bb], 0, jnp.maximum(ss - MOE_NF, 0))

    def o_map(b, s, be, br, na):
        return (b, jnp.maximum(s - MOE_NF, 0))

    grid_spec = pltpu.PrefetchScalarGridSpec(
        num_scalar_prefetch=3,
        grid=(nb, MOE_NF + MOE_NN),
        in_specs=[
            pl.BlockSpec((MOE_RC, d), x_map),
            pl.BlockSpec((None, d, MOE_TF), w1_map),
            pl.BlockSpec((None, d, MOE_TF), w1_map),
            pl.BlockSpec((None, EXPERT_FF, MOE_TN), wd_map),
        ],
        out_specs=pl.BlockSpec((MOE_RC, MOE_TN), o_map),
        scratch_shapes=[
            pltpu.VMEM((MOE_NF, MOE_RC, MOE_TF), BF16),
            pltpu.VMEM((d, MOE_TF), BF16),
            pltpu.VMEM((d, MOE_TF), BF16),
            pltpu.VMEM((EXPERT_FF, MOE_TN), BF16),
        ],
    )
    return pl.pallas_call(
        _ffn_kernel,
        grid_spec=grid_spec,
        out_shape=jax.ShapeDtypeStruct((rows, d), F32),
        compiler_params=_cparams(("arbitrary", "arbitrary")),
        name="moe_ffn",
    )(blk_e, blk_rows, nact, x_sorted, w_gate, w_up, w_down)


def _combine_kernel(d0_ref, d1_ref, x_ref, route_ref, g2_ref, fg_ref, y_hbm, o_ref, buf0, buf1, sem0, sem1):
    tm = COMB_TM

    def issue(r, c):
        pltpu.make_async_copy(y_hbm.at[pl.ds(d0_ref[0, r], 1), :], buf0.at[pl.ds(r, 1), :], sem0).start()
        pltpu.make_async_copy(y_hbm.at[pl.ds(d1_ref[0, r], 1), :], buf1.at[pl.ds(r, 1), :], sem1).start()
        return c

    lax.fori_loop(0, tm, issue, 0)
    pltpu.make_async_copy(y_hbm.at[pl.ds(0, tm), :], buf0, sem0).wait()
    pltpu.make_async_copy(y_hbm.at[pl.ds(0, tm), :], buf1, sem1).wait()
    route = route_ref[...]
    y = buf0[...] * route[:, 4:5] + buf1[...] * route[:, 5:6]
    x2 = x_ref[...] + g2_ref[...] * y
    ms = jnp.mean(x2 * x2, axis=-1, keepdims=True)
    o_ref[...] = (x2 * lax.rsqrt(ms + RMS_EPS)) * fg_ref[...]


def _combine(dest0, dest1, x1, route, mod3, final_g, y_sorted):
    t, d = x1.shape
    nb = t // COMB_TM
    smem_idx = lambda: pl.BlockSpec((None, 1, COMB_TM), lambda i: (i, 0, 0), memory_space=pltpu.SMEM)
    return pl.pallas_call(
        _combine_kernel,
        grid=(nb,),
        in_specs=[
            smem_idx(),
            smem_idx(),
            pl.BlockSpec((COMB_TM, d), lambda i: (i, 0)),
            pl.BlockSpec((COMB_TM, ROUTE_LANES), lambda i: (i, 0)),
            _mod_spec(5),
            pl.BlockSpec((1, d), lambda i: (0, 0)),
            pl.BlockSpec(memory_space=pl.ANY),
        ],
        out_specs=pl.BlockSpec((COMB_TM, d), lambda i: (i, 0)),
        out_shape=jax.ShapeDtypeStruct((t, d), F32),
        scratch_shapes=[
            pltpu.VMEM((COMB_TM, d), F32),
            pltpu.VMEM((COMB_TM, d), F32),
            pltpu.SemaphoreType.DMA(()),
            pltpu.SemaphoreType.DMA(()),
        ],
        compiler_params=_cparams(("arbitrary",)),
        name="moe_combine",
    )(dest0.reshape(nb, 1, COMB_TM), dest1.reshape(nb, 1, COMB_TM), x1, route, mod3,
      final_g.reshape(1, d), y_sorted)


def _dispatch_plan(route, cnt, n_tokens):
    nb = (n_tokens * TOP_K) // MOE_RC + N_EXPERTS
    eid = route[:, 0:2].astype(jnp.int32)
    rank = route[:, 2:4].astype(jnp.int32)
    counts = cnt[0, EXPERT_LANE0:EXPERT_LANE0 + N_EXPERTS].astype(jnp.int32)
    nblk = (counts + MOE_RC - 1) // MOE_RC
    blk_end = jnp.cumsum(nblk)
    blk_start = blk_end - nblk
    dest = blk_start[eid] * MOE_RC + rank
    nact = blk_end[-1:]
    bidx = jnp.arange(nb, dtype=jnp.int32)
    blk_e = jnp.minimum(jnp.searchsorted(blk_end, bidx, side="right"), N_EXPERTS - 1).astype(jnp.int32)
    blk_rows = jnp.clip(counts[blk_e] - (bidx - blk_start[blk_e]) * MOE_RC, 0, MOE_RC).astype(jnp.int32)
    tok = jnp.repeat(jnp.arange(n_tokens, dtype=jnp.int32), TOP_K)
    src_tok = jnp.zeros((nb * MOE_RC,), jnp.int32).at[dest.reshape(-1)].set(tok)
    return dest, nact.astype(jnp.int32), blk_e, blk_rows, src_tok.reshape(nb, 1, MOE_RC)


def kernel(x, c, ada_w, ada_b, norm1_g, w_in, lambda_q1, lambda_k1, lambda_q2, lambda_k2, subln_g,
           pool_w, pool_scale, w_out, norm2_g, router_group_w, router_group_b, router_expert_w,
           router_expert_b, expert_w_gate, expert_w_up, expert_w_down, final_norm_g):
    b_, s_, d = x.shape
    assert b_ == 1 and d == D_MODEL and ada_w.shape[0] == 1
    t = b_ * s_
    x2d = x.reshape(t, d)

    mod3 = _ada_mod(c, ada_w[0], ada_b[0]).reshape(6, 1, d)

    h = _norm1(x2d, norm1_g[0], mod3)
    proj = _inproj(h, w_in[0].astype(BF16))
    slopes = 2.0 ** (-8.0 * jnp.arange(1, N_HEADS + 1, dtype=F32) / N_HEADS)
    o_attn = _attention(proj, slopes, lambda_q1, lambda_k1, lambda_q2, lambda_k2, subln_g)
    o_pool = _pool(proj, pool_w[0].astype(BF16), pool_scale[0])
    x1 = _outproj(o_attn, o_pool, w_out[0].astype(BF16), x2d, mod3)

    rw = jnp.zeros((d, ROUTE_LANES), F32)
    rw = rw.at[:, :N_EXPERT_GROUPS].set(router_group_w[0])
    rw = rw.at[:, EXPERT_LANE0:EXPERT_LANE0 + N_EXPERTS].set(router_expert_w[0])
    rw_hi = rw.astype(BF16)
    rw_lo = (rw - rw_hi.astype(F32)).astype(BF16)
    rbias = jnp.full((1, ROUTE_LANES), NEG_BIG, F32)
    rbias = rbias.at[0, :N_EXPERT_GROUPS].set(router_group_b[0])
    rbias = rbias.at[0, EXPERT_LANE0:EXPERT_LANE0 + N_EXPERTS].set(router_expert_b[0].reshape(-1))
    h2, route, cnt = _norm2_route(x1, norm2_g[0], mod3, rw_hi, rw_lo, rbias)

    dest, nact, blk_e, blk_rows, src_tok3 = _dispatch_plan(route, cnt, t)
    x_sorted = _gather_rows(nact, src_tok3, h2)
    y_sorted = _ffn(blk_e, blk_rows, nact, x_sorted, expert_w_gate[0], expert_w_up[0], expert_w_down[0])
    out = _combine(dest[:, 0], dest[:, 1], x1, route, mod3, final_norm_g, y_sorted)
    return out.reshape(b_, s_, d)
```

```python
import functools
import math

import jax
import jax.numpy as jnp
from jax import lax
from jax.experimental import pallas as pl
from jax.experimental.pallas import tpu as pltpu

F32 = jnp.float32
BF16 = jnp.bfloat16

D_MODEL = 4096
ATTN_WIDTH = 2048
POOL_WIDTH = 2048
HEAD_DIM = 128
V_DIM = 2 * HEAD_DIM
N_HEADS = ATTN_WIDTH // V_DIM
QK_WIDTH = N_HEADS * 2 * HEAD_DIM
POOL_WINDOWS = (2, 4, 8, 16)
N_POOL_GROUPS = len(POOL_WINDOWS)
POOL_GROUP_WIDTH = POOL_WIDTH // N_POOL_GROUPS
IN_PROJ_WIDTH = 2 * QK_WIDTH + ATTN_WIDTH + POOL_WIDTH
N_EXPERT_GROUPS = 4
EXPERTS_PER_GROUP = 8
N_EXPERTS = N_EXPERT_GROUPS * EXPERTS_PER_GROUP
TOP_K = 2
EXPERT_FF = 1536
RMS_EPS = 1e-6
LAMBDA_INIT = 0.8 - 0.6 * math.exp(-0.3 * 0)

LANES = 128
VMEM_LIMIT = 56 * 1024 * 1024
NEG_BIG = -1e30
LOG2E = math.log2(math.e)

ADA_TN = 512
NORM_TM = 256
MM_TM = 1024
MM_TN = 1024
ATT_T = 512
POOL_TM = 512
POOL_HALO = 16
OUT_TM = 512
OUT_TN = 1024
ROUTE_LANES = LANES
EXPERT_LANE0 = N_EXPERT_GROUPS
MOE_RC = 1024
MOE_SB = 256
MOE_TF = 256
MOE_NF = EXPERT_FF // MOE_TF
MOE_TN = 512
MOE_NN = D_MODEL // MOE_TN
COMB_TM = 256


def _cparams(sem):
    return pltpu.CompilerParams(dimension_semantics=sem, vmem_limit_bytes=VMEM_LIMIT)


def _ada_kernel(c_ref, w_ref, b_ref, o_ref):
    d, tn = w_ref.shape
    ch = 256
    acc = jnp.zeros((8, tn), F32)
    for r in range(d // ch):
        cc = c_ref[r * ch:(r + 1) * ch, :]
        cc = cc * jax.nn.sigmoid(cc)
        w = w_ref[r * ch:(r + 1) * ch, :]
        acc = acc + (w * cc).reshape(ch // 8, 8, tn).sum(axis=0)
    o_ref[...] = acc.sum(axis=0, keepdims=True) + b_ref[...]


def _ada_mod(c, ada_w, ada_b):
    d, n = ada_w.shape
    return pl.pallas_call(
        _ada_kernel,
        grid=(n // ADA_TN,),
        in_specs=[
            pl.BlockSpec((d, 1), lambda j: (0, 0)),
            pl.BlockSpec((d, ADA_TN), lambda j: (0, j)),
            pl.BlockSpec((1, ADA_TN), lambda j: (0, j)),
        ],
        out_specs=pl.BlockSpec((1, ADA_TN), lambda j: (0, j)),
        out_shape=jax.ShapeDtypeStruct((1, n), F32),
        compiler_params=_cparams(("arbitrary",)),
        name="ada_mod",
    )(c.reshape(d, 1), ada_w, ada_b.reshape(1, n))


def _norm_mod(x, g, sc, sh):
    ms = jnp.mean(x * x, axis=-1, keepdims=True)
    return (x * lax.rsqrt(ms + RMS_EPS)) * g * (1.0 + sc) + sh


def _norm1_kernel(x_ref, g_ref, sc_ref, sh_ref, o_ref):
    o_ref[...] = _norm_mod(x_ref[...], g_ref[...], sc_ref[...], sh_ref[...]).astype(o_ref.dtype)


def _mod_spec(row):
    return pl.BlockSpec((None, 1, D_MODEL), lambda i, row=row: (row, 0, 0))


def _norm1(x2d, g, mod3):
    t, d = x2d.shape
    return pl.pallas_call(
        _norm1_kernel,
        grid=(t // NORM_TM,),
        in_specs=[
            pl.BlockSpec((NORM_TM, d), lambda i: (i, 0)),
            pl.BlockSpec((1, d), lambda i: (0, 0)),
            _mod_spec(1),
            _mod_spec(0),
        ],
        out_specs=pl.BlockSpec((NORM_TM, d), lambda i: (i, 0)),
        out_shape=jax.ShapeDtypeStruct((t, d), BF16),
        compiler_params=_cparams(("arbitrary",)),
        name="norm1_mod",
    )(x2d, g.reshape(1, d), mod3, mod3)


def _inproj_kernel(h_ref, w_ref, o_ref, *, n_q_tiles, scale):
    acc = jnp.dot(h_ref[...], w_ref[...], preferred_element_type=F32)
    s = jnp.where(pl.program_id(1) < n_q_tiles, scale, 1.0).astype(F32)
    o_ref[...] = (acc * s).astype(o_ref.dtype)


def _inproj(h, w_bf):
    t, d = h.shape
    n = w_bf.shape[1]
    kern = functools.partial(_inproj_kernel, n_q_tiles=QK_WIDTH // MM_TN, scale=LOG2E * HEAD_DIM ** -0.5)
    return pl.pallas_call(
        kern,
        grid=(t // MM_TM, n // MM_TN),
        in_specs=[
            pl.BlockSpec((MM_TM, d), lambda i, j: (i, 0)),
            pl.BlockSpec((d, MM_TN), lambda i, j: (0, j)),
        ],
        out_specs=pl.BlockSpec((MM_TM, MM_TN), lambda i, j: (i, j)),
        out_shape=jax.ShapeDtypeStruct((t, n), BF16),
        compiler_params=_cparams(("arbitrary", "arbitrary")),
        name="in_proj",
    )(h, w_bf)


def _attn_kernel(slopes_ref, q_ref, k_ref, vt_ref, lq1_ref, lk1_ref, lq2_ref, lk2_ref, sg_ref,
                 o_ref, m_scr, l_scr, acc_scr, s_scr):
    h = pl.program_id(0)
    i = pl.program_id(1)
    t = ATT_T
    slope2 = slopes_ref[h] * LOG2E
    q0 = i * t

    m_scr[...] = jnp.full(m_scr.shape, NEG_BIG, F32)
    l_scr[...] = jnp.zeros(l_scr.shape, F32)
    acc_scr[...] = jnp.zeros(acc_scr.shape, F32)

    def colreduce(x, op):
        part = op(x.reshape(t // 8, 8, t), axis=0)
        return jnp.broadcast_to(op(part, axis=0, keepdims=True), (8, t))

    def scores(j, buf):
        k0 = pl.multiple_of(j * t, t)
        kblk = k_ref[pl.ds(k0, t), :]
        for mp in range(2):
            qm = q_ref[:, mp * HEAD_DIM:(mp + 1) * HEAD_DIM]
            km = kblk[:, mp * HEAD_DIM:(mp + 1) * HEAD_DIM]
            s_scr[buf, mp] = lax.dot_general(km, qm, (((1,), (1,)), ((), ())),
                                             preferred_element_type=F32)

    def update(j, buf, masked):
        vtb = vt_ref[j]
        kidx = lax.broadcasted_iota(jnp.int32, (t, LANES), 0)
        bias = jnp.tile(slope2 * (kidx + (j * t - q0)).astype(F32), (1, t // LANES))
        if masked:
            keep = (lax.broadcasted_iota(jnp.int32, (t, t), 1)
                    >= lax.broadcasted_iota(jnp.int32, (t, t), 0))
        for mp in range(2):
            s = s_scr[buf, mp] + bias
            if masked:
                s = jnp.where(keep, s, NEG_BIG)
            m_prev = m_scr[mp]
            m_new = jnp.maximum(m_prev, colreduce(s, jnp.max))
            alpha = jnp.exp2(m_prev - m_new)
            p = jnp.exp2(s - jnp.tile(m_new, (t // 8, 1)))
            l_scr[mp] = alpha * l_scr[mp] + colreduce(p, jnp.sum)
            m_scr[mp] = m_new
            acc_scr[mp] = acc_scr[mp] * jnp.tile(alpha, (V_DIM // 8, 1)) + jnp.dot(
                vtb, p.astype(BF16), preferred_element_type=F32)

    scores(0, 0)

    def pair(p, c):
        j = 2 * p
        update(j, 0, False)
        scores(j + 1, 1)
        update(j + 1, 1, False)
        scores(j + 2, 0)
        return c

    lax.fori_loop(0, lax.shift_right_logical(i, 1), pair, 0)
    odd = (i & 1) == 1

    @pl.when(odd)
    def _():
        update(i - 1, 0, False)
        scores(i, 1)
        update(i, 1, True)

    @pl.when(jnp.logical_not(odd))
    def _():
        update(i, 0, True)

    lam = (jnp.exp(jnp.sum(lq1_ref[...] * lk1_ref[...], axis=1, keepdims=True))
           - jnp.exp(jnp.sum(lq2_ref[...] * lk2_ref[...], axis=1, keepdims=True))
           + LAMBDA_INIT)
    o1 = acc_scr[0] / jnp.tile(l_scr[0], (V_DIM // 8, 1))
    o2 = acc_scr[1] / jnp.tile(l_scr[1], (V_DIM // 8, 1))
    o = o1 - lam * o2
    ms = jnp.mean(o * o, axis=0, keepdims=True)
    gain = jnp.tile(sg_ref[...], (1, t // LANES)) * (1.0 - LAMBDA_INIT)
    o = (o * lax.rsqrt(ms + RMS_EPS)) * gain
    o_ref[...] = o.T.astype(o_ref.dtype)


def _attention(proj, slopes, lq1, lk1, lq2, lk2, subln_g):
    s_len = proj.shape[0]
    t = ATT_T
    nblk = s_len // t
    kcol0 = QK_WIDTH // V_DIM
    v = proj[:, 2 * QK_WIDTH:2 * QK_WIDTH + ATTN_WIDTH]
    vt = v.reshape(nblk, t, N_HEADS, V_DIM).transpose(2, 0, 3, 1)
    gain = jnp.broadcast_to(subln_g.reshape(V_DIM, 1), (V_DIM, LANES))
    vec = lambda: pl.BlockSpec((1, HEAD_DIM), lambda h, i, sl: (0, 0))
    grid_spec = pltpu.PrefetchScalarGridSpec(
        num_scalar_prefetch=1,
        grid=(N_HEADS, nblk),
        in_specs=[
            pl.BlockSpec((t, V_DIM), lambda h, i, sl: (i, h)),
            pl.BlockSpec((s_len, V_DIM), lambda h, i, sl: (0, kcol0 + h)),
            pl.BlockSpec((None, nblk, V_DIM, t), lambda h, i, sl: (h, 0, 0, 0)),
            vec(), vec(), vec(), vec(),
            pl.BlockSpec((V_DIM, LANES), lambda h, i, sl: (0, 0)),
        ],
        out_specs=pl.BlockSpec((t, V_DIM), lambda h, i, sl: (i, h)),
        scratch_shapes=[
            pltpu.VMEM((2, 8, t), F32),
            pltpu.VMEM((2, 8, t), F32),
            pltpu.VMEM((2, V_DIM, t), F32),
            pltpu.VMEM((2, 2, t, t), F32),
        ],
    )
    return pl.pallas_call(
        _attn_kernel,
        grid_spec=grid_spec,
        out_shape=jax.ShapeDtypeStruct((s_len, ATTN_WIDTH), BF16),
        compiler_params=_cparams(("arbitrary", "arbitrary")),
        name="diff_attn",
    )(slopes, proj, proj, vt, lq1, lk1, lq2, lk2, gain)


def _pool_kernel(u_ref, halo_ref, w_ref, sc_ref, o_ref, ext_scr):
    i = pl.program_id(0)
    tm = POOL_TM
    hl = POOL_HALO
    halo = halo_ref[...].astype(F32)
    ext_scr[0:hl, :] = jnp.where(i > 0, halo, 0.0)
    ext_scr[hl:hl + tm, :] = u_ref[...].astype(F32)
    pos = i * tm + lax.broadcasted_iota(jnp.int32, (tm, 1), 0)
    for g, win in enumerate(POOL_WINDOWS):
        c0, c1 = g * POOL_GROUP_WIDTH, (g + 1) * POOL_GROUP_WIDTH
        tok = ext_scr[hl:hl + tm, c0:c1]
        wsum = tok
        for dlt in range(1, win):
            wsum = wsum + ext_scr[hl - dlt:hl - dlt + tm, c0:c1]
        count = jnp.minimum(pos + 1, win).astype(F32)
        pooled = wsum / count - tok
        y = jnp.dot(pooled.astype(BF16), w_ref[g], preferred_element_type=F32)
        o_ref[:, c0:c1] = (y * sc_ref[:, c0:c1]).astype(o_ref.dtype)


def _pool(proj, pool_w_bf, pool_scale):
    s_len = proj.shape[0]
    ucol = (2 * QK_WIDTH + ATTN_WIDTH) // POOL_WIDTH
    rb = POOL_TM // POOL_HALO
    return pl.pallas_call(
        _pool_kernel,
        grid=(s_len // POOL_TM,),
        in_specs=[
            pl.BlockSpec((POOL_TM, POOL_WIDTH), lambda i: (i, ucol)),
            pl.BlockSpec((POOL_HALO, POOL_WIDTH), lambda i: (jnp.maximum(i * rb - 1, 0), ucol)),
            pl.BlockSpec((N_POOL_GROUPS, POOL_GROUP_WIDTH, POOL_GROUP_WIDTH), lambda i: (0, 0, 0)),
            pl.BlockSpec((1, POOL_WIDTH), lambda i: (0, 0)),
        ],
        out_specs=pl.BlockSpec((POOL_TM, POOL_WIDTH), lambda i: (i, 0)),
        out_shape=jax.ShapeDtypeStruct((s_len, POOL_WIDTH), BF16),
        scratch_shapes=[pltpu.VMEM((POOL_HALO + POOL_TM, POOL_WIDTH), F32)],
        compiler_params=_cparams(("arbitrary",)),
        name="pool_mixer",
    )(proj, proj, pool_w_bf, pool_scale.reshape(1, POOL_WIDTH))


def _outproj_kernel(a_ref, p_ref, wa_ref, wp_ref, x_ref, g_ref, o_ref):
    acc = jnp.dot(a_ref[...], wa_ref[...], preferred_element_type=F32)
    acc = acc + jnp.dot(p_ref[...], wp_ref[...], preferred_element_type=F32)
    o_ref[...] = x_ref[...] + g_ref[...] * acc


def _outproj(o_attn, o_pool, w_out_bf, x2d, mod3):
    t, d = x2d.shape
    return pl.pallas_call(
        _outproj_kernel,
        grid=(t // OUT_TM, d // OUT_TN),
        in_specs=[
            pl.BlockSpec((OUT_TM, ATTN_WIDTH), lambda i, j: (i, 0)),
            pl.BlockSpec((OUT_TM, POOL_WIDTH), lambda i, j: (i, 0)),
            pl.BlockSpec((ATTN_WIDTH, OUT_TN), lambda i, j: (0, j)),
            pl.BlockSpec((POOL_WIDTH, OUT_TN), lambda i, j: (1, j)),
            pl.BlockSpec((OUT_TM, OUT_TN), lambda i, j: (i, j)),
            pl.BlockSpec((None, 1, OUT_TN), lambda i, j: (2, 0, j)),
        ],
        out_specs=pl.BlockSpec((OUT_TM, OUT_TN), lambda i, j: (i, j)),
        out_shape=jax.ShapeDtypeStruct((t, d), F32),
        compiler_params=_cparams(("arbitrary", "arbitrary")),
        name="out_proj",
    )(o_attn, o_pool, w_out_bf, w_out_bf, x2d, mod3)


def _route_kernel(x_ref, g_ref, sc_ref, sh_ref, whi_ref, wlo_ref, rb_ref,
                  h_ref, route_ref, cnt_ref, base_scr):
    step = pl.program_id(0)
    tm = NORM_TM

    @pl.when(step == 0)
    def _():
        base_scr[...] = jnp.zeros(base_scr.shape, F32)

    h2 = _norm_mod(x_ref[...], g_ref[...], sc_ref[...], sh_ref[...])
    half = D_MODEL // 2
    lo = pltpu.bitcast(h2[:, :half].astype(BF16).astype(F32), jnp.uint32)
    hi = pltpu.bitcast(h2[:, half:].astype(BF16).astype(F32), jnp.uint32)
    h_ref[...] = hi | (lo >> 16)

    h_hi = h2.astype(BF16)
    h_lo = (h2 - h_hi.astype(F32)).astype(BF16)
    w_hi = whi_ref[...]
    lg = (jnp.dot(h_hi, w_hi, preferred_element_type=F32)
          + jnp.dot(h_lo, w_hi, preferred_element_type=F32)
          + jnp.dot(h_hi, wlo_ref[...], preferred_element_type=F32)) + rb_ref[...]

    lane = lax.broadcasted_iota(jnp.int32, (tm, ROUTE_LANES), 1)

    def first_max(vals):
        v = jnp.max(vals, axis=1, keepdims=True)
        idx = jnp.min(jnp.where(vals == v, lane, ROUTE_LANES), axis=1, keepdims=True)
        return v, idx

    gl = jnp.where(lane < N_EXPERT_GROUPS, lg, NEG_BIG)
    gmax, g_sel = first_max(gl)
    p_g = 1.0 / jnp.sum(jnp.exp(gl - gmax), axis=1, keepdims=True)

    e_lo = EXPERT_LANE0 + g_sel * EXPERTS_PER_GROUP
    el = jnp.where(jnp.logical_and(lane >= e_lo, lane < e_lo + EXPERTS_PER_GROUP), lg, NEG_BIG)
    v1, j1 = first_max(el)
    el2 = jnp.where(lane == j1, NEG_BIG, el)
    v2, j2 = first_max(el2)
    e2 = jnp.exp(v2 - v1)
    gate1 = p_g / (1.0 + e2)
    gate2 = p_g * e2 / (1.0 + e2)

    oh1 = (lane == j1).astype(BF16)
    oh2 = (lane == j2).astype(BF16)
    r_i = lax.broadcasted_iota(jnp.int32, (tm, tm), 0)
    c_i = lax.broadcasted_iota(jnp.int32, (tm, tm), 1)
    lower = (c_i < r_i).astype(BF16)
    before1 = jnp.dot(lower, oh1, preferred_element_type=F32)
    before2 = jnp.dot(lower, oh2, preferred_element_type=F32)
    oh1f = oh1.astype(F32)
    oh2f = oh2.astype(F32)
    tot1 = jnp.sum(oh1f, axis=0, keepdims=True)
    tot2 = jnp.sum(oh2f, axis=0, keepdims=True)
    base = base_scr[0:1, :]
    rank1 = jnp.sum((base + before1) * oh1f, axis=1, keepdims=True)
    rank2 = jnp.sum((base + tot1 + before2) * oh2f, axis=1, keepdims=True)
    new_base = base + tot1 + tot2
    base_scr[0:1, :] = new_base
    cnt_ref[...] = new_base

    eid1 = (j1 - EXPERT_LANE0).astype(F32)
    eid2 = (j2 - EXPERT_LANE0).astype(F32)
    packed = jnp.zeros((tm, ROUTE_LANES), F32)
    for k, val in enumerate((eid1, eid2, rank1, rank2, gate1, gate2)):
        packed = jnp.where(lane == k, val, packed)
    route_ref[...] = packed


def _norm2_route(x1, g, mod3, w_hi, w_lo, rbias):
    t, d = x1.shape
    const = lambda shape: pl.BlockSpec(shape, lambda i: (0,) * len(shape))
    return pl.pallas_call(
        _route_kernel,
        grid=(t // NORM_TM,),
        in_specs=[
            pl.BlockSpec((NORM_TM, d), lambda i: (i, 0)),
            const((1, d)),
            _mod_spec(4),
            _mod_spec(3),
            const((d, ROUTE_LANES)),
            const((d, ROUTE_LANES)),
            const((1, ROUTE_LANES)),
        ],
        out_specs=[
            pl.BlockSpec((NORM_TM, d // 2), lambda i: (i, 0)),
            pl.BlockSpec((NORM_TM, ROUTE_LANES), lambda i: (i, 0)),
            const((1, ROUTE_LANES)),
        ],
        out_shape=[
            jax.ShapeDtypeStruct((t, d // 2), jnp.uint32),
            jax.ShapeDtypeStruct((t, ROUTE_LANES), F32),
            jax.ShapeDtypeStruct((1, ROUTE_LANES), F32),
        ],
        scratch_shapes=[pltpu.VMEM((8, ROUTE_LANES), F32)],
        compiler_params=_cparams(("arbitrary",)),
        name="norm2_route",
    )(x1, g.reshape(1, d), mod3, mod3, w_hi, w_lo, rbias)


META_E, META_ROW0, META_NSUB, META_MISC = 0, 1, 2, 3
MISC_NACT, MISC_TAIL_ROW0, MISC_TAIL_PIECES = 0, 1, 2


def _ffn_kernel(meta_ref, idx_ref, idxn_ref, hp_hbm, wg_ref, wu_ref, wd_ref, y_hbm,
                xw, x_bf, h_scr, wg_bf, wu_bf, wd_bf, obuf, gsem, osem):
    c = pl.program_id(0)
    s = pl.program_id(1)
    nact = meta_ref[META_MISC, MISC_NACT]
    active = c < nact
    nsub = meta_ref[META_NSUB, c]
    row0 = meta_ref[META_ROW0, c]
    half = D_MODEL // 2
    sb_shift = int(math.log2(MOE_SB))

    def start_gather(ids_ref, n_sub):
        def issue(r, carry):
            pltpu.make_async_copy(hp_hbm.at[pl.ds(ids_ref[0, r], 1), :], xw.at[pl.ds(r, 1), :], gsem).start()
            return carry

        lax.fori_loop(0, lax.shift_left(n_sub, sb_shift), issue, 0)

    def wait_gather(n_sub):
        def wait(i, carry):
            pltpu.make_async_copy(hp_hbm.at[pl.ds(0, MOE_SB), :], xw.at[pl.ds(0, MOE_SB), :], gsem).wait()
            return carry

        lax.fori_loop(0, n_sub, wait, 0)

    def out_copy(slot, i, col):
        r = pl.multiple_of(i * MOE_SB, MOE_SB)
        return pltpu.make_async_copy(
            obuf.at[slot, pl.ds(r, MOE_SB), :],
            y_hbm.at[pl.ds(pl.multiple_of(row0 + r, MOE_SB), MOE_SB), pl.ds(col * MOE_TN, MOE_TN)],
            osem.at[slot])

    def wait_out(slot, n_sub):
        def wait(i, carry):
            out_copy(slot, 0, 0).wait()
            return carry

        lax.fori_loop(0, n_sub, wait, 0)

    @pl.when(jnp.logical_and(active, s == 0))
    def _():
        @pl.when(c == 0)
        def _():
            start_gather(idx_ref, nsub)

        wait_gather(nsub)

        def unpack(i, carry):
            r0 = pl.multiple_of(i * MOE_SB, MOE_SB)
            w = xw[pl.ds(r0, MOE_SB), :]
            x_bf[pl.ds(r0, MOE_SB), 0:half] = pltpu.bitcast(w << 16, F32).astype(BF16)
            x_bf[pl.ds(r0, MOE_SB), half:D_MODEL] = pltpu.bitcast(w & jnp.uint32(0xFFFF0000), F32).astype(BF16)
            return carry

        lax.fori_loop(0, nsub, unpack, 0)

    @pl.when(jnp.logical_and(c + 1 < nact, s == 1))
    def _():
        start_gather(idxn_ref, meta_ref[META_NSUB, c + 1])

    @pl.when(jnp.logical_and(active, s < MOE_NF))
    def _():
        wg_bf[...] = wg_ref[...].astype(BF16)
        wu_bf[...] = wu_ref[...].astype(BF16)

        def body(i, carry):
            r0 = pl.multiple_of(i * MOE_SB, MOE_SB)
            xs = x_bf[pl.ds(r0, MOE_SB), :]
            gt = jnp.dot(xs, wg_bf[...], preferred_element_type=F32)
            up = jnp.dot(xs, wu_bf[...], preferred_element_type=F32)
            h_scr[s, pl.ds(r0, MOE_SB), :] = (jax.nn.silu(gt) * up).astype(BF16)
            return carry

        lax.fori_loop(0, nsub, body, 0)

    @pl.when(jnp.logical_and(active, s >= MOE_NF))
    def _():
        n = s - MOE_NF
        slot = n & 1

        @pl.when(n >= 2)
        def _():
            wait_out(slot, nsub)

        wd_bf[...] = wd_ref[...].astype(BF16)

        def body(i, carry):
            r0 = pl.multiple_of(i * MOE_SB, MOE_SB)
            hs = jnp.concatenate([h_scr[f, pl.ds(r0, MOE_SB), :] for f in range(MOE_NF)], axis=1)
            obuf[slot, pl.ds(r0, MOE_SB), :] = jnp.dot(hs, wd_bf[...], preferred_element_type=F32)
            return carry

        lax.fori_loop(0, nsub, body, 0)

        for col in range(MOE_NN):
            @pl.when(n == col)
            def _(col=col):
                def send(i, carry):
                    out_copy(slot, i, col).start()
                    return carry

                lax.fori_loop(0, nsub, send, 0)

        @pl.when(n == MOE_NN - 1)
        def _():
            wait_out(1 - slot, nsub)
            wait_out(slot, nsub)

    @pl.when(jnp.logical_and(jnp.logical_not(active), s == 0))
    def _():
        per_chunk = MOE_RC // MOE_SB
        first = (c - nact) * per_chunk
        n_here = jnp.clip(meta_ref[META_MISC, MISC_TAIL_PIECES] - first, 0, per_chunk)
        tail0 = meta_ref[META_MISC, MISC_TAIL_ROW0]
        obuf[0, 0:MOE_SB, :] = jnp.zeros((MOE_SB, MOE_TN), F32)

        def zcopy(k, col):
            r = pl.multiple_of(tail0 + (first + k) * MOE_SB, MOE_SB)
            return pltpu.make_async_copy(
                obuf.at[0, pl.ds(0, MOE_SB), :],
                y_hbm.at[pl.ds(r, MOE_SB), pl.ds(col * MOE_TN, MOE_TN)],
                osem.at[0])

        def send(k, carry):
            for col in range(MOE_NN):
                zcopy(k, col).start()
            return carry

        def wait(k, carry):
            for col in range(MOE_NN):
                zcopy(k, col).wait()
            return carry

        lax.fori_loop(0, n_here, send, 0)
        lax.fori_loop(0, n_here, wait, 0)


def _ffn(meta, idx_tab, hp, w_gate, w_up, w_down, n_rows):
    d = D_MODEL
    nc = idx_tab.shape[0]
    last = MOE_NF + MOE_NN - 1

    def eff(c, s, m):
        nact = m[META_MISC, MISC_NACT]
        act = c < nact
        return jnp.where(act, c, nact - 1), jnp.where(act, s, last)

    def idx_map(c, s, m):
        return (c, 0, 0)

    def idxn_map(c, s, m):
        return (jnp.minimum(c + 1, nc - 1), 0, 0)

    def w1_map(c, s, m):
        cc, ss = eff(c, s, m)
        return (m[META_E, cc], 0, jnp.minimum(ss, MOE_NF - 1))

    def wd_map(c, s, m):
        cc, ss = eff(c, s, m)
        return (m[META_E, cc], 0, jnp.maximum(ss - MOE_NF, 0))

    grid_spec = pltpu.PrefetchScalarGridSpec(
        num_scalar_prefetch=1,
        grid=(nc, MOE_NF + MOE_NN),
        in_specs=[
            pl.BlockSpec((None, 1, MOE_RC), idx_map, memory_space=pltpu.SMEM),
            pl.BlockSpec((None, 1, MOE_RC), idxn_map, memory_space=pltpu.SMEM),
            pl.BlockSpec(memory_space=pl.ANY),
            pl.BlockSpec((None, d, MOE_TF), w1_map),
            pl.BlockSpec((None, d, MOE_TF), w1_map),
            pl.BlockSpec((None, EXPERT_FF, MOE_TN), wd_map),
        ],
        out_specs=pl.BlockSpec(memory_space=pl.ANY),
        scratch_shapes=[
            pltpu.VMEM((MOE_RC, d // 2), jnp.uint32),
            pltpu.VMEM((MOE_RC, d), BF16),
            pltpu.VMEM((MOE_NF, MOE_RC, MOE_TF), BF16),
            pltpu.VMEM((d, MOE_TF), BF16),
            pltpu.VMEM((d, MOE_TF), BF16),
            pltpu.VMEM((EXPERT_FF, MOE_TN), BF16),
            pltpu.VMEM((2, MOE_RC, MOE_TN), F32),
            pltpu.SemaphoreType.DMA(()),
            pltpu.SemaphoreType.DMA((2,)),
        ],
    )
    return pl.pallas_call(
        _ffn_kernel,
        grid_spec=grid_spec,
        out_shape=jax.ShapeDtypeStruct((n_rows, d), F32),
        compiler_params=_cparams(("arbitrary", "arbitrary")),
        name="moe_ffn",
    )(meta, idx_tab, idx_tab, hp, w_gate, w_up, w_down)


def _combine_kernel(d0_ref, d1_ref, x_ref, route_ref, g2_ref, fg_ref, y_hbm, o_ref, buf0, buf1, sem0, sem1):
    tm = COMB_TM

    def issue(r, c):
        pltpu.make_async_copy(y_hbm.at[pl.ds(d0_ref[0, r], 1), :], buf0.at[pl.ds(r, 1), :], sem0).start()
        pltpu.make_async_copy(y_hbm.at[pl.ds(d1_ref[0, r], 1), :], buf1.at[pl.ds(r, 1), :], sem1).start()
        return c

    lax.fori_loop(0, tm, issue, 0)
    pltpu.make_async_copy(y_hbm.at[pl.ds(0, tm), :], buf0, sem0).wait()
    pltpu.make_async_copy(y_hbm.at[pl.ds(0, tm), :], buf1, sem1).wait()
    route = route_ref[...]
    y = buf0[...] * route[:, 4:5] + buf1[...] * route[:, 5:6]
    x2 = x_ref[...] + g2_ref[...] * y
    ms = jnp.mean(x2 * x2, axis=-1, keepdims=True)
    o_ref[...] = (x2 * lax.rsqrt(ms + RMS_EPS)) * fg_ref[...]


def _combine(dest0, dest1, x1, route, mod3, final_g, y_sorted):
    t, d = x1.shape
    nb = t // COMB_TM
    smem_idx = lambda: pl.BlockSpec((None, 1, COMB_TM), lambda i: (i, 0, 0), memory_space=pltpu.SMEM)
    return pl.pallas_call(
        _combine_kernel,
        grid=(nb,),
        in_specs=[
            smem_idx(),
            smem_idx(),
            pl.BlockSpec((COMB_TM, d), lambda i: (i, 0)),
            pl.BlockSpec((COMB_TM, ROUTE_LANES), lambda i: (i, 0)),
            _mod_spec(5),
            pl.BlockSpec((1, d), lambda i: (0, 0)),
            pl.BlockSpec(memory_space=pl.ANY),
        ],
        out_specs=pl.BlockSpec((COMB_TM, d), lambda i: (i, 0)),
        out_shape=jax.ShapeDtypeStruct((t, d), F32),
        scratch_shapes=[
            pltpu.VMEM((COMB_TM, d), F32),
            pltpu.VMEM((COMB_TM, d), F32),
            pltpu.SemaphoreType.DMA(()),
            pltpu.SemaphoreType.DMA(()),
        ],
        compiler_params=_cparams(("arbitrary",)),
        name="moe_combine",
    )(dest0.reshape(nb, 1, COMB_TM), dest1.reshape(nb, 1, COMB_TM), x1, route, mod3,
      final_g.reshape(1, d), y_sorted)


def _dispatch_plan(route, cnt, n_tokens):
    n_assign = n_tokens * TOP_K
    n_rows = n_assign + N_EXPERTS * MOE_SB
    nc = n_assign // MOE_RC + N_EXPERTS
    i32 = jnp.int32
    eid = route[:, 0:2].astype(i32)
    rank = route[:, 2:4].astype(i32)
    counts = cnt[0, EXPERT_LANE0:EXPERT_LANE0 + N_EXPERTS].astype(i32)
    seg_rows = (counts + MOE_SB - 1) // MOE_SB * MOE_SB
    seg_end = jnp.cumsum(seg_rows)
    seg_start = seg_end - seg_rows
    dest = seg_start[eid] + rank
    nchunk = (seg_rows + MOE_RC - 1) // MOE_RC
    ch_end = jnp.cumsum(nchunk)
    ch_start = ch_end - nchunk
    nact = ch_end[-1]
    cidx = jnp.arange(nc, dtype=i32)
    ch_e = jnp.minimum(jnp.searchsorted(ch_end, cidx, side="right"), N_EXPERTS - 1).astype(i32)
    k = cidx - ch_start[ch_e]
    ch_row0 = jnp.where(cidx < nact, seg_start[ch_e] + k * MOE_RC, 0)
    ch_nsub = jnp.where(cidx < nact, jnp.clip(seg_rows[ch_e] - k * MOE_RC, 0, MOE_RC) // MOE_SB, 0)
    misc = jnp.zeros((nc,), i32).at[MISC_NACT].set(nact)
    misc = misc.at[MISC_TAIL_ROW0].set(seg_end[-1]).at[MISC_TAIL_PIECES].set((n_rows - seg_end[-1]) // MOE_SB)
    meta = jnp.stack([ch_e, ch_row0, ch_nsub, misc]).astype(i32)
    tok = jnp.repeat(jnp.arange(n_tokens, dtype=i32), TOP_K)
    src_tok = (jnp.arange(n_rows, dtype=i32) % n_tokens).at[dest.reshape(-1)].set(tok)
    idx_tab = src_tok[jnp.minimum(ch_row0[:, None] + jnp.arange(MOE_RC, dtype=i32)[None, :], n_rows - 1)]
    return dest, meta, idx_tab.reshape(nc, 1, MOE_RC), n_rows


def kernel(x, c, ada_w, ada_b, norm1_g, w_in, lambda_q1, lambda_k1, lambda_q2, lambda_k2, subln_g,
           pool_w, pool_scale, w_out, norm2_g, router_group_w, router_group_b, router_expert_w,
           router_expert_b, expert_w_gate, expert_w_up, expert_w_down, final_norm_g):
    b_, s_, d = x.shape
    assert b_ == 1 and d == D_MODEL and ada_w.shape[0] == 1
    t = b_ * s_
    x2d = x.reshape(t, d)

    mod3 = _ada_mod(c, ada_w[0], ada_b[0]).reshape(6, 1, d)

    h = _norm1(x2d, norm1_g[0], mod3)
    proj = _inproj(h, w_in[0].astype(BF16))
    slopes = 2.0 ** (-8.0 * jnp.arange(1, N_HEADS + 1, dtype=F32) / N_HEADS)
    o_attn = _attention(proj, slopes, lambda_q1, lambda_k1, lambda_q2, lambda_k2, subln_g)
    o_pool = _pool(proj, pool_w[0].astype(BF16), pool_scale[0])
    x1 = _outproj(o_attn, o_pool, w_out[0].astype(BF16), x2d, mod3)

    rw = jnp.zeros((d, ROUTE_LANES), F32)
    rw = rw.at[:, :N_EXPERT_GROUPS].set(router_group_w[0])
    rw = rw.at[:, EXPERT_LANE0:EXPERT_LANE0 + N_EXPERTS].set(router_expert_w[0])
    rw_hi = rw.astype(BF16)
    rw_lo = (rw - rw_hi.astype(F32)).astype(BF16)
    rbias = jnp.full((1, ROUTE_LANES), NEG_BIG, F32)
    rbias = rbias.at[0, :N_EXPERT_GROUPS].set(router_group_b[0])
    rbias = rbias.at[0, EXPERT_LANE0:EXPERT_LANE0 + N_EXPERTS].set(router_expert_b[0].reshape(-1))
    h2, route, cnt = _norm2_route(x1, norm2_g[0], mod3, rw_hi, rw_lo, rbias)

    dest, meta, idx_tab, n_rows = _dispatch_plan(route, cnt, t)
    y_sorted = _ffn(meta, idx_tab, h2, expert_w_gate[0], expert_w_up[0], expert_w_down[0], n_rows)
    out = _combine(dest[:, 0], dest[:, 1], x1, route, mod3, final_norm_g, y_sorted)
    return out.reshape(b_, s_, d)
```

```python
import functools
import math

import jax
import jax.numpy as jnp
from jax import lax
from jax.experimental import pallas as pl
from jax.experimental.pallas import tpu as pltpu

F32 = jnp.float32
BF16 = jnp.bfloat16

D_MODEL = 4096
ATTN_WIDTH = 2048
POOL_WIDTH = 2048
HEAD_DIM = 128
V_DIM = 2 * HEAD_DIM
N_HEADS = ATTN_WIDTH // V_DIM
QK_WIDTH = N_HEADS * 2 * HEAD_DIM
POOL_WINDOWS = (2, 4, 8, 16)
N_POOL_GROUPS = len(POOL_WINDOWS)
POOL_GROUP_WIDTH = POOL_WIDTH // N_POOL_GROUPS
IN_PROJ_WIDTH = 2 * QK_WIDTH + ATTN_WIDTH + POOL_WIDTH
N_EXPERT_GROUPS = 4
EXPERTS_PER_GROUP = 8
N_EXPERTS = N_EXPERT_GROUPS * EXPERTS_PER_GROUP
TOP_K = 2
EXPERT_FF = 1536
RMS_EPS = 1e-6
LAMBDA_INIT = 0.8 - 0.6 * math.exp(-0.3 * 0)

LANES = 128
VMEM_LIMIT = 56 * 1024 * 1024
NEG_BIG = -1e30
LOG2E = math.log2(math.e)

ADA_TN = 512
NORM_TM = 256
MM_TM = 1024
MM_TN = 1024
ATT_T = 512
POOL_TM = 512
POOL_HALO = 16
OUT_TM = 512
OUT_TN = 1024
ROUTE_LANES = LANES
EXPERT_LANE0 = N_EXPERT_GROUPS
MOE_RC = 1024
MOE_SB = 128
MOE_TF = 256
MOE_NF = EXPERT_FF // MOE_TF
MOE_TN = 512
MOE_NN = D_MODEL // MOE_TN
COMB_TM = 256


def _cparams(sem):
    return pltpu.CompilerParams(dimension_semantics=sem, vmem_limit_bytes=VMEM_LIMIT)


def _ada_kernel(c_ref, w_ref, b_ref, o_ref):
    d, tn = w_ref.shape
    ch = 256
    acc = jnp.zeros((8, tn), F32)
    for r in range(d // ch):
        cc = c_ref[r * ch:(r + 1) * ch, :]
        cc = cc * jax.nn.sigmoid(cc)
        w = w_ref[r * ch:(r + 1) * ch, :]
        acc = acc + (w * cc).reshape(ch // 8, 8, tn).sum(axis=0)
    o_ref[...] = acc.sum(axis=0, keepdims=True) + b_ref[...]


def _ada_mod(c, ada_w, ada_b):
    d, n = ada_w.shape
    return pl.pallas_call(
        _ada_kernel,
        grid=(n // ADA_TN,),
        in_specs=[
            pl.BlockSpec((d, 1), lambda j: (0, 0)),
            pl.BlockSpec((d, ADA_TN), lambda j: (0, j)),
            pl.BlockSpec((1, ADA_TN), lambda j: (0, j)),
        ],
        out_specs=pl.BlockSpec((1, ADA_TN), lambda j: (0, j)),
        out_shape=jax.ShapeDtypeStruct((1, n), F32),
        compiler_params=_cparams(("arbitrary",)),
        name="ada_mod",
    )(c.reshape(d, 1), ada_w, ada_b.reshape(1, n))


def _norm_mod(x, g, sc, sh):
    ms = jnp.mean(x * x, axis=-1, keepdims=True)
    return (x * lax.rsqrt(ms + RMS_EPS)) * g * (1.0 + sc) + sh


def _norm1_kernel(x_ref, g_ref, sc_ref, sh_ref, o_ref):
    o_ref[...] = _norm_mod(x_ref[...], g_ref[...], sc_ref[...], sh_ref[...]).astype(o_ref.dtype)


def _mod_spec(row):
    return pl.BlockSpec((None, 1, D_MODEL), lambda i, row=row: (row, 0, 0))


def _norm1(x2d, g, mod3):
    t, d = x2d.shape
    return pl.pallas_call(
        _norm1_kernel,
        grid=(t // NORM_TM,),
        in_specs=[
            pl.BlockSpec((NORM_TM, d), lambda i: (i, 0)),
            pl.BlockSpec((1, d), lambda i: (0, 0)),
            _mod_spec(1),
            _mod_spec(0),
        ],
        out_specs=pl.BlockSpec((NORM_TM, d), lambda i: (i, 0)),
        out_shape=jax.ShapeDtypeStruct((t, d), BF16),
        compiler_params=_cparams(("arbitrary",)),
        name="norm1_mod",
    )(x2d, g.reshape(1, d), mod3, mod3)


def _inproj_kernel(h_ref, w_ref, o_ref, *, n_q_tiles, scale):
    acc = jnp.dot(h_ref[...], w_ref[...], preferred_element_type=F32)
    s = jnp.where(pl.program_id(1) < n_q_tiles, scale, 1.0).astype(F32)
    o_ref[...] = (acc * s).astype(o_ref.dtype)


def _inproj(h, w_bf):
    t, d = h.shape
    n = w_bf.shape[1]
    kern = functools.partial(_inproj_kernel, n_q_tiles=QK_WIDTH // MM_TN, scale=LOG2E * HEAD_DIM ** -0.5)
    return pl.pallas_call(
        kern,
        grid=(t // MM_TM, n // MM_TN),
        in_specs=[
            pl.BlockSpec((MM_TM, d), lambda i, j: (i, 0)),
            pl.BlockSpec((d, MM_TN), lambda i, j: (0, j)),
        ],
        out_specs=pl.BlockSpec((MM_TM, MM_TN), lambda i, j: (i, j)),
        out_shape=jax.ShapeDtypeStruct((t, n), BF16),
        compiler_params=_cparams(("arbitrary", "arbitrary")),
        name="in_proj",
    )(h, w_bf)


def _attn_kernel(slopes_ref, q_ref, k_ref, vt_ref, lq1_ref, lk1_ref, lq2_ref, lk2_ref, sg_ref,
                 o_ref, m_scr, l_scr, acc_scr, s_scr):
    h = pl.program_id(0)
    i = pl.program_id(1)
    t = ATT_T
    slope2 = slopes_ref[h] * LOG2E
    q0 = i * t

    m_scr[...] = jnp.full(m_scr.shape, NEG_BIG, F32)
    l_scr[...] = jnp.zeros(l_scr.shape, F32)
    acc_scr[...] = jnp.zeros(acc_scr.shape, F32)

    def colreduce(x, op):
        part = op(x.reshape(t // 8, 8, t), axis=0)
        return jnp.broadcast_to(op(part, axis=0, keepdims=True), (8, t))

    def scores(j, buf):
        k0 = pl.multiple_of(j * t, t)
        kblk = k_ref[pl.ds(k0, t), :]
        for mp in range(2):
            qm = q_ref[:, mp * HEAD_DIM:(mp + 1) * HEAD_DIM]
            km = kblk[:, mp * HEAD_DIM:(mp + 1) * HEAD_DIM]
            s_scr[buf, mp] = lax.dot_general(km, qm, (((1,), (1,)), ((), ())),
                                             preferred_element_type=F32)

    def update(j, buf, masked):
        vtb = vt_ref[j]
        kidx = lax.broadcasted_iota(jnp.int32, (t, LANES), 0)
        bias = jnp.tile(slope2 * (kidx + (j * t - q0)).astype(F32), (1, t // LANES))
        if masked:
            keep = (lax.broadcasted_iota(jnp.int32, (t, t), 1)
                    >= lax.broadcasted_iota(jnp.int32, (t, t), 0))
        for mp in range(2):
            s = s_scr[buf, mp] + bias
            if masked:
                s = jnp.where(keep, s, NEG_BIG)
            m_prev = m_scr[mp]
            m_new = jnp.maximum(m_prev, colreduce(s, jnp.max))
            alpha = jnp.exp2(m_prev - m_new)
            p = jnp.exp2(s - jnp.tile(m_new, (t // 8, 1)))
            l_scr[mp] = alpha * l_scr[mp] + colreduce(p, jnp.sum)
            m_scr[mp] = m_new
            acc_scr[mp] = acc_scr[mp] * jnp.tile(alpha, (V_DIM // 8, 1)) + jnp.dot(
                vtb, p.astype(BF16), preferred_element_type=F32)

    scores(0, 0)

    def pair(p, c):
        j = 2 * p
        update(j, 0, False)
        scores(j + 1, 1)
        update(j + 1, 1, False)
        scores(j + 2, 0)
        return c

    lax.fori_loop(0, lax.shift_right_logical(i, 1), pair, 0)
    odd = (i & 1) == 1

    @pl.when(odd)
    def _():
        update(i - 1, 0, False)
        scores(i, 1)
        update(i, 1, True)

    @pl.when(jnp.logical_not(odd))
    def _():
        update(i, 0, True)

    lam = (jnp.exp(jnp.sum(lq1_ref[...] * lk1_ref[...], axis=1, keepdims=True))
           - jnp.exp(jnp.sum(lq2_ref[...] * lk2_ref[...], axis=1, keepdims=True))
           + LAMBDA_INIT)
    o1 = acc_scr[0] / jnp.tile(l_scr[0], (V_DIM // 8, 1))
    o2 = acc_scr[1] / jnp.tile(l_scr[1], (V_DIM // 8, 1))
    o = o1 - lam * o2
    ms = jnp.mean(o * o, axis=0, keepdims=True)
    gain = jnp.tile(sg_ref[...], (1, t // LANES)) * (1.0 - LAMBDA_INIT)
    o = (o * lax.rsqrt(ms + RMS_EPS)) * gain
    o_ref[...] = o.T.astype(o_ref.dtype)


def _attention(proj, slopes, lq1, lk1, lq2, lk2, subln_g):
    s_len = proj.shape[0]
    t = ATT_T
    nblk = s_len // t
    kcol0 = QK_WIDTH // V_DIM
    v = proj[:, 2 * QK_WIDTH:2 * QK_WIDTH + ATTN_WIDTH]
    vt = v.reshape(nblk, t, N_HEADS, V_DIM).transpose(2, 0, 3, 1)
    gain = jnp.broadcast_to(subln_g.reshape(V_DIM, 1), (V_DIM, LANES))
    vec = lambda: pl.BlockSpec((1, HEAD_DIM), lambda h, i, sl: (0, 0))
    grid_spec = pltpu.PrefetchScalarGridSpec(
        num_scalar_prefetch=1,
        grid=(N_HEADS, nblk),
        in_specs=[
            pl.BlockSpec((t, V_DIM), lambda h, i, sl: (i, h)),
            pl.BlockSpec((s_len, V_DIM), lambda h, i, sl: (0, kcol0 + h)),
            pl.BlockSpec((None, nblk, V_DIM, t), lambda h, i, sl: (h, 0, 0, 0)),
            vec(), vec(), vec(), vec(),
            pl.BlockSpec((V_DIM, LANES), lambda h, i, sl: (0, 0)),
        ],
        out_specs=pl.BlockSpec((t, V_DIM), lambda h, i, sl: (i, h)),
        scratch_shapes=[
            pltpu.VMEM((2, 8, t), F32),
            pltpu.VMEM((2, 8, t), F32),
            pltpu.VMEM((2, V_DIM, t), F32),
            pltpu.VMEM((2, 2, t, t), F32),
        ],
    )
    return pl.pallas_call(
        _attn_kernel,
        grid_spec=grid_spec,
        out_shape=jax.ShapeDtypeStruct((s_len, ATTN_WIDTH), BF16),
        compiler_params=_cparams(("arbitrary", "arbitrary")),
        name="diff_attn",
    )(slopes, proj, proj, vt, lq1, lk1, lq2, lk2, gain)


def _pool_kernel(u_ref, halo_ref, w_ref, sc_ref, o_ref, ext_scr):
    i = pl.program_id(0)
    tm = POOL_TM
    hl = POOL_HALO
    halo = halo_ref[...].astype(F32)
    ext_scr[0:hl, :] = jnp.where(i > 0, halo, 0.0)
    ext_scr[hl:hl + tm, :] = u_ref[...].astype(F32)
    pos = i * tm + lax.broadcasted_iota(jnp.int32, (tm, 1), 0)
    for g, win in enumerate(POOL_WINDOWS):
        c0, c1 = g * POOL_GROUP_WIDTH, (g + 1) * POOL_GROUP_WIDTH
        tok = ext_scr[hl:hl + tm, c0:c1]
        wsum = tok
        for dlt in range(1, win):
            wsum = wsum + ext_scr[hl - dlt:hl - dlt + tm, c0:c1]
        count = jnp.minimum(pos + 1, win).astype(F32)
        pooled = wsum / count - tok
        y = jnp.dot(pooled.astype(BF16), w_ref[g], preferred_element_type=F32)
        o_ref[:, c0:c1] = (y * sc_ref[:, c0:c1]).astype(o_ref.dtype)


def _pool(proj, pool_w_bf, pool_scale):
    s_len = proj.shape[0]
    ucol = (2 * QK_WIDTH + ATTN_WIDTH) // POOL_WIDTH
    rb = POOL_TM // POOL_HALO
    return pl.pallas_call(
        _pool_kernel,
        grid=(s_len // POOL_TM,),
        in_specs=[
            pl.BlockSpec((POOL_TM, POOL_WIDTH), lambda i: (i, ucol)),
            pl.BlockSpec((POOL_HALO, POOL_WIDTH), lambda i: (jnp.maximum(i * rb - 1, 0), ucol)),
            pl.BlockSpec((N_POOL_GROUPS, POOL_GROUP_WIDTH, POOL_GROUP_WIDTH), lambda i: (0, 0, 0)),
            pl.BlockSpec((1, POOL_WIDTH), lambda i: (0, 0)),
        ],
        out_specs=pl.BlockSpec((POOL_TM, POOL_WIDTH), lambda i: (i, 0)),
        out_shape=jax.ShapeDtypeStruct((s_len, POOL_WIDTH), BF16),
        scratch_shapes=[pltpu.VMEM((POOL_HALO + POOL_TM, POOL_WIDTH), F32)],
        compiler_params=_cparams(("arbitrary",)),
        name="pool_mixer",
    )(proj, proj, pool_w_bf, pool_scale.reshape(1, POOL_WIDTH))


def _outproj_kernel(a_ref, p_ref, wa_ref, wp_ref, x_ref, g_ref, o_ref):
    acc = jnp.dot(a_ref[...], wa_ref[...], preferred_element_type=F32)
    acc = acc + jnp.dot(p_ref[...], wp_ref[...], preferred_element_type=F32)
    o_ref[...] = x_ref[...] + g_ref[...] * acc


def _outproj(o_attn, o_pool, w_out_bf, x2d, mod3):
    t, d = x2d.shape
    return pl.pallas_call(
        _outproj_kernel,
        grid=(t // OUT_TM, d // OUT_TN),
        in_specs=[
            pl.BlockSpec((OUT_TM, ATTN_WIDTH), lambda i, j: (i, 0)),
            pl.BlockSpec((OUT_TM, POOL_WIDTH), lambda i, j: (i, 0)),
            pl.BlockSpec((ATTN_WIDTH, OUT_TN), lambda i, j: (0, j)),
            pl.BlockSpec((POOL_WIDTH, OUT_TN), lambda i, j: (1, j)),
            pl.BlockSpec((OUT_TM, OUT_TN), lambda i, j: (i, j)),
            pl.BlockSpec((None, 1, OUT_TN), lambda i, j: (2, 0, j)),
        ],
        out_specs=pl.BlockSpec((OUT_TM, OUT_TN), lambda i, j: (i, j)),
        out_shape=jax.ShapeDtypeStruct((t, d), F32),
        compiler_params=_cparams(("arbitrary", "arbitrary")),
        name="out_proj",
    )(o_attn, o_pool, w_out_bf, w_out_bf, x2d, mod3)


def _route_kernel(x_ref, g_ref, sc_ref, sh_ref, whi_ref, wlo_ref, rb_ref,
                  h_ref, route_ref, cnt_ref, base_scr):
    step = pl.program_id(0)
    tm = NORM_TM

    @pl.when(step == 0)
    def _():
        base_scr[...] = jnp.zeros(base_scr.shape, F32)

    h2 = _norm_mod(x_ref[...], g_ref[...], sc_ref[...], sh_ref[...])
    half = D_MODEL // 2
    lo = pltpu.bitcast(h2[:, :half].astype(BF16).astype(F32), jnp.uint32)
    hi = pltpu.bitcast(h2[:, half:].astype(BF16).astype(F32), jnp.uint32)
    word = hi | (lo >> 16)
    for g in range(half // LANES):
        h_ref[:, g, :] = word[:, g * LANES:(g + 1) * LANES]

    h_hi = h2.astype(BF16)
    h_lo = (h2 - h_hi.astype(F32)).astype(BF16)
    w_hi = whi_ref[...]
    lg = (jnp.dot(h_hi, w_hi, preferred_element_type=F32)
          + jnp.dot(h_lo, w_hi, preferred_element_type=F32)
          + jnp.dot(h_hi, wlo_ref[...], preferred_element_type=F32)) + rb_ref[...]

    lane = lax.broadcasted_iota(jnp.int32, (tm, ROUTE_LANES), 1)

    def first_max(vals):
        v = jnp.max(vals, axis=1, keepdims=True)
        idx = jnp.min(jnp.where(vals == v, lane, ROUTE_LANES), axis=1, keepdims=True)
        return v, idx

    gl = jnp.where(lane < N_EXPERT_GROUPS, lg, NEG_BIG)
    gmax, g_sel = first_max(gl)
    p_g = 1.0 / jnp.sum(jnp.exp(gl - gmax), axis=1, keepdims=True)

    e_lo = EXPERT_LANE0 + g_sel * EXPERTS_PER_GROUP
    el = jnp.where(jnp.logical_and(lane >= e_lo, lane < e_lo + EXPERTS_PER_GROUP), lg, NEG_BIG)
    v1, j1 = first_max(el)
    el2 = jnp.where(lane == j1, NEG_BIG, el)
    v2, j2 = first_max(el2)
    e2 = jnp.exp(v2 - v1)
    gate1 = p_g / (1.0 + e2)
    gate2 = p_g * e2 / (1.0 + e2)

    oh1 = (lane == j1).astype(BF16)
    oh2 = (lane == j2).astype(BF16)
    r_i = lax.broadcasted_iota(jnp.int32, (tm, tm), 0)
    c_i = lax.broadcasted_iota(jnp.int32, (tm, tm), 1)
    lower = (c_i < r_i).astype(BF16)
    before1 = jnp.dot(lower, oh1, preferred_element_type=F32)
    before2 = jnp.dot(lower, oh2, preferred_element_type=F32)
    oh1f = oh1.astype(F32)
    oh2f = oh2.astype(F32)
    tot1 = jnp.sum(oh1f, axis=0, keepdims=True)
    tot2 = jnp.sum(oh2f, axis=0, keepdims=True)
    base = base_scr[0:1, :]
    rank1 = jnp.sum((base + before1) * oh1f, axis=1, keepdims=True)
    rank2 = jnp.sum((base + tot1 + before2) * oh2f, axis=1, keepdims=True)
    new_base = base + tot1 + tot2
    base_scr[0:1, :] = new_base
    cnt_ref[...] = new_base

    eid1 = (j1 - EXPERT_LANE0).astype(F32)
    eid2 = (j2 - EXPERT_LANE0).astype(F32)
    packed = jnp.zeros((tm, ROUTE_LANES), F32)
    for k, val in enumerate((eid1, eid2, rank1, rank2, gate1, gate2)):
        packed = jnp.where(lane == k, val, packed)
    route_ref[...] = packed


def _norm2_route(x1, g, mod3, w_hi, w_lo, rbias):
    t, d = x1.shape
    const = lambda shape: pl.BlockSpec(shape, lambda i: (0,) * len(shape))
    return pl.pallas_call(
        _route_kernel,
        grid=(t // NORM_TM,),
        in_specs=[
            pl.BlockSpec((NORM_TM, d), lambda i: (i, 0)),
            const((1, d)),
            _mod_spec(4),
            _mod_spec(3),
            const((d, ROUTE_LANES)),
            const((d, ROUTE_LANES)),
            const((1, ROUTE_LANES)),
        ],
        out_specs=[
            pl.BlockSpec((NORM_TM, d // 2 // LANES, LANES), lambda i: (i, 0, 0)),
            pl.BlockSpec((NORM_TM, ROUTE_LANES), lambda i: (i, 0)),
            const((1, ROUTE_LANES)),
        ],
        out_shape=[
            jax.ShapeDtypeStruct((t, d // 2 // LANES, LANES), jnp.uint32),
            jax.ShapeDtypeStruct((t, ROUTE_LANES), F32),
            jax.ShapeDtypeStruct((1, ROUTE_LANES), F32),
        ],
        scratch_shapes=[pltpu.VMEM((8, ROUTE_LANES), F32)],
        compiler_params=_cparams(("arbitrary",)),
        name="norm2_route",
    )(x1, g.reshape(1, d), mod3, mod3, w_hi, w_lo, rbias)


META_E, META_ROW0, META_NSUB, META_MISC = 0, 1, 2, 3
MISC_NACT, MISC_TAIL_ROW0, MISC_TAIL_PIECES = 0, 1, 2


def _rest_blocks(nsub, fn):
    rest = nsub - 1
    pairs = lax.shift_right_logical(rest, 1)

    def body(p, carry):
        fn(pl.multiple_of(MOE_SB + p * (2 * MOE_SB), MOE_SB), 2 * MOE_SB)
        return carry

    lax.fori_loop(0, pairs, body, 0)

    @pl.when((rest & 1) == 1)
    def _():
        fn(pl.multiple_of(MOE_SB + pairs * (2 * MOE_SB), MOE_SB), MOE_SB)


def _ffn_kernel(meta_ref, idx_ref, idxn_ref, hp_hbm, wg_ref, wu_ref, wd_ref, y_hbm,
                xw, x_bf, h_scr, wg_bf, wu_bf, wd_bf, obuf, gsem, osem):
    c = pl.program_id(0)
    s = pl.program_id(1)
    nact = meta_ref[META_MISC, MISC_NACT]
    active = c < nact
    nsub = meta_ref[META_NSUB, c]
    row0 = meta_ref[META_ROW0, c]
    half = D_MODEL // 2
    groups = half // LANES
    sb_shift = int(math.log2(MOE_SB))

    def start_gather(ids_ref, n_sub):
        def issue(r, carry):
            dst = xw.at[pl.ds(pl.multiple_of(r * groups, groups), groups), :]
            pltpu.make_async_copy(hp_hbm.at[ids_ref[0, r]], dst, gsem).start()
            return carry

        lax.fori_loop(0, lax.shift_left(n_sub, sb_shift), issue, 0)

    def wait_gather(n_sub):
        def wait(i, carry):
            pltpu.make_async_copy(xw.at[pl.ds(0, MOE_SB * groups), :], xw.at[pl.ds(0, MOE_SB * groups), :],
                                  gsem).wait()
            return carry

        lax.fori_loop(0, n_sub, wait, 0)

    def out_copy(slot, i, col):
        r = pl.multiple_of(i * MOE_SB, MOE_SB)
        return pltpu.make_async_copy(
            obuf.at[slot, pl.ds(r, MOE_SB), :],
            y_hbm.at[pl.ds(pl.multiple_of(row0 + r, MOE_SB), MOE_SB), pl.ds(col * MOE_TN, MOE_TN)],
            osem.at[slot])

    def wait_out(slot, n_sub):
        def wait(i, carry):
            out_copy(slot, 0, 0).wait()
            return carry

        lax.fori_loop(0, n_sub, wait, 0)

    @pl.when(jnp.logical_and(active, s == 0))
    def _():
        @pl.when(c == 0)
        def _():
            start_gather(idx_ref, nsub)

        wait_gather(nsub)

        def unpack(i, carry):
            r0 = pl.multiple_of(i * MOE_SB, MOE_SB)
            for g in range(groups):
                w = xw[pl.ds(r0 * groups + g, MOE_SB, stride=groups), :]
                c0 = g * LANES
                x_bf[pl.ds(r0, MOE_SB), c0:c0 + LANES] = pltpu.bitcast(w << 16, F32).astype(BF16)
                x_bf[pl.ds(r0, MOE_SB), half + c0:half + c0 + LANES] = pltpu.bitcast(
                    w & jnp.uint32(0xFFFF0000), F32).astype(BF16)
            return carry

        lax.fori_loop(0, nsub, unpack, 0)

    @pl.when(jnp.logical_and(c + 1 < nact, s == 1))
    def _():
        start_gather(idxn_ref, meta_ref[META_NSUB, c + 1])

    @pl.when(jnp.logical_and(active, s < MOE_NF))
    def _():
        def gate_up(r0, rows, wg, wu):
            xs = x_bf[pl.ds(r0, rows), :]
            gt = jnp.dot(xs, wg, preferred_element_type=F32)
            up = jnp.dot(xs, wu, preferred_element_type=F32)
            h_scr[s, pl.ds(r0, rows), :] = (jax.nn.silu(gt) * up).astype(BF16)

        wg = wg_ref[...].astype(BF16)
        wu = wu_ref[...].astype(BF16)
        wg_bf[...] = wg
        wu_bf[...] = wu
        gate_up(0, MOE_SB, wg, wu)
        _rest_blocks(nsub, lambda r0, rows: gate_up(r0, rows, wg_bf[...], wu_bf[...]))

    @pl.when(jnp.logical_and(active, s >= MOE_NF))
    def _():
        n = s - MOE_NF
        slot = n & 1

        @pl.when(n >= 2)
        def _():
            wait_out(slot, nsub)

        def down(r0, rows, wd):
            hs = jnp.concatenate([h_scr[f, pl.ds(r0, rows), :] for f in range(MOE_NF)], axis=1)
            obuf[slot, pl.ds(r0, rows), :] = jnp.dot(hs, wd, preferred_element_type=F32)

        wd = wd_ref[...].astype(BF16)
        wd_bf[...] = wd
        down(0, MOE_SB, wd)
        _rest_blocks(nsub, lambda r0, rows: down(r0, rows, wd_bf[...]))

        for col in range(MOE_NN):
            @pl.when(n == col)
            def _(col=col):
                def send(i, carry):
                    out_copy(slot, i, col).start()
                    return carry

                lax.fori_loop(0, nsub, send, 0)

        @pl.when(n == MOE_NN - 1)
        def _():
            wait_out(1 - slot, nsub)
            wait_out(slot, nsub)

    @pl.when(jnp.logical_and(jnp.logical_not(active), s == 0))
    def _():
        per_chunk = MOE_RC // MOE_SB
        first = (c - nact) * per_chunk
        n_here = jnp.clip(meta_ref[META_MISC, MISC_TAIL_PIECES] - first, 0, per_chunk)
        tail0 = meta_ref[META_MISC, MISC_TAIL_ROW0]
        obuf[0, 0:MOE_SB, :] = jnp.zeros((MOE_SB, MOE_TN), F32)

        def zcopy(k, col):
            r = pl.multiple_of(tail0 + (first + k) * MOE_SB, MOE_SB)
            return pltpu.make_async_copy(
                obuf.at[0, pl.ds(0, MOE_SB), :],
                y_hbm.at[pl.ds(r, MOE_SB), pl.ds(col * MOE_TN, MOE_TN)],
                osem.at[0])

        def send(k, carry):
            for col in range(MOE_NN):
                zcopy(k, col).start()
            return carry

        def wait(k, carry):
            for col in range(MOE_NN):
                zcopy(k, col).wait()
            return carry

        lax.fori_loop(0, n_here, send, 0)
        lax.fori_loop(0, n_here, wait, 0)


def _ffn(meta, idx_tab, hp, w_gate, w_up, w_down, n_rows):
    d = D_MODEL
    nc = idx_tab.shape[0]
    last = MOE_NF + MOE_NN - 1

    def eff(c, s, m):
        nact = m[META_MISC, MISC_NACT]
        act = c < nact
        return jnp.where(act, c, nact - 1), jnp.where(act, s, last)

    def idx_map(c, s, m):
        return (c, 0, 0)

    def idxn_map(c, s, m):
        return (jnp.minimum(c + 1, nc - 1), 0, 0)

    def w1_map(c, s, m):
        cc, ss = eff(c, s, m)
        return (m[META_E, cc], 0, jnp.minimum(ss, MOE_NF - 1))

    def wd_map(c, s, m):
        cc, ss = eff(c, s, m)
        return (m[META_E, cc], 0, jnp.maximum(ss - MOE_NF, 0))

    grid_spec = pltpu.PrefetchScalarGridSpec(
        num_scalar_prefetch=1,
        grid=(nc, MOE_NF + MOE_NN),
        in_specs=[
            pl.BlockSpec((None, 1, MOE_RC), idx_map, memory_space=pltpu.SMEM),
            pl.BlockSpec((None, 1, MOE_RC), idxn_map, memory_space=pltpu.SMEM),
            pl.BlockSpec(memory_space=pl.ANY),
            pl.BlockSpec((None, d, MOE_TF), w1_map),
            pl.BlockSpec((None, d, MOE_TF), w1_map),
            pl.BlockSpec((None, EXPERT_FF, MOE_TN), wd_map),
        ],
        out_specs=pl.BlockSpec(memory_space=pl.ANY),
        scratch_shapes=[
            pltpu.VMEM((MOE_RC * (d // 2 // LANES), LANES), jnp.uint32),
            pltpu.VMEM((MOE_RC, d), BF16),
            pltpu.VMEM((MOE_NF, MOE_RC, MOE_TF), BF16),
            pltpu.VMEM((d, MOE_TF), BF16),
            pltpu.VMEM((d, MOE_TF), BF16),
            pltpu.VMEM((EXPERT_FF, MOE_TN), BF16),
            pltpu.VMEM((2, MOE_RC, MOE_TN), F32),
            pltpu.SemaphoreType.DMA(()),
            pltpu.SemaphoreType.DMA((2,)),
        ],
    )
    return pl.pallas_call(
        _ffn_kernel,
        grid_spec=grid_spec,
        out_shape=jax.ShapeDtypeStruct((n_rows, d), F32),
        compiler_params=_cparams(("arbitrary", "arbitrary")),
        name="moe_ffn",
    )(meta, idx_tab, idx_tab, hp, w_gate, w_up, w_down)


def _combine_kernel(d0_ref, d1_ref, x_ref, route_ref, g2_ref, fg_ref, y_hbm, o_ref, buf0, buf1, sem0, sem1):
    tm = COMB_TM

    def issue(r, c):
        pltpu.make_async_copy(y_hbm.at[pl.ds(d0_ref[0, r], 1), :], buf0.at[pl.ds(r, 1), :], sem0).start()
        pltpu.make_async_copy(y_hbm.at[pl.ds(d1_ref[0, r], 1), :], buf1.at[pl.ds(r, 1), :], sem1).start()
        return c

    lax.fori_loop(0, tm, issue, 0)
    pltpu.make_async_copy(y_hbm.at[pl.ds(0, tm), :], buf0, sem0).wait()
    pltpu.make_async_copy(y_hbm.at[pl.ds(0, tm), :], buf1, sem1).wait()
    route = route_ref[...]
    y = buf0[...] * route[:, 4:5] + buf1[...] * route[:, 5:6]
    x2 = x_ref[...] + g2_ref[...] * y
    ms = jnp.mean(x2 * x2, axis=-1, keepdims=True)
    o_ref[...] = (x2 * lax.rsqrt(ms + RMS_EPS)) * fg_ref[...]


def _combine(dest0, dest1, x1, route, mod3, final_g, y_sorted):
    t, d = x1.shape
    nb = t // COMB_TM
    smem_idx = lambda: pl.BlockSpec((None, 1, COMB_TM), lambda i: (i, 0, 0), memory_space=pltpu.SMEM)
    return pl.pallas_call(
        _combine_kernel,
        grid=(nb,),
        in_specs=[
            smem_idx(),
            smem_idx(),
            pl.BlockSpec((COMB_TM, d), lambda i: (i, 0)),
            pl.BlockSpec((COMB_TM, ROUTE_LANES), lambda i: (i, 0)),
            _mod_spec(5),
            pl.BlockSpec((1, d), lambda i: (0, 0)),
            pl.BlockSpec(memory_space=pl.ANY),
        ],
        out_specs=pl.BlockSpec((COMB_TM, d), lambda i: (i, 0)),
        out_shape=jax.ShapeDtypeStruct((t, d), F32),
        scratch_shapes=[
            pltpu.VMEM((COMB_TM, d), F32),
            pltpu.VMEM((COMB_TM, d), F32),
            pltpu.SemaphoreType.DMA(()),
            pltpu.SemaphoreType.DMA(()),
        ],
        compiler_params=_cparams(("arbitrary",)),
        name="moe_combine",
    )(dest0.reshape(nb, 1, COMB_TM), dest1.reshape(nb, 1, COMB_TM), x1, route, mod3,
      final_g.reshape(1, d), y_sorted)


def _dispatch_plan(route, cnt, n_tokens):
    n_assign = n_tokens * TOP_K
    n_rows = n_assign + N_EXPERTS * MOE_SB
    nc = n_assign // MOE_RC + N_EXPERTS
    i32 = jnp.int32
    eid = route[:, 0:2].astype(i32)
    rank = route[:, 2:4].astype(i32)
    counts = cnt[0, EXPERT_LANE0:EXPERT_LANE0 + N_EXPERTS].astype(i32)
    seg_rows = (counts + MOE_SB - 1) // MOE_SB * MOE_SB
    seg_end = jnp.cumsum(seg_rows)
    seg_start = seg_end - seg_rows
    dest = seg_start[eid] + rank
    nchunk = (seg_rows + MOE_RC - 1) // MOE_RC
    ch_end = jnp.cumsum(nchunk)
    ch_start = ch_end - nchunk
    nact = ch_end[-1]
    cidx = jnp.arange(nc, dtype=i32)
    ch_e = jnp.minimum(jnp.searchsorted(ch_end, cidx, side="right"), N_EXPERTS - 1).astype(i32)
    k = cidx - ch_start[ch_e]
    ch_row0 = jnp.where(cidx < nact, seg_start[ch_e] + k * MOE_RC, 0)
    ch_nsub = jnp.where(cidx < nact, jnp.clip(seg_rows[ch_e] - k * MOE_RC, 0, MOE_RC) // MOE_SB, 0)
    misc = jnp.zeros((nc,), i32).at[MISC_NACT].set(nact)
    misc = misc.at[MISC_TAIL_ROW0].set(seg_end[-1]).at[MISC_TAIL_PIECES].set((n_rows - seg_end[-1]) // MOE_SB)
    meta = jnp.stack([ch_e, ch_row0, ch_nsub, misc]).astype(i32)
    slot = (ch_start[eid] + rank // MOE_RC) * MOE_RC + rank % MOE_RC
    tok = jnp.broadcast_to(jnp.arange(n_tokens, dtype=i32)[:, None], (n_tokens, TOP_K))
    idx_tab = (jnp.arange(nc * MOE_RC, dtype=i32) % n_tokens).at[slot.reshape(-1)].set(tok.reshape(-1))
    return dest, meta, idx_tab.reshape(nc, 1, MOE_RC), n_rows


def kernel(x, c, ada_w, ada_b, norm1_g, w_in, lambda_q1, lambda_k1, lambda_q2, lambda_k2, subln_g,
           pool_w, pool_scale, w_out, norm2_g, router_group_w, router_group_b, router_expert_w,
           router_expert_b, expert_w_gate, expert_w_up, expert_w_down, final_norm_g):
    b_, s_, d = x.shape
    assert b_ == 1 and d == D_MODEL and ada_w.shape[0] == 1
    t = b_ * s_
    x2d = x.reshape(t, d)

    mod3 = _ada_mod(c, ada_w[0], ada_b[0]).reshape(6, 1, d)

    h = _norm1(x2d, norm1_g[0], mod3)
    proj = _inproj(h, w_in[0].astype(BF16))
    slopes = 2.0 ** (-8.0 * jnp.arange(1, N_HEADS + 1, dtype=F32) / N_HEADS)
    o_attn = _attention(proj, slopes, lambda_q1, lambda_k1, lambda_q2, lambda_k2, subln_g)
    o_pool = _pool(proj, pool_w[0].astype(BF16), pool_scale[0])
    x1 = _outproj(o_attn, o_pool, w_out[0].astype(BF16), x2d, mod3)

    rw = jnp.zeros((d, ROUTE_LANES), F32)
    rw = rw.at[:, :N_EXPERT_GROUPS].set(router_group_w[0])
    rw = rw.at[:, EXPERT_LANE0:EXPERT_LANE0 + N_EXPERTS].set(router_expert_w[0])
    rw_hi = rw.astype(BF16)
    rw_lo = (rw - rw_hi.astype(F32)).astype(BF16)
    rbias = jnp.full((1, ROUTE_LANES), NEG_BIG, F32)
    rbias = rbias.at[0, :N_EXPERT_GROUPS].set(router_group_b[0])
    rbias = rbias.at[0, EXPERT_LANE0:EXPERT_LANE0 + N_EXPERTS].set(router_expert_b[0].reshape(-1))
    h2, route, cnt = _norm2_route(x1, norm2_g[0], mod3, rw_hi, rw_lo, rbias)

    dest, meta, idx_tab, n_rows = _dispatch_plan(route, cnt, t)
    y_sorted = _ffn(meta, idx_tab, h2, expert_w_gate[0], expert_w_up[0], expert_w_down[0], n_rows)
    out = _combine(dest[:, 0], dest[:, 1], x1, route, mod3, final_norm_g, y_sorted)
    return out.reshape(b_, s_, d)
```

```python
import functools
import math

import jax
import jax.numpy as jnp
from jax import lax
from jax.experimental import pallas as pl
from jax.experimental.pallas import tpu as pltpu

F32 = jnp.float32
BF16 = jnp.bfloat16

D_MODEL = 4096
ATTN_WIDTH = 2048
POOL_WIDTH = 2048
HEAD_DIM = 128
V_DIM = 2 * HEAD_DIM
N_HEADS = ATTN_WIDTH // V_DIM
QK_WIDTH = N_HEADS * 2 * HEAD_DIM
POOL_WINDOWS = (2, 4, 8, 16)
N_POOL_GROUPS = len(POOL_WINDOWS)
POOL_GROUP_WIDTH = POOL_WIDTH // N_POOL_GROUPS
IN_PROJ_WIDTH = 2 * QK_WIDTH + ATTN_WIDTH + POOL_WIDTH
N_EXPERT_GROUPS = 4
EXPERTS_PER_GROUP = 8
N_EXPERTS = N_EXPERT_GROUPS * EXPERTS_PER_GROUP
TOP_K = 2
EXPERT_FF = 1536
RMS_EPS = 1e-6
LAMBDA_INIT = 0.8 - 0.6 * math.exp(-0.3 * 0)

LANES = 128
VMEM_LIMIT = 56 * 1024 * 1024
NEG_BIG = -1e30
LOG2E = math.log2(math.e)

ADA_TN = 512
NORM_TM = 256
MM_TM = 1024
MM_TN = 512
ATT_T = 512
POOL_TM = 512
POOL_HALO = 16
OUT_TM = 1024
OUT_TN = 512
ROUTE_LANES = LANES
EXPERT_LANE0 = N_EXPERT_GROUPS
MOE_RC = 1024
MOE_SB = 128
MOE_TF = 256
MOE_NF = EXPERT_FF // MOE_TF
MOE_TN = 512
MOE_NN = D_MODEL // MOE_TN
COMB_TM = 256


def _cparams(sem):
    return pltpu.CompilerParams(dimension_semantics=sem, vmem_limit_bytes=VMEM_LIMIT)


def _ada_kernel(c_ref, w_ref, b_ref, o_ref):
    d, tn = w_ref.shape
    ch = 256
    acc = jnp.zeros((8, tn), F32)
    for r in range(d // ch):
        cc = c_ref[r * ch:(r + 1) * ch, :]
        cc = cc * jax.nn.sigmoid(cc)
        w = w_ref[r * ch:(r + 1) * ch, :]
        acc = acc + (w * cc).reshape(ch // 8, 8, tn).sum(axis=0)
    o_ref[...] = acc.sum(axis=0, keepdims=True) + b_ref[...]


def _ada_mod(c, ada_w, ada_b):
    d, n = ada_w.shape
    return pl.pallas_call(
        _ada_kernel,
        grid=(n // ADA_TN,),
        in_specs=[
            pl.BlockSpec((d, 1), lambda j: (0, 0)),
            pl.BlockSpec((d, ADA_TN), lambda j: (0, j)),
            pl.BlockSpec((1, ADA_TN), lambda j: (0, j)),
        ],
        out_specs=pl.BlockSpec((1, ADA_TN), lambda j: (0, j)),
        out_shape=jax.ShapeDtypeStruct((1, n), F32),
        compiler_params=_cparams(("arbitrary",)),
        name="ada_mod",
    )(c.reshape(d, 1), ada_w, ada_b.reshape(1, n))


def _norm_mod(x, g, sc, sh):
    ms = jnp.mean(x * x, axis=-1, keepdims=True)
    return (x * lax.rsqrt(ms + RMS_EPS)) * g * (1.0 + sc) + sh


def _norm1_kernel(x_ref, g_ref, sc_ref, sh_ref, o_ref):
    o_ref[...] = _norm_mod(x_ref[...], g_ref[...], sc_ref[...], sh_ref[...]).astype(o_ref.dtype)


def _mod_spec(row):
    return pl.BlockSpec((None, 1, D_MODEL), lambda i, row=row: (row, 0, 0))


def _norm1(x2d, g, mod3):
    t, d = x2d.shape
    return pl.pallas_call(
        _norm1_kernel,
        grid=(t // NORM_TM,),
        in_specs=[
            pl.BlockSpec((NORM_TM, d), lambda i: (i, 0)),
            pl.BlockSpec((1, d), lambda i: (0, 0)),
            _mod_spec(1),
            _mod_spec(0),
        ],
        out_specs=pl.BlockSpec((NORM_TM, d), lambda i: (i, 0)),
        out_shape=jax.ShapeDtypeStruct((t, d), BF16),
        compiler_params=_cparams(("arbitrary",)),
        name="norm1_mod",
    )(x2d, g.reshape(1, d), mod3, mod3)


def _inproj_kernel(h_ref, w_ref, o_ref, w_bf, *, n_q_tiles, scale):
    j = pl.program_id(0)

    @pl.when(pl.program_id(1) == 0)
    def _():
        w_bf[...] = w_ref[...].astype(BF16)

    acc = jnp.dot(h_ref[...], w_bf[...], preferred_element_type=F32)
    s = jnp.where(j < n_q_tiles, scale, 1.0).astype(F32)
    o_ref[...] = (acc * s).astype(o_ref.dtype)


def _inproj(h, w):
    t, d = h.shape
    n = w.shape[1]
    kern = functools.partial(_inproj_kernel, n_q_tiles=QK_WIDTH // MM_TN, scale=LOG2E * HEAD_DIM ** -0.5)
    return pl.pallas_call(
        kern,
        grid=(n // MM_TN, t // MM_TM),
        in_specs=[
            pl.BlockSpec((MM_TM, d), lambda j, i: (i, 0)),
            pl.BlockSpec((d, MM_TN), lambda j, i: (0, j)),
        ],
        out_specs=pl.BlockSpec((MM_TM, MM_TN), lambda j, i: (i, j)),
        out_shape=jax.ShapeDtypeStruct((t, n), BF16),
        scratch_shapes=[pltpu.VMEM((d, MM_TN), BF16)],
        compiler_params=_cparams(("arbitrary", "arbitrary")),
        name="in_proj",
    )(h, w)


def _attn_kernel(slopes_ref, q_ref, k_ref, vt_ref, lq1_ref, lk1_ref, lq2_ref, lk2_ref, sg_ref,
                 o_ref, m_scr, l_scr, acc_scr, s_scr):
    h = pl.program_id(0)
    i = pl.program_id(1)
    t = ATT_T
    slope2 = slopes_ref[h] * LOG2E
    q0 = i * t

    m_scr[...] = jnp.full(m_scr.shape, NEG_BIG, F32)
    l_scr[...] = jnp.zeros(l_scr.shape, F32)
    acc_scr[...] = jnp.zeros(acc_scr.shape, F32)

    def colreduce(x, op):
        part = op(x.reshape(t // 8, 8, t), axis=0)
        return jnp.broadcast_to(op(part, axis=0, keepdims=True), (8, t))

    def scores(j, buf):
        k0 = pl.multiple_of(j * t, t)
        kblk = k_ref[pl.ds(k0, t), :]
        for mp in range(2):
            qm = q_ref[:, mp * HEAD_DIM:(mp + 1) * HEAD_DIM]
            km = kblk[:, mp * HEAD_DIM:(mp + 1) * HEAD_DIM]
            s_scr[buf, mp] = lax.dot_general(km, qm, (((1,), (1,)), ((), ())),
                                             preferred_element_type=F32)

    def update(j, buf, masked):
        vtb = vt_ref[j]
        kidx = lax.broadcasted_iota(jnp.int32, (t, LANES), 0)
        bias = jnp.tile(slope2 * (kidx + (j * t - q0)).astype(F32), (1, t // LANES))
        if masked:
            keep = (lax.broadcasted_iota(jnp.int32, (t, t), 1)
                    >= lax.broadcasted_iota(jnp.int32, (t, t), 0))
        for mp in range(2):
            s = s_scr[buf, mp] + bias
            if masked:
                s = jnp.where(keep, s, NEG_BIG)
            m_prev = m_scr[mp]
            m_new = jnp.maximum(m_prev, colreduce(s, jnp.max))
            alpha = jnp.exp2(m_prev - m_new)
            p = jnp.exp2(s - jnp.tile(m_new, (t // 8, 1)))
            l_scr[mp] = alpha * l_scr[mp] + colreduce(p, jnp.sum)
            m_scr[mp] = m_new
            acc_scr[mp] = acc_scr[mp] * jnp.tile(alpha, (V_DIM // 8, 1)) + jnp.dot(
                vtb, p.astype(BF16), preferred_element_type=F32)

    scores(0, 0)

    def pairs(j, n_pairs):
        for q in range(n_pairs):
            update(j + 2 * q, 0, False)
            scores(j + 2 * q + 1, 1)
            update(j + 2 * q + 1, 1, False)
            scores(j + 2 * q + 2, 0)

    def quad(p, c):
        pairs(4 * p, 2)
        return c

    n_quads = lax.shift_right_logical(i, 2)
    lax.fori_loop(0, n_quads, quad, 0)

    @pl.when((i & 2) != 0)
    def _():
        pairs(4 * n_quads, 1)

    odd = (i & 1) == 1

    @pl.when(odd)
    def _():
        update(i - 1, 0, False)
        scores(i, 1)
        update(i, 1, True)

    @pl.when(jnp.logical_not(odd))
    def _():
        update(i, 0, True)

    lam = (jnp.exp(jnp.sum(lq1_ref[...] * lk1_ref[...], axis=1, keepdims=True))
           - jnp.exp(jnp.sum(lq2_ref[...] * lk2_ref[...], axis=1, keepdims=True))
           + LAMBDA_INIT)
    o1 = acc_scr[0] / jnp.tile(l_scr[0], (V_DIM // 8, 1))
    o2 = acc_scr[1] / jnp.tile(l_scr[1], (V_DIM // 8, 1))
    o = o1 - lam * o2
    ms = jnp.mean(o * o, axis=0, keepdims=True)
    gain = jnp.tile(sg_ref[...], (1, t // LANES)) * (1.0 - LAMBDA_INIT)
    o = (o * lax.rsqrt(ms + RMS_EPS)) * gain
    o_ref[...] = o.T.astype(o_ref.dtype)


def _attention(proj, slopes, lq1, lk1, lq2, lk2, subln_g):
    s_len = proj.shape[0]
    t = ATT_T
    nblk = s_len // t
    kcol0 = QK_WIDTH // V_DIM
    v = proj[:, 2 * QK_WIDTH:2 * QK_WIDTH + ATTN_WIDTH]
    vt = v.reshape(nblk, t, N_HEADS, V_DIM).transpose(2, 0, 3, 1)
    gain = jnp.broadcast_to(subln_g.reshape(V_DIM, 1), (V_DIM, LANES))
    vec = lambda: pl.BlockSpec((1, HEAD_DIM), lambda h, i, sl: (0, 0))
    grid_spec = pltpu.PrefetchScalarGridSpec(
        num_scalar_prefetch=1,
        grid=(N_HEADS, nblk),
        in_specs=[
            pl.BlockSpec((t, V_DIM), lambda h, i, sl: (i, h)),
            pl.BlockSpec((s_len, V_DIM), lambda h, i, sl: (0, kcol0 + h)),
            pl.BlockSpec((None, nblk, V_DIM, t), lambda h, i, sl: (h, 0, 0, 0)),
            vec(), vec(), vec(), vec(),
            pl.BlockSpec((V_DIM, LANES), lambda h, i, sl: (0, 0)),
        ],
        out_specs=pl.BlockSpec((t, V_DIM), lambda h, i, sl: (i, h)),
        scratch_shapes=[
            pltpu.VMEM((2, 8, t), F32),
            pltpu.VMEM((2, 8, t), F32),
            pltpu.VMEM((2, V_DIM, t), F32),
            pltpu.VMEM((2, 2, t, t), F32),
        ],
    )
    return pl.pallas_call(
        _attn_kernel,
        grid_spec=grid_spec,
        out_shape=jax.ShapeDtypeStruct((s_len, ATTN_WIDTH), BF16),
        compiler_params=_cparams(("arbitrary", "arbitrary")),
        name="diff_attn",
    )(slopes, proj, proj, vt, lq1, lk1, lq2, lk2, gain)


def _pool_kernel(u_ref, halo_ref, w_ref, sc_ref, o_ref, ext_scr):
    i = pl.program_id(0)
    tm = POOL_TM
    hl = POOL_HALO
    halo = halo_ref[...].astype(F32)
    ext_scr[0:hl, :] = jnp.where(i > 0, halo, 0.0)
    ext_scr[hl:hl + tm, :] = u_ref[...].astype(F32)
    pos = i * tm + lax.broadcasted_iota(jnp.int32, (tm, 1), 0)
    for g, win in enumerate(POOL_WINDOWS):
        c0, c1 = g * POOL_GROUP_WIDTH, (g + 1) * POOL_GROUP_WIDTH
        tok = ext_scr[hl:hl + tm, c0:c1]
        wsum = tok
        for dlt in range(1, win):
            wsum = wsum + ext_scr[hl - dlt:hl - dlt + tm, c0:c1]
        count = jnp.minimum(pos + 1, win).astype(F32)
        pooled = wsum / count - tok
        y = jnp.dot(pooled.astype(BF16), w_ref[g], preferred_element_type=F32)
        o_ref[:, c0:c1] = (y * sc_ref[:, c0:c1]).astype(o_ref.dtype)


def _pool(proj, pool_w_bf, pool_scale):
    s_len = proj.shape[0]
    ucol = (2 * QK_WIDTH + ATTN_WIDTH) // POOL_WIDTH
    rb = POOL_TM // POOL_HALO
    return pl.pallas_call(
        _pool_kernel,
        grid=(s_len // POOL_TM,),
        in_specs=[
            pl.BlockSpec((POOL_TM, POOL_WIDTH), lambda i: (i, ucol)),
            pl.BlockSpec((POOL_HALO, POOL_WIDTH), lambda i: (jnp.maximum(i * rb - 1, 0), ucol)),
            pl.BlockSpec((N_POOL_GROUPS, POOL_GROUP_WIDTH, POOL_GROUP_WIDTH), lambda i: (0, 0, 0)),
            pl.BlockSpec((1, POOL_WIDTH), lambda i: (0, 0)),
        ],
        out_specs=pl.BlockSpec((POOL_TM, POOL_WIDTH), lambda i: (i, 0)),
        out_shape=jax.ShapeDtypeStruct((s_len, POOL_WIDTH), BF16),
        scratch_shapes=[pltpu.VMEM((POOL_HALO + POOL_TM, POOL_WIDTH), F32)],
        compiler_params=_cparams(("arbitrary",)),
        name="pool_mixer",
    )(proj, proj, pool_w_bf, pool_scale.reshape(1, POOL_WIDTH))


def _outproj_kernel(a_ref, p_ref, wa_ref, wp_ref, x_ref, g_ref, o_ref, wa_bf, wp_bf):
    @pl.when(pl.program_id(1) == 0)
    def _():
        wa_bf[...] = wa_ref[...].astype(BF16)
        wp_bf[...] = wp_ref[...].astype(BF16)

    acc = jnp.dot(a_ref[...], wa_bf[...], preferred_element_type=F32)
    acc = acc + jnp.dot(p_ref[...], wp_bf[...], preferred_element_type=F32)
    o_ref[...] = x_ref[...] + g_ref[...] * acc


def _outproj(o_attn, o_pool, w_out, x2d, mod3):
    t, d = x2d.shape
    return pl.pallas_call(
        _outproj_kernel,
        grid=(d // OUT_TN, t // OUT_TM),
        in_specs=[
            pl.BlockSpec((OUT_TM, ATTN_WIDTH), lambda j, i: (i, 0)),
            pl.BlockSpec((OUT_TM, POOL_WIDTH), lambda j, i: (i, 0)),
            pl.BlockSpec((ATTN_WIDTH, OUT_TN), lambda j, i: (0, j)),
            pl.BlockSpec((POOL_WIDTH, OUT_TN), lambda j, i: (1, j)),
            pl.BlockSpec((OUT_TM, OUT_TN), lambda j, i: (i, j)),
            pl.BlockSpec((None, 1, OUT_TN), lambda j, i: (2, 0, j)),
        ],
        out_specs=pl.BlockSpec((OUT_TM, OUT_TN), lambda j, i: (i, j)),
        out_shape=jax.ShapeDtypeStruct((t, d), F32),
        scratch_shapes=[pltpu.VMEM((ATTN_WIDTH, OUT_TN), BF16), pltpu.VMEM((POOL_WIDTH, OUT_TN), BF16)],
        compiler_params=_cparams(("arbitrary", "arbitrary")),
        name="out_proj",
    )(o_attn, o_pool, w_out, w_out, x2d, mod3)


def _route_kernel(x_ref, g_ref, sc_ref, sh_ref, whi_ref, wlo_ref, rb_ref,
                  h_ref, route_ref, cnt_ref, base_scr):
    step = pl.program_id(0)
    tm = NORM_TM

    @pl.when(step == 0)
    def _():
        base_scr[...] = jnp.zeros(base_scr.shape, F32)

    h2 = _norm_mod(x_ref[...], g_ref[...], sc_ref[...], sh_ref[...])
    half = D_MODEL // 2
    lo = pltpu.bitcast(h2[:, :half].astype(BF16).astype(F32), jnp.uint32)
    hi = pltpu.bitcast(h2[:, half:].astype(BF16).astype(F32), jnp.uint32)
    word = hi | (lo >> 16)
    for g in range(half // LANES):
        h_ref[:, g, :] = word[:, g * LANES:(g + 1) * LANES]

    h_hi = h2.astype(BF16)
    h_lo = (h2 - h_hi.astype(F32)).astype(BF16)
    w_hi = whi_ref[...]
    lg = (jnp.dot(h_hi, w_hi, preferred_element_type=F32)
          + jnp.dot(h_lo, w_hi, preferred_element_type=F32)
          + jnp.dot(h_hi, wlo_ref[...], preferred_element_type=F32)) + rb_ref[...]

    lane = lax.broadcasted_iota(jnp.int32, (tm, ROUTE_LANES), 1)

    def first_max(vals):
        v = jnp.max(vals, axis=1, keepdims=True)
        idx = jnp.min(jnp.where(vals == v, lane, ROUTE_LANES), axis=1, keepdims=True)
        return v, idx

    gl = jnp.where(lane < N_EXPERT_GROUPS, lg, NEG_BIG)
    gmax, g_sel = first_max(gl)
    p_g = 1.0 / jnp.sum(jnp.exp(gl - gmax), axis=1, keepdims=True)

    e_lo = EXPERT_LANE0 + g_sel * EXPERTS_PER_GROUP
    el = jnp.where(jnp.logical_and(lane >= e_lo, lane < e_lo + EXPERTS_PER_GROUP), lg, NEG_BIG)
    v1, j1 = first_max(el)
    el2 = jnp.where(lane == j1, NEG_BIG, el)
    v2, j2 = first_max(el2)
    e2 = jnp.exp(v2 - v1)
    gate1 = p_g / (1.0 + e2)
    gate2 = p_g * e2 / (1.0 + e2)

    oh1 = (lane == j1).astype(BF16)
    oh2 = (lane == j2).astype(BF16)
    r_i = lax.broadcasted_iota(jnp.int32, (tm, tm), 0)
    c_i = lax.broadcasted_iota(jnp.int32, (tm, tm), 1)
    lower = (c_i < r_i).astype(BF16)
    before1 = jnp.dot(lower, oh1, preferred_element_type=F32)
    before2 = jnp.dot(lower, oh2, preferred_element_type=F32)
    oh1f = oh1.astype(F32)
    oh2f = oh2.astype(F32)
    tot1 = jnp.sum(oh1f, axis=0, keepdims=True)
    tot2 = jnp.sum(oh2f, axis=0, keepdims=True)
    base = base_scr[0:1, :]
    rank1 = jnp.sum((base + before1) * oh1f, axis=1, keepdims=True)
    rank2 = jnp.sum((base + tot1 + before2) * oh2f, axis=1, keepdims=True)
    new_base = base + tot1 + tot2
    base_scr[0:1, :] = new_base
    cnt_ref[...] = new_base

    eid1 = (j1 - EXPERT_LANE0).astype(F32)
    eid2 = (j2 - EXPERT_LANE0).astype(F32)
    packed = jnp.zeros((tm, ROUTE_LANES), F32)
    for k, val in enumerate((eid1, eid2, rank1, rank2, gate1, gate2)):
        packed = jnp.where(lane == k, val, packed)
    route_ref[...] = packed


def _norm2_route(x1, g, mod3, w_hi, w_lo, rbias):
    t, d = x1.shape
    const = lambda shape: pl.BlockSpec(shape, lambda i: (0,) * len(shape))
    return pl.pallas_call(
        _route_kernel,
        grid=(t // NORM_TM,),
        in_specs=[
            pl.BlockSpec((NORM_TM, d), lambda i: (i, 0)),
            const((1, d)),
            _mod_spec(4),
            _mod_spec(3),
            const((d, ROUTE_LANES)),
            const((d, ROUTE_LANES)),
            const((1, ROUTE_LANES)),
        ],
        out_specs=[
            pl.BlockSpec((NORM_TM, d // 2 // LANES, LANES), lambda i: (i, 0, 0)),
            pl.BlockSpec((NORM_TM, ROUTE_LANES), lambda i: (i, 0)),
            const((1, ROUTE_LANES)),
        ],
        out_shape=[
            jax.ShapeDtypeStruct((t, d // 2 // LANES, LANES), jnp.uint32),
            jax.ShapeDtypeStruct((t, ROUTE_LANES), F32),
            jax.ShapeDtypeStruct((1, ROUTE_LANES), F32),
        ],
        scratch_shapes=[pltpu.VMEM((8, ROUTE_LANES), F32)],
        compiler_params=_cparams(("arbitrary",)),
        name="norm2_route",
    )(x1, g.reshape(1, d), mod3, mod3, w_hi, w_lo, rbias)


META_E, META_ROW0, META_NSUB, META_MISC = 0, 1, 2, 3
MISC_NACT, MISC_TAIL_ROW0, MISC_TAIL_PIECES = 0, 1, 2


def _rest_blocks(nsub, fn):
    rest = nsub - 1
    r0 = jnp.int32(MOE_SB)
    for k in (4, 2, 1):
        take = (rest & k) != 0

        @pl.when(take)
        def _(r0=r0, k=k):
            fn(pl.multiple_of(r0, MOE_SB), k * MOE_SB)

        r0 = r0 + jnp.where(take, k * MOE_SB, 0)


def _ffn_kernel(meta_ref, idx_ref, idxn_ref, hp_hbm, wg_ref, wu_ref, wd_ref, y_hbm,
                xw, x_bf, h_scr, wgu_bf, wd_bf, obuf, gsem, osem):
    c = pl.program_id(0)
    s = pl.program_id(1)
    nact = meta_ref[META_MISC, MISC_NACT]
    active = c < nact
    nsub = meta_ref[META_NSUB, c]
    row0 = meta_ref[META_ROW0, c]
    half = D_MODEL // 2
    groups = half // LANES
    sb_shift = int(math.log2(MOE_SB))

    def start_gather(ids_ref, n_sub):
        def issue(r, carry):
            dst = xw.at[pl.ds(pl.multiple_of(r * groups, groups), groups), :]
            pltpu.make_async_copy(hp_hbm.at[ids_ref[0, r]], dst, gsem).start()
            return carry

        lax.fori_loop(0, lax.shift_left(n_sub, sb_shift), issue, 0)

    def wait_gather(n_sub):
        def wait(i, carry):
            pltpu.make_async_copy(xw.at[pl.ds(0, MOE_SB * groups), :], xw.at[pl.ds(0, MOE_SB * groups), :],
                                  gsem).wait()
            return carry

        lax.fori_loop(0, n_sub, wait, 0)

    def out_copy(slot, i, col):
        r = pl.multiple_of(i * MOE_SB, MOE_SB)
        return pltpu.make_async_copy(
            obuf.at[slot, pl.ds(r, MOE_SB), :],
            y_hbm.at[pl.ds(pl.multiple_of(row0 + r, MOE_SB), MOE_SB), pl.ds(col * MOE_TN, MOE_TN)],
            osem.at[slot])

    def wait_out(slot, n_sub):
        def wait(i, carry):
            out_copy(slot, 0, 0).wait()
            return carry

        lax.fori_loop(0, n_sub, wait, 0)

    @pl.when(jnp.logical_and(active, s == 0))
    def _():
        @pl.when(c == 0)
        def _():
            start_gather(idx_ref, nsub)

        wait_gather(nsub)

        def unpack(i, carry):
            r0 = pl.multiple_of(i * MOE_SB, MOE_SB)
            for g in range(groups):
                w = xw[pl.ds(r0 * groups + g, MOE_SB, stride=groups), :]
                c0 = g * LANES
                x_bf[pl.ds(r0, MOE_SB), c0:c0 + LANES] = pltpu.bitcast(w << 16, F32).astype(BF16)
                x_bf[pl.ds(r0, MOE_SB), half + c0:half + c0 + LANES] = pltpu.bitcast(
                    w & jnp.uint32(0xFFFF0000), F32).astype(BF16)
            return carry

        lax.fori_loop(0, nsub, unpack, 0)

    @pl.when(jnp.logical_and(c + 1 < nact, s == 1))
    def _():
        start_gather(idxn_ref, meta_ref[META_NSUB, c + 1])

    @pl.when(jnp.logical_and(active, s < MOE_NF))
    def _():
        def gate_up(r0, rows, wgu):
            xs = x_bf[pl.ds(r0, rows), :]
            gu = jnp.dot(xs, wgu, preferred_element_type=F32)
            gt = gu[:, :MOE_TF]
            up = gu[:, MOE_TF:]
            h_scr[s, pl.ds(r0, rows), :] = (jax.nn.silu(gt) * up).astype(BF16)

        wgu = jnp.concatenate([wg_ref[...].astype(BF16), wu_ref[...].astype(BF16)], axis=1)
        wgu_bf[...] = wgu
        gate_up(0, MOE_SB, wgu)
        _rest_blocks(nsub, lambda r0, rows: gate_up(r0, rows, wgu_bf[...]))

    @pl.when(jnp.logical_and(active, s >= MOE_NF))
    def _():
        n = s - MOE_NF
        slot = n & 1

        @pl.when(n >= 2)
        def _():
            wait_out(slot, nsub)

        def down(r0, rows, wd):
            hs = jnp.concatenate([h_scr[f, pl.ds(r0, rows), :] for f in range(MOE_NF)], axis=1)
            obuf[slot, pl.ds(r0, rows), :] = jnp.dot(hs, wd, preferred_element_type=F32)

        wd = wd_ref[...].astype(BF16)
        wd_bf[...] = wd
        down(0, MOE_SB, wd)
        _rest_blocks(nsub, lambda r0, rows: down(r0, rows, wd_bf[...]))

        for col in range(MOE_NN):
            @pl.when(n == col)
            def _(col=col):
                def send(i, carry):
                    out_copy(slot, i, col).start()
                    return carry

                lax.fori_loop(0, nsub, send, 0)

        @pl.when(n == MOE_NN - 1)
        def _():
            wait_out(1 - slot, nsub)
            wait_out(slot, nsub)

    @pl.when(jnp.logical_and(jnp.logical_not(active), s == 0))
    def _():
        per_chunk = MOE_RC // MOE_SB
        first = (c - nact) * per_chunk
        n_here = jnp.clip(meta_ref[META_MISC, MISC_TAIL_PIECES] - first, 0, per_chunk)
        tail0 = meta_ref[META_MISC, MISC_TAIL_ROW0]
        obuf[0, 0:MOE_SB, :] = jnp.zeros((MOE_SB, MOE_TN), F32)

        def zcopy(k, col):
            r = pl.multiple_of(tail0 + (first + k) * MOE_SB, MOE_SB)
            return pltpu.make_async_copy(
                obuf.at[0, pl.ds(0, MOE_SB), :],
                y_hbm.at[pl.ds(r, MOE_SB), pl.ds(col * MOE_TN, MOE_TN)],
                osem.at[0])

        def send(k, carry):
            for col in range(MOE_NN):
                zcopy(k, col).start()
            return carry

        def wait(k, carry):
            for col in range(MOE_NN):
                zcopy(k, col).wait()
            return carry

        lax.fori_loop(0, n_here, send, 0)
        lax.fori_loop(0, n_here, wait, 0)


def _ffn(meta, idx_tab, hp, w_gate, w_up, w_down, n_rows):
    d = D_MODEL
    nc = idx_tab.shape[0]
    last = MOE_NF + MOE_NN - 1

    def eff(c, s, m):
        nact = m[META_MISC, MISC_NACT]
        act = c < nact
        return jnp.where(act, c, nact - 1), jnp.where(act, s, last)

    def idx_map(c, s, m):
        return (c, 0, 0)

    def idxn_map(c, s, m):
        return (jnp.minimum(c + 1, nc - 1), 0, 0)

    def w1_map(c, s, m):
        cc, ss = eff(c, s, m)
        return (m[META_E, cc], 0, jnp.minimum(ss, MOE_NF - 1))

    def wd_map(c, s, m):
        cc, ss = eff(c, s, m)
        return (m[META_E, cc], 0, jnp.maximum(ss - MOE_NF, 0))

    grid_spec = pltpu.PrefetchScalarGridSpec(
        num_scalar_prefetch=1,
        grid=(nc, MOE_NF + MOE_NN),
        in_specs=[
            pl.BlockSpec((None, 1, MOE_RC), idx_map, memory_space=pltpu.SMEM),
            pl.BlockSpec((None, 1, MOE_RC), idxn_map, memory_space=pltpu.SMEM),
            pl.BlockSpec(memory_space=pl.ANY),
            pl.BlockSpec((None, d, MOE_TF), w1_map),
            pl.BlockSpec((None, d, MOE_TF), w1_map),
            pl.BlockSpec((None, EXPERT_FF, MOE_TN), wd_map),
        ],
        out_specs=pl.BlockSpec(memory_space=pl.ANY),
        scratch_shapes=[
            pltpu.VMEM((MOE_RC * (d // 2 // LANES), LANES), jnp.uint32),
            pltpu.VMEM((MOE_RC, d), BF16),
            pltpu.VMEM((MOE_NF, MOE_RC, MOE_TF), BF16),
            pltpu.VMEM((d, 2 * MOE_TF), BF16),
            pltpu.VMEM((EXPERT_FF, MOE_TN), BF16),
            pltpu.VMEM((2, MOE_RC, MOE_TN), F32),
            pltpu.SemaphoreType.DMA(()),
            pltpu.SemaphoreType.DMA((2,)),
        ],
    )
    return pl.pallas_call(
        _ffn_kernel,
        grid_spec=grid_spec,
        out_shape=jax.ShapeDtypeStruct((n_rows, d), F32),
        compiler_params=_cparams(("arbitrary", "arbitrary")),
        name="moe_ffn",
    )(meta, idx_tab, idx_tab, hp, w_gate, w_up, w_down)


def _combine_kernel(d0_ref, d1_ref, x_ref, route_ref, g2_ref, fg_ref, y_hbm, o_ref, buf0, buf1, sem0, sem1):
    tm = COMB_TM

    def issue(r, c):
        pltpu.make_async_copy(y_hbm.at[pl.ds(d0_ref[0, r], 1), :], buf0.at[pl.ds(r, 1), :], sem0).start()
        pltpu.make_async_copy(y_hbm.at[pl.ds(d1_ref[0, r], 1), :], buf1.at[pl.ds(r, 1), :], sem1).start()
        return c

    lax.fori_loop(0, tm, issue, 0)
    pltpu.make_async_copy(y_hbm.at[pl.ds(0, tm), :], buf0, sem0).wait()
    pltpu.make_async_copy(y_hbm.at[pl.ds(0, tm), :], buf1, sem1).wait()
    route = route_ref[...]
    y = buf0[...] * route[:, 4:5] + buf1[...] * route[:, 5:6]
    x2 = x_ref[...] + g2_ref[...] * y
    ms = jnp.mean(x2 * x2, axis=-1, keepdims=True)
    o_ref[...] = (x2 * lax.rsqrt(ms + RMS_EPS)) * fg_ref[...]


def _combine(dest0, dest1, x1, route, mod3, final_g, y_sorted):
    t, d = x1.shape
    nb = t // COMB_TM
    smem_idx = lambda: pl.BlockSpec((None, 1, COMB_TM), lambda i: (i, 0, 0), memory_space=pltpu.SMEM)
    return pl.pallas_call(
        _combine_kernel,
        grid=(nb,),
        in_specs=[
            smem_idx(),
            smem_idx(),
            pl.BlockSpec((COMB_TM, d), lambda i: (i, 0)),
            pl.BlockSpec((COMB_TM, ROUTE_LANES), lambda i: (i, 0)),
            _mod_spec(5),
            pl.BlockSpec((1, d), lambda i: (0, 0)),
            pl.BlockSpec(memory_space=pl.ANY),
        ],
        out_specs=pl.BlockSpec((COMB_TM, d), lambda i: (i, 0)),
        out_shape=jax.ShapeDtypeStruct((t, d), F32),
        scratch_shapes=[
            pltpu.VMEM((COMB_TM, d), F32),
            pltpu.VMEM((COMB_TM, d), F32),
            pltpu.SemaphoreType.DMA(()),
            pltpu.SemaphoreType.DMA(()),
        ],
        compiler_params=_cparams(("arbitrary",)),
        name="moe_combine",
    )(dest0.reshape(nb, 1, COMB_TM), dest1.reshape(nb, 1, COMB_TM), x1, route, mod3,
      final_g.reshape(1, d), y_sorted)


def _dispatch_plan(route, cnt, n_tokens):
    n_assign = n_tokens * TOP_K
    n_rows = n_assign + N_EXPERTS * MOE_SB
    nc = n_assign // MOE_RC + N_EXPERTS
    i32 = jnp.int32
    eid = route[:, 0:2].astype(i32)
    rank = route[:, 2:4].astype(i32)
    counts = cnt[0, EXPERT_LANE0:EXPERT_LANE0 + N_EXPERTS].astype(i32)
    seg_rows = (counts + MOE_SB - 1) // MOE_SB * MOE_SB
    seg_end = jnp.cumsum(seg_rows)
    seg_start = seg_end - seg_rows
    e_hot = eid[:, :, None] == jnp.arange(N_EXPERTS, dtype=i32)
    lookup = lambda tab: jnp.sum(jnp.where(e_hot, tab, 0), axis=-1)
    dest = lookup(seg_start) + rank
    nchunk = (seg_rows + MOE_RC - 1) // MOE_RC
    ch_end = jnp.cumsum(nchunk)
    ch_start = ch_end - nchunk
    nact = ch_end[-1]
    cidx = jnp.arange(nc, dtype=i32)
    ch_e = jnp.minimum(jnp.searchsorted(ch_end, cidx, side="right"), N_EXPERTS - 1).astype(i32)
    k = cidx - ch_start[ch_e]
    ch_row0 = jnp.where(cidx < nact, seg_start[ch_e] + k * MOE_RC, 0)
    ch_nsub = jnp.where(cidx < nact, jnp.clip(seg_rows[ch_e] - k * MOE_RC, 0, MOE_RC) // MOE_SB, 0)
    misc = jnp.zeros((nc,), i32).at[MISC_NACT].set(nact)
    misc = misc.at[MISC_TAIL_ROW0].set(seg_end[-1]).at[MISC_TAIL_PIECES].set((n_rows - seg_end[-1]) // MOE_SB)
    meta = jnp.stack([ch_e, ch_row0, ch_nsub, misc]).astype(i32)
    slot = (lookup(ch_start) + rank // MOE_RC) * MOE_RC + rank % MOE_RC
    tok = jnp.broadcast_to(jnp.arange(n_tokens, dtype=i32)[:, None], (n_tokens, TOP_K))
    idx_tab = (jnp.arange(nc * MOE_RC, dtype=i32) % n_tokens).at[slot.reshape(-1)].set(tok.reshape(-1))
    return dest, meta, idx_tab.reshape(nc, 1, MOE_RC), n_rows


def kernel(x, c, ada_w, ada_b, norm1_g, w_in, lambda_q1, lambda_k1, lambda_q2, lambda_k2, subln_g,
           pool_w, pool_scale, w_out, norm2_g, router_group_w, router_group_b, router_expert_w,
           router_expert_b, expert_w_gate, expert_w_up, expert_w_down, final_norm_g):
    b_, s_, d = x.shape
    assert b_ == 1 and d == D_MODEL and ada_w.shape[0] == 1
    t = b_ * s_
    x2d = x.reshape(t, d)

    mod3 = _ada_mod(c, ada_w[0], ada_b[0]).reshape(6, 1, d)

    h = _norm1(x2d, norm1_g[0], mod3)
    proj = _inproj(h, w_in[0])
    slopes = 2.0 ** (-8.0 * jnp.arange(1, N_HEADS + 1, dtype=F32) / N_HEADS)
    o_attn = _attention(proj, slopes, lambda_q1, lambda_k1, lambda_q2, lambda_k2, subln_g)
    o_pool = _pool(proj, pool_w[0].astype(BF16), pool_scale[0])
    x1 = _outproj(o_attn, o_pool, w_out[0], x2d, mod3)

    rw = jnp.zeros((d, ROUTE_LANES), F32)
    rw = rw.at[:, :N_EXPERT_GROUPS].set(router_group_w[0])
    rw = rw.at[:, EXPERT_LANE0:EXPERT_LANE0 + N_EXPERTS].set(router_expert_w[0])
    rw_hi = rw.astype(BF16)
    rw_lo = (rw - rw_hi.astype(F32)).astype(BF16)
    rbias = jnp.full((1, ROUTE_LANES), NEG_BIG, F32)
    rbias = rbias.at[0, :N_EXPERT_GROUPS].set(router_group_b[0])
    rbias = rbias.at[0, EXPERT_LANE0:EXPERT_LANE0 + N_EXPERTS].set(router_expert_b[0].reshape(-1))
    h2, route, cnt = _norm2_route(x1, norm2_g[0], mod3, rw_hi, rw_lo, rbias)

    dest, meta, idx_tab, n_rows = _dispatch_plan(route, cnt, t)
    y_sorted = _ffn(meta, idx_tab, h2, expert_w_gate[0], expert_w_up[0], expert_w_down[0], n_rows)
    out = _combine(dest[:, 0], dest[:, 1], x1, route, mod3, final_norm_g, y_sorted)
    return out.reshape(b_, s_, d)
```

```python
import functools
import math

import jax
import jax.numpy as jnp
from jax import lax
from jax.experimental import pallas as pl
from jax.experimental.pallas import tpu as pltpu

F32 = jnp.float32
BF16 = jnp.bfloat16

D_MODEL = 4096
ATTN_WIDTH = 2048
POOL_WIDTH = 2048
HEAD_DIM = 128
V_DIM = 2 * HEAD_DIM
N_HEADS = ATTN_WIDTH // V_DIM
QK_WIDTH = N_HEADS * 2 * HEAD_DIM
POOL_WINDOWS = (2, 4, 8, 16)
N_POOL_GROUPS = len(POOL_WINDOWS)
POOL_GROUP_WIDTH = POOL_WIDTH // N_POOL_GROUPS
IN_PROJ_WIDTH = 2 * QK_WIDTH + ATTN_WIDTH + POOL_WIDTH
N_EXPERT_GROUPS = 4
EXPERTS_PER_GROUP = 8
N_EXPERTS = N_EXPERT_GROUPS * EXPERTS_PER_GROUP
TOP_K = 2
EXPERT_FF = 1536
RMS_EPS = 1e-6
LAMBDA_INIT = 0.8 - 0.6 * math.exp(-0.3 * 0)

LANES = 128
VMEM_LIMIT = 56 * 1024 * 1024
NEG_BIG = -1e30
LOG2E = math.log2(math.e)

ADA_TN = 512
NORM_TM = 256
MM_TM = 1024
MM_TN = 512
ATT_T = 512
POOL_TM = 512
POOL_HALO = 16
OUT_TM = 1024
OUT_TN = 512
ROUTE_LANES = LANES
EXPERT_LANE0 = N_EXPERT_GROUPS
MOE_RC = 1024
MOE_SB = 128
MOE_TF = 128
MOE_NF = EXPERT_FF // MOE_TF
MOE_TN = 256
MOE_NN = D_MODEL // MOE_TN
MOE_RING = 4
MOE_AHEAD = MOE_RING - 1
COMB_TM = 256


def _cparams(sem):
    return pltpu.CompilerParams(dimension_semantics=sem, vmem_limit_bytes=VMEM_LIMIT)


def _ada_kernel(c_ref, w_ref, b_ref, o_ref):
    d, tn = w_ref.shape
    ch = 256
    acc = jnp.zeros((8, tn), F32)
    for r in range(d // ch):
        cc = c_ref[r * ch:(r + 1) * ch, :]
        cc = cc * jax.nn.sigmoid(cc)
        w = w_ref[r * ch:(r + 1) * ch, :]
        acc = acc + (w * cc).reshape(ch // 8, 8, tn).sum(axis=0)
    o_ref[...] = acc.sum(axis=0, keepdims=True) + b_ref[...]


def _ada_mod(c, ada_w, ada_b):
    d, n = ada_w.shape
    return pl.pallas_call(
        _ada_kernel,
        grid=(n // ADA_TN,),
        in_specs=[
            pl.BlockSpec((d, 1), lambda j: (0, 0)),
            pl.BlockSpec((d, ADA_TN), lambda j: (0, j)),
            pl.BlockSpec((1, ADA_TN), lambda j: (0, j)),
        ],
        out_specs=pl.BlockSpec((1, ADA_TN), lambda j: (0, j)),
        out_shape=jax.ShapeDtypeStruct((1, n), F32),
        compiler_params=_cparams(("arbitrary",)),
        name="ada_mod",
    )(c.reshape(d, 1), ada_w, ada_b.reshape(1, n))


def _norm_mod(x, g, sc, sh):
    ms = jnp.mean(x * x, axis=-1, keepdims=True)
    return (x * lax.rsqrt(ms + RMS_EPS)) * g * (1.0 + sc) + sh


def _norm1_kernel(x_ref, g_ref, sc_ref, sh_ref, o_ref):
    o_ref[...] = _norm_mod(x_ref[...], g_ref[...], sc_ref[...], sh_ref[...]).astype(o_ref.dtype)


def _mod_spec(row):
    return pl.BlockSpec((None, 1, D_MODEL), lambda i, row=row: (row, 0, 0))


def _norm1(x2d, g, mod3):
    t, d = x2d.shape
    return pl.pallas_call(
        _norm1_kernel,
        grid=(t // NORM_TM,),
        in_specs=[
            pl.BlockSpec((NORM_TM, d), lambda i: (i, 0)),
            pl.BlockSpec((1, d), lambda i: (0, 0)),
            _mod_spec(1),
            _mod_spec(0),
        ],
        out_specs=pl.BlockSpec((NORM_TM, d), lambda i: (i, 0)),
        out_shape=jax.ShapeDtypeStruct((t, d), BF16),
        compiler_params=_cparams(("arbitrary",)),
        name="norm1_mod",
    )(x2d, g.reshape(1, d), mod3, mod3)


def _inproj_kernel(h_ref, w_ref, o_ref, w_bf, *, n_q_tiles, scale):
    j = pl.program_id(0)

    @pl.when(pl.program_id(1) == 0)
    def _():
        w_bf[...] = w_ref[...].astype(BF16)

    acc = jnp.dot(h_ref[...], w_bf[...], preferred_element_type=F32)
    s = jnp.where(j < n_q_tiles, scale, 1.0).astype(F32)
    o_ref[...] = (acc * s).astype(o_ref.dtype)


def _inproj(h, w):
    t, d = h.shape
    n = w.shape[1]
    kern = functools.partial(_inproj_kernel, n_q_tiles=QK_WIDTH // MM_TN, scale=LOG2E * HEAD_DIM ** -0.5)
    return pl.pallas_call(
        kern,
        grid=(n // MM_TN, t // MM_TM),
        in_specs=[
            pl.BlockSpec((MM_TM, d), lambda j, i: (i, 0)),
            pl.BlockSpec((d, MM_TN), lambda j, i: (0, j)),
        ],
        out_specs=pl.BlockSpec((MM_TM, MM_TN), lambda j, i: (i, j)),
        out_shape=jax.ShapeDtypeStruct((t, n), BF16),
        scratch_shapes=[pltpu.VMEM((d, MM_TN), BF16)],
        compiler_params=_cparams(("arbitrary", "arbitrary")),
        name="in_proj",
    )(h, w)


def _attn_kernel(slopes_ref, q_ref, k_ref, vt_ref, lq1_ref, lk1_ref, lq2_ref, lk2_ref, sg_ref,
                 o_ref, m_scr, l_scr, acc_scr, s_scr):
    h = pl.program_id(0)
    i = pl.program_id(1)
    t = ATT_T
    slope2 = slopes_ref[h] * LOG2E
    q0 = i * t

    m_scr[...] = jnp.full(m_scr.shape, NEG_BIG, F32)
    l_scr[...] = jnp.zeros(l_scr.shape, F32)
    acc_scr[...] = jnp.zeros(acc_scr.shape, F32)

    def colreduce(x, op):
        part = op(x.reshape(t // 8, 8, t), axis=0)
        return jnp.broadcast_to(op(part, axis=0, keepdims=True), (8, t))

    def scores(j, buf):
        k0 = pl.multiple_of(j * t, t)
        kblk = k_ref[pl.ds(k0, t), :]
        for mp in range(2):
            qm = q_ref[:, mp * HEAD_DIM:(mp + 1) * HEAD_DIM]
            km = kblk[:, mp * HEAD_DIM:(mp + 1) * HEAD_DIM]
            s_scr[buf, mp] = lax.dot_general(km, qm, (((1,), (1,)), ((), ())),
                                             preferred_element_type=F32)

    def update(j, buf, masked):
        vtb = vt_ref[j]
        kidx = lax.broadcasted_iota(jnp.int32, (t, LANES), 0)
        bias = jnp.tile(slope2 * (kidx + (j * t - q0)).astype(F32), (1, t // LANES))
        if masked:
            keep = (lax.broadcasted_iota(jnp.int32, (t, t), 1)
                    >= lax.broadcasted_iota(jnp.int32, (t, t), 0))
        for mp in range(2):
            s = s_scr[buf, mp] + bias
            if masked:
                s = jnp.where(keep, s, NEG_BIG)
            m_prev = m_scr[mp]
            m_new = jnp.maximum(m_prev, colreduce(s, jnp.max))
            alpha = jnp.exp2(m_prev - m_new)
            p = jnp.exp2(s - jnp.tile(m_new, (t // 8, 1)))
            l_scr[mp] = alpha * l_scr[mp] + colreduce(p, jnp.sum)
            m_scr[mp] = m_new
            acc_scr[mp] = acc_scr[mp] * jnp.tile(alpha, (V_DIM // 8, 1)) + jnp.dot(
                vtb, p.astype(BF16), preferred_element_type=F32)

    scores(0, 0)

    def pairs(j, n_pairs):
        for q in range(n_pairs):
            update(j + 2 * q, 0, False)
            scores(j + 2 * q + 1, 1)
            update(j + 2 * q + 1, 1, False)
            scores(j + 2 * q + 2, 0)

    def quad(p, c):
        pairs(4 * p, 2)
        return c

    n_quads = lax.shift_right_logical(i, 2)
    lax.fori_loop(0, n_quads, quad, 0)

    @pl.when((i & 2) != 0)
    def _():
        pairs(4 * n_quads, 1)

    odd = (i & 1) == 1

    @pl.when(odd)
    def _():
        update(i - 1, 0, False)
        scores(i, 1)
        update(i, 1, True)

    @pl.when(jnp.logical_not(odd))
    def _():
        update(i, 0, True)

    lam = (jnp.exp(jnp.sum(lq1_ref[...] * lk1_ref[...], axis=1, keepdims=True))
           - jnp.exp(jnp.sum(lq2_ref[...] * lk2_ref[...], axis=1, keepdims=True))
           + LAMBDA_INIT)
    o1 = acc_scr[0] / jnp.tile(l_scr[0], (V_DIM // 8, 1))
    o2 = acc_scr[1] / jnp.tile(l_scr[1], (V_DIM // 8, 1))
    o = o1 - lam * o2
    ms = jnp.mean(o * o, axis=0, keepdims=True)
    gain = jnp.tile(sg_ref[...], (1, t // LANES)) * (1.0 - LAMBDA_INIT)
    o = (o * lax.rsqrt(ms + RMS_EPS)) * gain
    o_ref[...] = o.T.astype(o_ref.dtype)


def _attention(proj, slopes, lq1, lk1, lq2, lk2, subln_g):
    s_len = proj.shape[0]
    t = ATT_T
    nblk = s_len // t
    kcol0 = QK_WIDTH // V_DIM
    v = proj[:, 2 * QK_WIDTH:2 * QK_WIDTH + ATTN_WIDTH]
    vt = v.reshape(nblk, t, N_HEADS, V_DIM).transpose(2, 0, 3, 1)
    gain = jnp.broadcast_to(subln_g.reshape(V_DIM, 1), (V_DIM, LANES))
    vec = lambda: pl.BlockSpec((1, HEAD_DIM), lambda h, i, sl: (0, 0))
    grid_spec = pltpu.PrefetchScalarGridSpec(
        num_scalar_prefetch=1,
        grid=(N_HEADS, nblk),
        in_specs=[
            pl.BlockSpec((t, V_DIM), lambda h, i, sl: (i, h)),
            pl.BlockSpec((s_len, V_DIM), lambda h, i, sl: (0, kcol0 + h)),
            pl.BlockSpec((None, nblk, V_DIM, t), lambda h, i, sl: (h, 0, 0, 0)),
            vec(), vec(), vec(), vec(),
            pl.BlockSpec((V_DIM, LANES), lambda h, i, sl: (0, 0)),
        ],
        out_specs=pl.BlockSpec((t, V_DIM), lambda h, i, sl: (i, h)),
        scratch_shapes=[
            pltpu.VMEM((2, 8, t), F32),
            pltpu.VMEM((2, 8, t), F32),
            pltpu.VMEM((2, V_DIM, t), F32),
            pltpu.VMEM((2, 2, t, t), F32),
        ],
    )
    return pl.pallas_call(
        _attn_kernel,
        grid_spec=grid_spec,
        out_shape=jax.ShapeDtypeStruct((s_len, ATTN_WIDTH), BF16),
        compiler_params=_cparams(("arbitrary", "arbitrary")),
        name="diff_attn",
    )(slopes, proj, proj, vt, lq1, lk1, lq2, lk2, gain)


def _pool_kernel(u_ref, halo_ref, w_ref, sc_ref, o_ref, ext_scr):
    i = pl.program_id(0)
    tm = POOL_TM
    hl = POOL_HALO
    halo = halo_ref[...].astype(F32)
    ext_scr[0:hl, :] = jnp.where(i > 0, halo, 0.0)
    ext_scr[hl:hl + tm, :] = u_ref[...].astype(F32)
    pos = i * tm + lax.broadcasted_iota(jnp.int32, (tm, 1), 0)
    for g, win in enumerate(POOL_WINDOWS):
        c0, c1 = g * POOL_GROUP_WIDTH, (g + 1) * POOL_GROUP_WIDTH
        tok = ext_scr[hl:hl + tm, c0:c1]
        wsum = tok
        for dlt in range(1, win):
            wsum = wsum + ext_scr[hl - dlt:hl - dlt + tm, c0:c1]
        count = jnp.minimum(pos + 1, win).astype(F32)
        pooled = wsum / count - tok
        y = jnp.dot(pooled.astype(BF16), w_ref[g], preferred_element_type=F32)
        o_ref[:, c0:c1] = (y * sc_ref[:, c0:c1]).astype(o_ref.dtype)


def _pool(proj, pool_w_bf, pool_scale):
    s_len = proj.shape[0]
    ucol = (2 * QK_WIDTH + ATTN_WIDTH) // POOL_WIDTH
    rb = POOL_TM // POOL_HALO
    return pl.pallas_call(
        _pool_kernel,
        grid=(s_len // POOL_TM,),
        in_specs=[
            pl.BlockSpec((POOL_TM, POOL_WIDTH), lambda i: (i, ucol)),
            pl.BlockSpec((POOL_HALO, POOL_WIDTH), lambda i: (jnp.maximum(i * rb - 1, 0), ucol)),
            pl.BlockSpec((N_POOL_GROUPS, POOL_GROUP_WIDTH, POOL_GROUP_WIDTH), lambda i: (0, 0, 0)),
            pl.BlockSpec((1, POOL_WIDTH), lambda i: (0, 0)),
        ],
        out_specs=pl.BlockSpec((POOL_TM, POOL_WIDTH), lambda i: (i, 0)),
        out_shape=jax.ShapeDtypeStruct((s_len, POOL_WIDTH), BF16),
        scratch_shapes=[pltpu.VMEM((POOL_HALO + POOL_TM, POOL_WIDTH), F32)],
        compiler_params=_cparams(("arbitrary",)),
        name="pool_mixer",
    )(proj, proj, pool_w_bf, pool_scale.reshape(1, POOL_WIDTH))


def _outproj_kernel(a_ref, p_ref, wa_ref, wp_ref, x_ref, g_ref, o_ref, wa_bf, wp_bf):
    @pl.when(pl.program_id(1) == 0)
    def _():
        wa_bf[...] = wa_ref[...].astype(BF16)
        wp_bf[...] = wp_ref[...].astype(BF16)

    acc = jnp.dot(a_ref[...], wa_bf[...], preferred_element_type=F32)
    acc = acc + jnp.dot(p_ref[...], wp_bf[...], preferred_element_type=F32)
    o_ref[...] = x_ref[...] + g_ref[...] * acc


def _outproj(o_attn, o_pool, w_out, x2d, mod3):
    t, d = x2d.shape
    return pl.pallas_call(
        _outproj_kernel,
        grid=(d // OUT_TN, t // OUT_TM),
        in_specs=[
            pl.BlockSpec((OUT_TM, ATTN_WIDTH), lambda j, i: (i, 0)),
            pl.BlockSpec((OUT_TM, POOL_WIDTH), lambda j, i: (i, 0)),
            pl.BlockSpec((ATTN_WIDTH, OUT_TN), lambda j, i: (0, j)),
            pl.BlockSpec((POOL_WIDTH, OUT_TN), lambda j, i: (1, j)),
            pl.BlockSpec((OUT_TM, OUT_TN), lambda j, i: (i, j)),
            pl.BlockSpec((None, 1, OUT_TN), lambda j, i: (2, 0, j)),
        ],
        out_specs=pl.BlockSpec((OUT_TM, OUT_TN), lambda j, i: (i, j)),
        out_shape=jax.ShapeDtypeStruct((t, d), F32),
        scratch_shapes=[pltpu.VMEM((ATTN_WIDTH, OUT_TN), BF16), pltpu.VMEM((POOL_WIDTH, OUT_TN), BF16)],
        compiler_params=_cparams(("arbitrary", "arbitrary")),
        name="out_proj",
    )(o_attn, o_pool, w_out, w_out, x2d, mod3)


def _route_kernel(x_ref, g_ref, sc_ref, sh_ref, whi_ref, wlo_ref, rb_ref,
                  h_ref, route_ref, cnt_ref, base_scr):
    step = pl.program_id(0)
    tm = NORM_TM

    @pl.when(step == 0)
    def _():
        base_scr[...] = jnp.zeros(base_scr.shape, F32)

    h2 = _norm_mod(x_ref[...], g_ref[...], sc_ref[...], sh_ref[...])
    half = D_MODEL // 2
    lo = pltpu.bitcast(h2[:, :half].astype(BF16).astype(F32), jnp.uint32)
    hi = pltpu.bitcast(h2[:, half:].astype(BF16).astype(F32), jnp.uint32)
    h_ref[...] = hi | (lo >> 16)

    h_hi = h2.astype(BF16)
    h_lo = (h2 - h_hi.astype(F32)).astype(BF16)
    w_hi = whi_ref[...]
    lg = (jnp.dot(h_hi, w_hi, preferred_element_type=F32)
          + jnp.dot(h_lo, w_hi, preferred_element_type=F32)
          + jnp.dot(h_hi, wlo_ref[...], preferred_element_type=F32)) + rb_ref[...]

    lane = lax.broadcasted_iota(jnp.int32, (tm, ROUTE_LANES), 1)

    def first_max(vals):
        v = jnp.max(vals, axis=1, keepdims=True)
        idx = jnp.min(jnp.where(vals == v, lane, ROUTE_LANES), axis=1, keepdims=True)
        return v, idx

    gl = jnp.where(lane < N_EXPERT_GROUPS, lg, NEG_BIG)
    gmax, g_sel = first_max(gl)
    p_g = 1.0 / jnp.sum(jnp.exp(gl - gmax), axis=1, keepdims=True)

    e_lo = EXPERT_LANE0 + g_sel * EXPERTS_PER_GROUP
    el = jnp.where(jnp.logical_and(lane >= e_lo, lane < e_lo + EXPERTS_PER_GROUP), lg, NEG_BIG)
    v1, j1 = first_max(el)
    el2 = jnp.where(lane == j1, NEG_BIG, el)
    v2, j2 = first_max(el2)
    e2 = jnp.exp(v2 - v1)
    gate1 = p_g / (1.0 + e2)
    gate2 = p_g * e2 / (1.0 + e2)

    oh1 = (lane == j1).astype(BF16)
    oh2 = (lane == j2).astype(BF16)
    r_i = lax.broadcasted_iota(jnp.int32, (tm, tm), 0)
    c_i = lax.broadcasted_iota(jnp.int32, (tm, tm), 1)
    lower = (c_i < r_i).astype(BF16)
    before1 = jnp.dot(lower, oh1, preferred_element_type=F32)
    before2 = jnp.dot(lower, oh2, preferred_element_type=F32)
    oh1f = oh1.astype(F32)
    oh2f = oh2.astype(F32)
    tot1 = jnp.sum(oh1f, axis=0, keepdims=True)
    tot2 = jnp.sum(oh2f, axis=0, keepdims=True)
    base = base_scr[0:1, :]
    rank1 = jnp.sum((base + before1) * oh1f, axis=1, keepdims=True)
    rank2 = jnp.sum((base + tot1 + before2) * oh2f, axis=1, keepdims=True)
    new_base = base + tot1 + tot2
    base_scr[0:1, :] = new_base
    cnt_ref[...] = new_base

    eid1 = (j1 - EXPERT_LANE0).astype(F32)
    eid2 = (j2 - EXPERT_LANE0).astype(F32)
    packed = jnp.zeros((tm, ROUTE_LANES), F32)
    for k, val in enumerate((eid1, eid2, rank1, rank2, gate1, gate2)):
        packed = jnp.where(lane == k, val, packed)
    route_ref[...] = packed


def _norm2_route(x1, g, mod3, w_hi, w_lo, rbias):
    t, d = x1.shape
    const = lambda shape: pl.BlockSpec(shape, lambda i: (0,) * len(shape))
    return pl.pallas_call(
        _route_kernel,
        grid=(t // NORM_TM,),
        in_specs=[
            pl.BlockSpec((NORM_TM, d), lambda i: (i, 0)),
            const((1, d)),
            _mod_spec(4),
            _mod_spec(3),
            const((d, ROUTE_LANES)),
            const((d, ROUTE_LANES)),
            const((1, ROUTE_LANES)),
        ],
        out_specs=[
            pl.BlockSpec((NORM_TM, d // 2), lambda i: (i, 0)),
            pl.BlockSpec((NORM_TM, ROUTE_LANES), lambda i: (i, 0)),
            const((1, ROUTE_LANES)),
        ],
        out_shape=[
            jax.ShapeDtypeStruct((t, d // 2), jnp.uint32),
            jax.ShapeDtypeStruct((t, ROUTE_LANES), F32),
            jax.ShapeDtypeStruct((1, ROUTE_LANES), F32),
        ],
        scratch_shapes=[pltpu.VMEM((8, ROUTE_LANES), F32)],
        compiler_params=_cparams(("arbitrary",)),
        name="norm2_route",
    )(x1, g.reshape(1, d), mod3, mod3, w_hi, w_lo, rbias)


META_E, META_ROW0, META_NSUB, META_MISC = 0, 1, 2, 3
MISC_NACT, MISC_TAIL_ROW0, MISC_TAIL_PIECES = 0, 1, 2


def _rest_blocks(nsub, fn):
    rest = nsub - 1
    r0 = jnp.int32(MOE_SB)
    for k in (4, 2, 1):
        take = (rest & k) != 0

        @pl.when(take)
        def _(r0=r0, k=k):
            fn(pl.multiple_of(r0, MOE_SB), k * MOE_SB)

        r0 = r0 + jnp.where(take, k * MOE_SB, 0)


def _ffn_kernel(meta_ref, idx_ref, idxn_ref, hp_hbm, wg_hbm, wu_hbm, wd_hbm, y_hbm,
                xw, x_bf, h_scr, wgu_f32, wd_f32, wgu_bf, wd_bf, obuf, gsem, osem, wsem, dsem):
    c = pl.program_id(0)
    nact = meta_ref[META_MISC, MISC_NACT]
    active = c < nact
    nsub = meta_ref[META_NSUB, c]
    row0 = meta_ref[META_ROW0, c]
    expert = meta_ref[META_E, c]
    half = D_MODEL // 2
    sb_shift = int(math.log2(MOE_SB))

    def start_gather(ids_ref, n_sub):
        def issue(r, carry):
            pltpu.make_async_copy(hp_hbm.at[pl.ds(ids_ref[0, r], 1), :], xw.at[pl.ds(r, 1), :], gsem).start()
            return carry

        lax.fori_loop(0, lax.shift_left(n_sub, sb_shift), issue, 0)

    def wait_gather(n_sub):
        def wait(i, carry):
            pltpu.make_async_copy(xw.at[pl.ds(0, MOE_SB), :], xw.at[pl.ds(0, MOE_SB), :], gsem).wait()
            return carry

        lax.fori_loop(0, n_sub, wait, 0)

    def gate_up_copies(e, f, slot):
        col = pl.ds(pl.multiple_of(f * MOE_TF, MOE_TF), MOE_TF)
        return (pltpu.make_async_copy(wg_hbm.at[e, :, col], wgu_f32.at[slot, 0], wsem.at[slot]),
                pltpu.make_async_copy(wu_hbm.at[e, :, col], wgu_f32.at[slot, 1], wsem.at[slot]))

    def down_copy(e, n, slot):
        col = pl.ds(pl.multiple_of(n * MOE_TN, MOE_TN), MOE_TN)
        return pltpu.make_async_copy(wd_hbm.at[e, :, col], wd_f32.at[slot], dsem.at[slot])

    def out_copy(slot, i, n):
        r = pl.multiple_of(i * MOE_SB, MOE_SB)
        col = pl.ds(pl.multiple_of(n * MOE_TN, MOE_TN), MOE_TN)
        return pltpu.make_async_copy(
            obuf.at[slot, pl.ds(r, MOE_SB), :],
            y_hbm.at[pl.ds(pl.multiple_of(row0 + r, MOE_SB), MOE_SB), col],
            osem.at[slot])

    def wait_out(slot, n_sub):
        def wait(i, carry):
            out_copy(slot, 0, 0).wait()
            return carry

        lax.fori_loop(0, n_sub, wait, 0)

    def active_chunk():
        @pl.when(c == 0)
        def _():
            start_gather(idx_ref, nsub)
            for f in range(MOE_AHEAD):
                for cp in gate_up_copies(expert, f, f):
                    cp.start()

        wait_gather(nsub)

        def unpack(i, carry):
            r0 = pl.multiple_of(i * MOE_SB, MOE_SB)
            w = xw[pl.ds(r0, MOE_SB), :]
            x_bf[pl.ds(r0, MOE_SB), 0:half] = pltpu.bitcast(w << 16, F32).astype(BF16)
            x_bf[pl.ds(r0, MOE_SB), half:D_MODEL] = pltpu.bitcast(w & jnp.uint32(0xFFFF0000), F32).astype(BF16)
            return carry

        lax.fori_loop(0, nsub, unpack, 0)

        @pl.when(c + 1 < nact)
        def _():
            start_gather(idxn_ref, meta_ref[META_NSUB, c + 1])

        def phase1(f, carry):
            slot = f & (MOE_RING - 1)
            for cp in gate_up_copies(expert, f, slot):
                cp.wait()

            nxt = f + MOE_AHEAD

            @pl.when(nxt < MOE_NF)
            def _():
                for cp in gate_up_copies(expert, nxt, nxt & (MOE_RING - 1)):
                    cp.start()

            @pl.when(nxt >= MOE_NF)
            def _():
                down_copy(expert, nxt - MOE_NF, nxt - MOE_NF).start()

            def gate_up(r0, rows, wgu):
                xs = x_bf[pl.ds(r0, rows), :]
                gu = jnp.dot(xs, wgu, preferred_element_type=F32)
                gt = gu[:, :MOE_TF]
                up = gu[:, MOE_TF:]
                h_scr[f, pl.ds(r0, rows), :] = (jax.nn.silu(gt) * up).astype(BF16)

            wgu = jnp.concatenate([wgu_f32[slot, 0].astype(BF16), wgu_f32[slot, 1].astype(BF16)], axis=1)
            wgu_bf[...] = wgu
            gate_up(0, MOE_SB, wgu)
            _rest_blocks(nsub, lambda r0, rows: gate_up(r0, rows, wgu_bf[...]))
            return carry

        lax.fori_loop(0, MOE_NF, phase1, 0)

        def phase2(n, carry):
            slot = n & 1
            wslot = n & (MOE_RING - 1)
            down_copy(expert, n, wslot).wait()

            nxt = n + MOE_AHEAD

            @pl.when(nxt < MOE_NN)
            def _():
                down_copy(expert, nxt, nxt & (MOE_RING - 1)).start()

            @pl.when(jnp.logical_and(nxt >= MOE_NN, c + 1 < nact))
            def _():
                for cp in gate_up_copies(meta_ref[META_E, c + 1], nxt - MOE_NN, nxt - MOE_NN):
                    cp.start()

            @pl.when(n >= 2)
            def _():
                wait_out(slot, nsub)

            def down(r0, rows, wd):
                hs = jnp.concatenate([h_scr[f, pl.ds(r0, rows), :] for f in range(MOE_NF)], axis=1)
                obuf[slot, pl.ds(r0, rows), :] = jnp.dot(hs, wd, preferred_element_type=F32)

            wd = wd_f32[wslot].astype(BF16)
            wd_bf[...] = wd
            down(0, MOE_SB, wd)
            _rest_blocks(nsub, lambda r0, rows: down(r0, rows, wd_bf[...]))

            def send(i, carry2):
                out_copy(slot, i, n).start()
                return carry2

            lax.fori_loop(0, nsub, send, 0)
            return carry

        lax.fori_loop(0, MOE_NN, phase2, 0)
        wait_out(0, nsub)
        wait_out(1, nsub)

    pl.when(active)(active_chunk)

    @pl.when(jnp.logical_not(active))
    def _():
        per_chunk = MOE_RC // MOE_SB
        first = (c - nact) * per_chunk
        n_here = jnp.clip(meta_ref[META_MISC, MISC_TAIL_PIECES] - first, 0, per_chunk)
        tail0 = meta_ref[META_MISC, MISC_TAIL_ROW0]
        obuf[0, 0:MOE_SB, :] = jnp.zeros((MOE_SB, MOE_TN), F32)

        def zcopy(k, col):
            r = pl.multiple_of(tail0 + (first + k) * MOE_SB, MOE_SB)
            return pltpu.make_async_copy(
                obuf.at[0, pl.ds(0, MOE_SB), :],
                y_hbm.at[pl.ds(r, MOE_SB), pl.ds(col * MOE_TN, MOE_TN)],
                osem.at[0])

        def send(k, carry):
            for col in range(MOE_NN):
                zcopy(k, col).start()
            return carry

        def wait(k, carry):
            for col in range(MOE_NN):
                zcopy(k, col).wait()
            return carry

        lax.fori_loop(0, n_here, send, 0)
        lax.fori_loop(0, n_here, wait, 0)


def _ffn(meta, idx_tab, hp, w_gate, w_up, w_down, n_rows):
    d = D_MODEL
    nc = idx_tab.shape[0]
    any_spec = lambda: pl.BlockSpec(memory_space=pl.ANY)
    grid_spec = pltpu.PrefetchScalarGridSpec(
        num_scalar_prefetch=1,
        grid=(nc,),
        in_specs=[
            pl.BlockSpec((None, 1, MOE_RC), lambda c, m: (c, 0, 0), memory_space=pltpu.SMEM),
            pl.BlockSpec((None, 1, MOE_RC), lambda c, m: (jnp.minimum(c + 1, nc - 1), 0, 0),
                         memory_space=pltpu.SMEM),
            any_spec(), any_spec(), any_spec(), any_spec(),
        ],
        out_specs=any_spec(),
        scratch_shapes=[
            pltpu.VMEM((MOE_RC, d // 2), jnp.uint32),
            pltpu.VMEM((MOE_RC, d), BF16),
            pltpu.VMEM((MOE_NF, MOE_RC, MOE_TF), BF16),
            pltpu.VMEM((MOE_RING, 2, d, MOE_TF), F32),
            pltpu.VMEM((MOE_RING, EXPERT_FF, MOE_TN), F32),
            pltpu.VMEM((d, 2 * MOE_TF), BF16),
            pltpu.VMEM((EXPERT_FF, MOE_TN), BF16),
            pltpu.VMEM((2, MOE_RC, MOE_TN), F32),
            pltpu.SemaphoreType.DMA(()),
            pltpu.SemaphoreType.DMA((2,)),
            pltpu.SemaphoreType.DMA((MOE_RING,)),
            pltpu.SemaphoreType.DMA((MOE_RING,)),
        ],
    )
    return pl.pallas_call(
        _ffn_kernel,
        grid_spec=grid_spec,
        out_shape=jax.ShapeDtypeStruct((n_rows, d), F32),
        compiler_params=_cparams(("arbitrary",)),
        name="moe_ffn",
    )(meta, idx_tab, idx_tab, hp, w_gate, w_up, w_down)


def _combine_kernel(d0_ref, d1_ref, d0n_ref, d1n_ref, x_ref, route_ref, g2_ref, fg_ref, y_hbm, o_ref,
                    buf, sem):
    tm = COMB_TM
    i = pl.program_id(0)
    slot = i & 1

    def start_rows(a_ref, b_ref, sl):
        def issue(r, c):
            pltpu.make_async_copy(y_hbm.at[pl.ds(a_ref[0, r], 1), :], buf.at[sl, 0, pl.ds(r, 1), :],
                                  sem.at[sl, 0]).start()
            pltpu.make_async_copy(y_hbm.at[pl.ds(b_ref[0, r], 1), :], buf.at[sl, 1, pl.ds(r, 1), :],
                                  sem.at[sl, 1]).start()
            return c

        lax.fori_loop(0, tm, issue, 0)

    @pl.when(i == 0)
    def _():
        start_rows(d0_ref, d1_ref, 0)

    @pl.when(i + 1 < pl.num_programs(0))
    def _():
        start_rows(d0n_ref, d1n_ref, 1 - slot)

    for k in range(TOP_K):
        pltpu.make_async_copy(y_hbm.at[pl.ds(0, tm), :], buf.at[slot, k], sem.at[slot, k]).wait()
    route = route_ref[...]
    y = buf[slot, 0] * route[:, 4:5] + buf[slot, 1] * route[:, 5:6]
    x2 = x_ref[...] + g2_ref[...] * y
    ms = jnp.mean(x2 * x2, axis=-1, keepdims=True)
    o_ref[...] = (x2 * lax.rsqrt(ms + RMS_EPS)) * fg_ref[...]


def _combine(dest0, dest1, x1, route, mod3, final_g, y_sorted):
    t, d = x1.shape
    nb = t // COMB_TM
    smem_idx = lambda: pl.BlockSpec((None, 1, COMB_TM), lambda i: (i, 0, 0), memory_space=pltpu.SMEM)
    smem_next = lambda: pl.BlockSpec((None, 1, COMB_TM), lambda i: (jnp.minimum(i + 1, nb - 1), 0, 0),
                                     memory_space=pltpu.SMEM)
    d0 = dest0.reshape(nb, 1, COMB_TM)
    d1 = dest1.reshape(nb, 1, COMB_TM)
    return pl.pallas_call(
        _combine_kernel,
        grid=(nb,),
        in_specs=[
            smem_idx(),
            smem_idx(),
            smem_next(),
            smem_next(),
            pl.BlockSpec((COMB_TM, d), lambda i: (i, 0)),
            pl.BlockSpec((COMB_TM, ROUTE_LANES), lambda i: (i, 0)),
            _mod_spec(5),
            pl.BlockSpec((1, d), lambda i: (0, 0)),
            pl.BlockSpec(memory_space=pl.ANY),
        ],
        out_specs=pl.BlockSpec((COMB_TM, d), lambda i: (i, 0)),
        out_shape=jax.ShapeDtypeStruct((t, d), F32),
        scratch_shapes=[
            pltpu.VMEM((2, TOP_K, COMB_TM, d), F32),
            pltpu.SemaphoreType.DMA((2, TOP_K)),
        ],
        compiler_params=_cparams(("arbitrary",)),
        name="moe_combine",
    )(d0, d1, d0, d1, x1, route, mod3, final_g.reshape(1, d), y_sorted)


def _dispatch_plan(route, cnt, n_tokens):
    n_assign = n_tokens * TOP_K
    n_rows = n_assign + N_EXPERTS * MOE_SB
    nc = n_assign // MOE_RC + N_EXPERTS
    i32 = jnp.int32
    eid = route[:, 0:2].astype(i32)
    rank = route[:, 2:4].astype(i32)
    counts = cnt[0, EXPERT_LANE0:EXPERT_LANE0 + N_EXPERTS].astype(i32)
    seg_rows = (counts + MOE_SB - 1) // MOE_SB * MOE_SB
    seg_end = jnp.cumsum(seg_rows)
    seg_start = seg_end - seg_rows
    e_hot = eid[:, :, None] == jnp.arange(N_EXPERTS, dtype=i32)
    lookup = lambda tab: jnp.sum(jnp.where(e_hot, tab, 0), axis=-1)
    dest = lookup(seg_start) + rank
    nchunk = (seg_rows + MOE_RC - 1) // MOE_RC
    ch_end = jnp.cumsum(nchunk)
    ch_start = ch_end - nchunk
    nact = ch_end[-1]
    cidx = jnp.arange(nc, dtype=i32)
    ch_e = jnp.minimum(jnp.searchsorted(ch_end, cidx, side="right"), N_EXPERTS - 1).astype(i32)
    k = cidx - ch_start[ch_e]
    ch_row0 = jnp.where(cidx < nact, seg_start[ch_e] + k * MOE_RC, 0)
    ch_nsub = jnp.where(cidx < nact, jnp.clip(seg_rows[ch_e] - k * MOE_RC, 0, MOE_RC) // MOE_SB, 0)
    misc = jnp.zeros((nc,), i32).at[MISC_NACT].set(nact)
    misc = misc.at[MISC_TAIL_ROW0].set(seg_end[-1]).at[MISC_TAIL_PIECES].set((n_rows - seg_end[-1]) // MOE_SB)
    meta = jnp.stack([ch_e, ch_row0, ch_nsub, misc]).astype(i32)
    slot = (lookup(ch_start) + rank // MOE_RC) * MOE_RC + rank % MOE_RC
    tok = jnp.broadcast_to(jnp.arange(n_tokens, dtype=i32)[:, None], (n_tokens, TOP_K))
    idx_tab = (jnp.arange(nc * MOE_RC, dtype=i32) % n_tokens).at[slot.reshape(-1)].set(tok.reshape(-1))
    return dest, meta, idx_tab.reshape(nc, 1, MOE_RC), n_rows


def kernel(x, c, ada_w, ada_b, norm1_g, w_in, lambda_q1, lambda_k1, lambda_q2, lambda_k2, subln_g,
           pool_w, pool_scale, w_out, norm2_g, router_group_w, router_group_b, router_expert_w,
           router_expert_b, expert_w_gate, expert_w_up, expert_w_down, final_norm_g):
    b_, s_, d = x.shape
    assert b_ == 1 and d == D_MODEL and ada_w.shape[0] == 1
    t = b_ * s_
    x2d = x.reshape(t, d)

    mod3 = _ada_mod(c, ada_w[0], ada_b[0]).reshape(6, 1, d)

    h = _norm1(x2d, norm1_g[0], mod3)
    proj = _inproj(h, w_in[0])
    slopes = 2.0 ** (-8.0 * jnp.arange(1, N_HEADS + 1, dtype=F32) / N_HEADS)
    o_attn = _attention(proj, slopes, lambda_q1, lambda_k1, lambda_q2, lambda_k2, subln_g)
    o_pool = _pool(proj, pool_w[0].astype(BF16), pool_scale[0])
    x1 = _outproj(o_attn, o_pool, w_out[0], x2d, mod3)

    rw = jnp.zeros((d, ROUTE_LANES), F32)
    rw = rw.at[:, :N_EXPERT_GROUPS].set(router_group_w[0])
    rw = rw.at[:, EXPERT_LANE0:EXPERT_LANE0 + N_EXPERTS].set(router_expert_w[0])
    rw_hi = rw.astype(BF16)
    rw_lo = (rw - rw_hi.astype(F32)).astype(BF16)
    rbias = jnp.full((1, ROUTE_LANES), NEG_BIG, F32)
    rbias = rbias.at[0, :N_EXPERT_GROUPS].set(router_group_b[0])
    rbias = rbias.at[0, EXPERT_LANE0:EXPERT_LANE0 + N_EXPERTS].set(router_expert_b[0].reshape(-1))
    h2, route, cnt = _norm2_route(x1, norm2_g[0], mod3, rw_hi, rw_lo, rbias)

    dest, meta, idx_tab, n_rows = _dispatch_plan(route, cnt, t)
    y_sorted = _ffn(meta, idx_tab, h2, expert_w_gate[0], expert_w_up[0], expert_w_down[0], n_rows)
    out = _combine(dest[:, 0], dest[:, 1], x1, route, mod3, final_norm_g, y_sorted)
    return out.reshape(b_, s_, d)
```

```python
import functools
import math

import jax
import jax.numpy as jnp
from jax import lax
from jax.experimental import pallas as pl
from jax.experimental.pallas import tpu as pltpu

F32 = jnp.float32
BF16 = jnp.bfloat16

D_MODEL = 4096
ATTN_WIDTH = 2048
POOL_WIDTH = 2048
HEAD_DIM = 128
V_DIM = 2 * HEAD_DIM
N_HEADS = ATTN_WIDTH // V_DIM
QK_WIDTH = N_HEADS * 2 * HEAD_DIM
POOL_WINDOWS = (2, 4, 8, 16)
N_POOL_GROUPS = len(POOL_WINDOWS)
POOL_GROUP_WIDTH = POOL_WIDTH // N_POOL_GROUPS
IN_PROJ_WIDTH = 2 * QK_WIDTH + ATTN_WIDTH + POOL_WIDTH
N_EXPERT_GROUPS = 4
EXPERTS_PER_GROUP = 8
N_EXPERTS = N_EXPERT_GROUPS * EXPERTS_PER_GROUP
TOP_K = 2
EXPERT_FF = 1536
RMS_EPS = 1e-6
LAMBDA_INIT = 0.8 - 0.6 * math.exp(-0.3 * 0)

LANES = 128
VMEM_LIMIT = 60 * 1024 * 1024
NEG_BIG = -1e30
LOG2E = math.log2(math.e)

ADA_TN = 512
NORM_TM = 256
MM_TM = 2048
MM_TN = 512
ATT_T = 512
ATT_HP = 2
POOL_TM = 512
POOL_HALO = 16
OUT_TM = 1024
OUT_TN = 512
ROUTE_LANES = LANES
EXPERT_LANE0 = N_EXPERT_GROUPS
MOE_RC = 1024
MOE_SB = 128
MOE_TF = 128
MOE_NF = EXPERT_FF // MOE_TF
MOE_TN = 256
MOE_NN = D_MODEL // MOE_TN
MOE_RING = 4
MOE_AHEAD = MOE_RING - 1
COMB_TM = 256


def _cparams(sem):
    return pltpu.CompilerParams(dimension_semantics=sem, vmem_limit_bytes=VMEM_LIMIT)


def _ada_kernel(c_ref, w_ref, b_ref, o_ref):
    d, tn = w_ref.shape
    ch = 256
    acc = jnp.zeros((8, tn), F32)
    for r in range(d // ch):
        cc = c_ref[r * ch:(r + 1) * ch, :]
        cc = cc * jax.nn.sigmoid(cc)
        w = w_ref[r * ch:(r + 1) * ch, :]
        acc = acc + (w * cc).reshape(ch // 8, 8, tn).sum(axis=0)
    o_ref[...] = acc.sum(axis=0, keepdims=True) + b_ref[...]


def _ada_mod(c, ada_w, ada_b):
    d, n = ada_w.shape
    return pl.pallas_call(
        _ada_kernel,
        grid=(n // ADA_TN,),
        in_specs=[
            pl.BlockSpec((d, 1), lambda j: (0, 0)),
            pl.BlockSpec((d, ADA_TN), lambda j: (0, j)),
            pl.BlockSpec((1, ADA_TN), lambda j: (0, j)),
        ],
        out_specs=pl.BlockSpec((1, ADA_TN), lambda j: (0, j)),
        out_shape=jax.ShapeDtypeStruct((1, n), F32),
        compiler_params=_cparams(("arbitrary",)),
        name="ada_mod",
    )(c.reshape(d, 1), ada_w, ada_b.reshape(1, n))


def _norm_mod(x, g, sc, sh):
    ms = jnp.mean(x * x, axis=-1, keepdims=True)
    return (x * lax.rsqrt(ms + RMS_EPS)) * g * (1.0 + sc) + sh


def _norm1_kernel(x_ref, g_ref, sc_ref, sh_ref, o_ref):
    o_ref[...] = _norm_mod(x_ref[...], g_ref[...], sc_ref[...], sh_ref[...]).astype(o_ref.dtype)


def _mod_spec(row):
    return pl.BlockSpec((None, 1, D_MODEL), lambda i, row=row: (row, 0, 0))


def _norm1(x2d, g, mod3):
    t, d = x2d.shape
    return pl.pallas_call(
        _norm1_kernel,
        grid=(t // NORM_TM,),
        in_specs=[
            pl.BlockSpec((NORM_TM, d), lambda i: (i, 0)),
            pl.BlockSpec((1, d), lambda i: (0, 0)),
            _mod_spec(1),
            _mod_spec(0),
        ],
        out_specs=pl.BlockSpec((NORM_TM, d), lambda i: (i, 0)),
        out_shape=jax.ShapeDtypeStruct((t, d), BF16),
        compiler_params=_cparams(("arbitrary",)),
        name="norm1_mod",
    )(x2d, g.reshape(1, d), mod3, mod3)


def _inproj_kernel(h_ref, w_ref, o_ref, w_bf, *, n_q_tiles, scale):
    j = pl.program_id(0)

    @pl.when(pl.program_id(1) == 0)
    def _():
        w_bf[...] = w_ref[...].astype(BF16)

    acc = jnp.dot(h_ref[...], w_bf[...], preferred_element_type=F32)
    s = jnp.where(j < n_q_tiles, scale, 1.0).astype(F32)
    o_ref[...] = (acc * s).astype(o_ref.dtype)


def _inproj(h, w):
    t, d = h.shape
    n = w.shape[1]
    kern = functools.partial(_inproj_kernel, n_q_tiles=QK_WIDTH // MM_TN, scale=LOG2E * HEAD_DIM ** -0.5)
    return pl.pallas_call(
        kern,
        grid=(n // MM_TN, t // MM_TM),
        in_specs=[
            pl.BlockSpec((MM_TM, d), lambda j, i: (i, 0)),
            pl.BlockSpec((d, MM_TN), lambda j, i: (0, j)),
        ],
        out_specs=pl.BlockSpec((MM_TM, MM_TN), lambda j, i: (i, j)),
        out_shape=jax.ShapeDtypeStruct((t, n), BF16),
        scratch_shapes=[pltpu.VMEM((d, MM_TN), BF16)],
        compiler_params=_cparams(("arbitrary", "arbitrary")),
        name="in_proj",
    )(h, w)


def _attn_kernel(slopes_ref, q_ref, k_ref, vt_ref, lq1_ref, lk1_ref, lq2_ref, lk2_ref, sg_ref,
                 o_ref, m_scr, l_scr, acc_scr, s_scr):
    g = pl.program_id(0)
    i = pl.program_id(1)
    t = ATT_T
    heads = range(ATT_HP)
    slope2 = [slopes_ref[g * ATT_HP + hh] * LOG2E for hh in heads]
    q0 = i * t

    m_scr[...] = jnp.full(m_scr.shape, NEG_BIG, F32)
    l_scr[...] = jnp.zeros(l_scr.shape, F32)
    acc_scr[...] = jnp.zeros(acc_scr.shape, F32)

    def colreduce(x, op):
        part = op(x.reshape(t // 8, 8, t), axis=0)
        return jnp.broadcast_to(op(part, axis=0, keepdims=True), (8, t))

    def scores(j, buf):
        k0 = pl.multiple_of(j * t, t)
        kblk = k_ref[pl.ds(k0, t), :]
        for hh in heads:
            for mp in range(2):
                c0 = hh * V_DIM + mp * HEAD_DIM
                s_scr[buf, hh, mp] = lax.dot_general(
                    kblk[:, c0:c0 + HEAD_DIM], q_ref[:, c0:c0 + HEAD_DIM], (((1,), (1,)), ((), ())),
                    preferred_element_type=F32)

    def update(j, buf, masked):
        dist = (lax.broadcasted_iota(jnp.int32, (t, LANES), 0) + (j * t - q0)).astype(F32)
        if masked:
            keep = (lax.broadcasted_iota(jnp.int32, (t, t), 1)
                    >= lax.broadcasted_iota(jnp.int32, (t, t), 0))
        for hh in heads:
            vtb = vt_ref[hh, j]
            bias = jnp.tile(slope2[hh] * dist, (1, t // LANES))
            for mp in range(2):
                s = s_scr[buf, hh, mp] + bias
                if masked:
                    s = jnp.where(keep, s, NEG_BIG)
                m_prev = m_scr[hh, mp]
                m_new = jnp.maximum(m_prev, colreduce(s, jnp.max))
                alpha = jnp.exp2(m_prev - m_new)
                p = jnp.exp2(s - jnp.tile(m_new, (t // 8, 1)))
                l_scr[hh, mp] = alpha * l_scr[hh, mp] + colreduce(p, jnp.sum)
                m_scr[hh, mp] = m_new
                acc_scr[hh, mp] = acc_scr[hh, mp] * jnp.tile(alpha, (V_DIM // 8, 1)) + jnp.dot(
                    vtb, p.astype(BF16), preferred_element_type=F32)

    scores(0, 0)

    def pair(p, c):
        j = 2 * p
        update(j, 0, False)
        scores(j + 1, 1)
        update(j + 1, 1, False)
        scores(j + 2, 0)
        return c

    lax.fori_loop(0, lax.shift_right_logical(i, 1), pair, 0)
    odd = (i & 1) == 1

    @pl.when(odd)
    def _():
        update(i - 1, 0, False)
        scores(i, 1)
        update(i, 1, True)

    @pl.when(jnp.logical_not(odd))
    def _():
        update(i, 0, True)

    lam = (jnp.exp(jnp.sum(lq1_ref[...] * lk1_ref[...], axis=1, keepdims=True))
           - jnp.exp(jnp.sum(lq2_ref[...] * lk2_ref[...], axis=1, keepdims=True))
           + LAMBDA_INIT)
    gain = jnp.tile(sg_ref[...], (1, t // LANES)) * (1.0 - LAMBDA_INIT)
    for hh in heads:
        o1 = acc_scr[hh, 0] / jnp.tile(l_scr[hh, 0], (V_DIM // 8, 1))
        o2 = acc_scr[hh, 1] / jnp.tile(l_scr[hh, 1], (V_DIM // 8, 1))
        o = o1 - lam * o2
        ms = jnp.mean(o * o, axis=0, keepdims=True)
        o = (o * lax.rsqrt(ms + RMS_EPS)) * gain
        o_ref[:, hh * V_DIM:(hh + 1) * V_DIM] = o.T.astype(o_ref.dtype)


def _attention(proj, slopes, lq1, lk1, lq2, lk2, subln_g):
    s_len = proj.shape[0]
    t = ATT_T
    nblk = s_len // t
    hp = ATT_HP
    kcol0 = QK_WIDTH // (hp * V_DIM)
    v = proj[:, 2 * QK_WIDTH:2 * QK_WIDTH + ATTN_WIDTH]
    vt = v.reshape(nblk, t, N_HEADS, V_DIM).transpose(2, 0, 3, 1)
    gain = jnp.broadcast_to(subln_g.reshape(V_DIM, 1), (V_DIM, LANES))
    vec = lambda: pl.BlockSpec((1, HEAD_DIM), lambda g, i, sl: (0, 0))
    grid_spec = pltpu.PrefetchScalarGridSpec(
        num_scalar_prefetch=1,
        grid=(N_HEADS // hp, nblk),
        in_specs=[
            pl.BlockSpec((t, hp * V_DIM), lambda g, i, sl: (i, g)),
            pl.BlockSpec((s_len, hp * V_DIM), lambda g, i, sl: (0, kcol0 + g)),
            pl.BlockSpec((hp, nblk, V_DIM, t), lambda g, i, sl: (g, 0, 0, 0)),
            vec(), vec(), vec(), vec(),
            pl.BlockSpec((V_DIM, LANES), lambda g, i, sl: (0, 0)),
        ],
        out_specs=pl.BlockSpec((t, hp * V_DIM), lambda g, i, sl: (i, g)),
        scratch_shapes=[
            pltpu.VMEM((hp, 2, 8, t), F32),
            pltpu.VMEM((hp, 2, 8, t), F32),
            pltpu.VMEM((hp, 2, V_DIM, t), F32),
            pltpu.VMEM((2, hp, 2, t, t), F32),
        ],
    )
    return pl.pallas_call(
        _attn_kernel,
        grid_spec=grid_spec,
        out_shape=jax.ShapeDtypeStruct((s_len, ATTN_WIDTH), BF16),
        compiler_params=_cparams(("arbitrary", "arbitrary")),
        name="diff_attn",
    )(slopes, proj, proj, vt, lq1, lk1, lq2, lk2, gain)


def _pool_kernel(u_ref, halo_ref, w_ref, sc_ref, o_ref, ext_scr):
    i = pl.program_id(0)
    tm = POOL_TM
    hl = POOL_HALO
    halo = halo_ref[...].astype(F32)
    ext_scr[0:hl, :] = jnp.where(i > 0, halo, 0.0)
    ext_scr[hl:hl + tm, :] = u_ref[...].astype(F32)
    pos = i * tm + lax.broadcasted_iota(jnp.int32, (tm, 1), 0)
    for g, win in enumerate(POOL_WINDOWS):
        c0, c1 = g * POOL_GROUP_WIDTH, (g + 1) * POOL_GROUP_WIDTH
        tok = ext_scr[hl:hl + tm, c0:c1]
        wsum = tok
        for dlt in range(1, win):
            wsum = wsum + ext_scr[hl - dlt:hl - dlt + tm, c0:c1]
        count = jnp.minimum(pos + 1, win).astype(F32)
        pooled = wsum / count - tok
        y = jnp.dot(pooled.astype(BF16), w_ref[g], preferred_element_type=F32)
        o_ref[:, c0:c1] = (y * sc_ref[:, c0:c1]).astype(o_ref.dtype)


def _pool(proj, pool_w_bf, pool_scale):
    s_len = proj.shape[0]
    ucol = (2 * QK_WIDTH + ATTN_WIDTH) // POOL_WIDTH
    rb = POOL_TM // POOL_HALO
    return pl.pallas_call(
        _pool_kernel,
        grid=(s_len // POOL_TM,),
        in_specs=[
            pl.BlockSpec((POOL_TM, POOL_WIDTH), lambda i: (i, ucol)),
            pl.BlockSpec((POOL_HALO, POOL_WIDTH), lambda i: (jnp.maximum(i * rb - 1, 0), ucol)),
            pl.BlockSpec((N_POOL_GROUPS, POOL_GROUP_WIDTH, POOL_GROUP_WIDTH), lambda i: (0, 0, 0)),
            pl.BlockSpec((1, POOL_WIDTH), lambda i: (0, 0)),
        ],
        out_specs=pl.BlockSpec((POOL_TM, POOL_WIDTH), lambda i: (i, 0)),
        out_shape=jax.ShapeDtypeStruct((s_len, POOL_WIDTH), BF16),
        scratch_shapes=[pltpu.VMEM((POOL_HALO + POOL_TM, POOL_WIDTH), F32)],
        compiler_params=_cparams(("arbitrary",)),
        name="pool_mixer",
    )(proj, proj, pool_w_bf, pool_scale.reshape(1, POOL_WIDTH))


def _outproj_kernel(a_ref, p_ref, wa_ref, wp_ref, x_ref, g_ref, o_ref, wa_bf, wp_bf):
    @pl.when(pl.program_id(1) == 0)
    def _():
        wa_bf[...] = wa_ref[...].astype(BF16)
        wp_bf[...] = wp_ref[...].astype(BF16)

    acc = jnp.dot(a_ref[...], wa_bf[...], preferred_element_type=F32)
    acc = acc + jnp.dot(p_ref[...], wp_bf[...], preferred_element_type=F32)
    o_ref[...] = x_ref[...] + g_ref[...] * acc


def _outproj(o_attn, o_pool, w_out, x2d, mod3):
    t, d = x2d.shape
    return pl.pallas_call(
        _outproj_kernel,
        grid=(d // OUT_TN, t // OUT_TM),
        in_specs=[
            pl.BlockSpec((OUT_TM, ATTN_WIDTH), lambda j, i: (i, 0)),
            pl.BlockSpec((OUT_TM, POOL_WIDTH), lambda j, i: (i, 0)),
            pl.BlockSpec((ATTN_WIDTH, OUT_TN), lambda j, i: (0, j)),
            pl.BlockSpec((POOL_WIDTH, OUT_TN), lambda j, i: (1, j)),
            pl.BlockSpec((OUT_TM, OUT_TN), lambda j, i: (i, j)),
            pl.BlockSpec((None, 1, OUT_TN), lambda j, i: (2, 0, j)),
        ],
        out_specs=pl.BlockSpec((OUT_TM, OUT_TN), lambda j, i: (i, j)),
        out_shape=jax.ShapeDtypeStruct((t, d), F32),
        scratch_shapes=[pltpu.VMEM((ATTN_WIDTH, OUT_TN), BF16), pltpu.VMEM((POOL_WIDTH, OUT_TN), BF16)],
        compiler_params=_cparams(("arbitrary", "arbitrary")),
        name="out_proj",
    )(o_attn, o_pool, w_out, w_out, x2d, mod3)


def _route_kernel(x_ref, g_ref, sc_ref, sh_ref, whi_ref, wlo_ref, rb_ref,
                  h_ref, route_ref, cnt_ref, base_scr):
    step = pl.program_id(0)
    tm = NORM_TM

    @pl.when(step == 0)
    def _():
        base_scr[...] = jnp.zeros(base_scr.shape, F32)

    h2 = _norm_mod(x_ref[...], g_ref[...], sc_ref[...], sh_ref[...])
    half = D_MODEL // 2
    lo = pltpu.bitcast(h2[:, :half].astype(BF16).astype(F32), jnp.uint32)
    hi = pltpu.bitcast(h2[:, half:].astype(BF16).astype(F32), jnp.uint32)
    h_ref[...] = hi | (lo >> 16)

    h_hi = h2.astype(BF16)
    h_lo = (h2 - h_hi.astype(F32)).astype(BF16)
    w_hi = whi_ref[...]
    lg = (jnp.dot(h_hi, w_hi, preferred_element_type=F32)
          + jnp.dot(h_lo, w_hi, preferred_element_type=F32)
          + jnp.dot(h_hi, wlo_ref[...], preferred_element_type=F32)) + rb_ref[...]

    lane = lax.broadcasted_iota(jnp.int32, (tm, ROUTE_LANES), 1)

    def first_max(vals):
        v = jnp.max(vals, axis=1, keepdims=True)
        idx = jnp.min(jnp.where(vals == v, lane, ROUTE_LANES), axis=1, keepdims=True)
        return v, idx

    gl = jnp.where(lane < N_EXPERT_GROUPS, lg, NEG_BIG)
    gmax, g_sel = first_max(gl)
    p_g = 1.0 / jnp.sum(jnp.exp(gl - gmax), axis=1, keepdims=True)

    e_lo = EXPERT_LANE0 + g_sel * EXPERTS_PER_GROUP
    el = jnp.where(jnp.logical_and(lane >= e_lo, lane < e_lo + EXPERTS_PER_GROUP), lg, NEG_BIG)
    v1, j1 = first_max(el)
    el2 = jnp.where(lane == j1, NEG_BIG, el)
    v2, j2 = first_max(el2)
    e2 = jnp.exp(v2 - v1)
    gate1 = p_g / (1.0 + e2)
    gate2 = p_g * e2 / (1.0 + e2)

    oh1 = (lane == j1).astype(BF16)
    oh2 = (lane == j2).astype(BF16)
    r_i = lax.broadcasted_iota(jnp.int32, (tm, tm), 0)
    c_i = lax.broadcasted_iota(jnp.int32, (tm, tm), 1)
    lower = (c_i < r_i).astype(BF16)
    before1 = jnp.dot(lower, oh1, preferred_element_type=F32)
    before2 = jnp.dot(lower, oh2, preferred_element_type=F32)
    oh1f = oh1.astype(F32)
    oh2f = oh2.astype(F32)
    tot1 = jnp.sum(oh1f, axis=0, keepdims=True)
    tot2 = jnp.sum(oh2f, axis=0, keepdims=True)
    base = base_scr[0:1, :]
    rank1 = jnp.sum((base + before1) * oh1f, axis=1, keepdims=True)
    rank2 = jnp.sum((base + tot1 + before2) * oh2f, axis=1, keepdims=True)
    new_base = base + tot1 + tot2
    base_scr[0:1, :] = new_base
    cnt_ref[...] = new_base

    eid1 = (j1 - EXPERT_LANE0).astype(F32)
    eid2 = (j2 - EXPERT_LANE0).astype(F32)
    packed = jnp.zeros((tm, ROUTE_LANES), F32)
    for k, val in enumerate((eid1, eid2, rank1, rank2, gate1, gate2)):
        packed = jnp.where(lane == k, val, packed)
    route_ref[...] = packed


def _norm2_route(x1, g, mod3, w_hi, w_lo, rbias):
    t, d = x1.shape
    const = lambda shape: pl.BlockSpec(shape, lambda i: (0,) * len(shape))
    return pl.pallas_call(
        _route_kernel,
        grid=(t // NORM_TM,),
        in_specs=[
            pl.BlockSpec((NORM_TM, d), lambda i: (i, 0)),
            const((1, d)),
            _mod_spec(4),
            _mod_spec(3),
            const((d, ROUTE_LANES)),
            const((d, ROUTE_LANES)),
            const((1, ROUTE_LANES)),
        ],
        out_specs=[
            pl.BlockSpec((NORM_TM, d // 2), lambda i: (i, 0)),
            pl.BlockSpec((NORM_TM, ROUTE_LANES), lambda i: (i, 0)),
            const((1, ROUTE_LANES)),
        ],
        out_shape=[
            jax.ShapeDtypeStruct((t, d // 2), jnp.uint32),
            jax.ShapeDtypeStruct((t, ROUTE_LANES), F32),
            jax.ShapeDtypeStruct((1, ROUTE_LANES), F32),
        ],
        scratch_shapes=[pltpu.VMEM((8, ROUTE_LANES), F32)],
        compiler_params=_cparams(("arbitrary",)),
        name="norm2_route",
    )(x1, g.reshape(1, d), mod3, mod3, w_hi, w_lo, rbias)


META_E, META_ROW0, META_NSUB, META_MISC = 0, 1, 2, 3
MISC_NACT, MISC_TAIL_ROW0, MISC_TAIL_PIECES = 0, 1, 2


def _rest_blocks(nsub, fn):
    rest = nsub - 1
    r0 = jnp.int32(MOE_SB)
    for k in (4, 2, 1):
        take = (rest & k) != 0

        @pl.when(take)
        def _(r0=r0, k=k):
            fn(pl.multiple_of(r0, MOE_SB), k * MOE_SB)

        r0 = r0 + jnp.where(take, k * MOE_SB, 0)


def _ffn_kernel(meta_ref, idx_ref, idxn_ref, hp_hbm, wg_hbm, wu_hbm, wd_hbm, y_hbm,
                xw, x_bf, h_scr, wgu_f32, wd_f32, wgu_bf, wd_bf, obuf, gsem, osem, wsem, dsem):
    c = pl.program_id(0)
    nact = meta_ref[META_MISC, MISC_NACT]
    active = c < nact
    nsub = meta_ref[META_NSUB, c]
    row0 = meta_ref[META_ROW0, c]
    expert = meta_ref[META_E, c]
    half = D_MODEL // 2
    sb_shift = int(math.log2(MOE_SB))

    def start_gather(ids_ref, n_sub):
        def issue(r, carry):
            pltpu.make_async_copy(hp_hbm.at[pl.ds(ids_ref[0, r], 1), :], xw.at[pl.ds(r, 1), :], gsem).start()
            return carry

        lax.fori_loop(0, lax.shift_left(n_sub, sb_shift), issue, 0)

    def wait_gather(n_sub):
        def wait(i, carry):
            pltpu.make_async_copy(xw.at[pl.ds(0, MOE_SB), :], xw.at[pl.ds(0, MOE_SB), :], gsem).wait()
            return carry

        lax.fori_loop(0, n_sub, wait, 0)

    def gate_up_copies(e, f, slot):
        col = pl.ds(pl.multiple_of(f * MOE_TF, MOE_TF), MOE_TF)
        return (pltpu.make_async_copy(wg_hbm.at[e, :, col], wgu_f32.at[slot, 0], wsem.at[slot]),
                pltpu.make_async_copy(wu_hbm.at[e, :, col], wgu_f32.at[slot, 1], wsem.at[slot]))

    def down_copy(e, n, slot):
        col = pl.ds(pl.multiple_of(n * MOE_TN, MOE_TN), MOE_TN)
        return pltpu.make_async_copy(wd_hbm.at[e, :, col], wd_f32.at[slot], dsem.at[slot])

    def out_copy(slot, i, n):
        r = pl.multiple_of(i * MOE_SB, MOE_SB)
        col = pl.ds(pl.multiple_of(n * MOE_TN, MOE_TN), MOE_TN)
        return pltpu.make_async_copy(
            obuf.at[slot, pl.ds(r, MOE_SB), :],
            y_hbm.at[pl.ds(pl.multiple_of(row0 + r, MOE_SB), MOE_SB), col],
            osem.at[slot])

    def wait_out(slot, n_sub):
        def wait(i, carry):
            out_copy(slot, 0, 0).wait()
            return carry

        lax.fori_loop(0, n_sub, wait, 0)

    def active_chunk():
        @pl.when(c == 0)
        def _():
            start_gather(idx_ref, nsub)
            for f in range(MOE_AHEAD):
                for cp in gate_up_copies(expert, f, f):
                    cp.start()

        wait_gather(nsub)

        def unpack(i, carry):
            r0 = pl.multiple_of(i * MOE_SB, MOE_SB)
            w = xw[pl.ds(r0, MOE_SB), :]
            x_bf[pl.ds(r0, MOE_SB), 0:half] = pltpu.bitcast(w << 16, F32).astype(BF16)
            x_bf[pl.ds(r0, MOE_SB), half:D_MODEL] = pltpu.bitcast(w & jnp.uint32(0xFFFF0000), F32).astype(BF16)
            return carry

        lax.fori_loop(0, nsub, unpack, 0)

        @pl.when(c + 1 < nact)
        def _():
            start_gather(idxn_ref, meta_ref[META_NSUB, c + 1])

        def phase1(f, carry):
            slot = f & (MOE_RING - 1)
            for cp in gate_up_copies(expert, f, slot):
                cp.wait()

            nxt = f + MOE_AHEAD

            @pl.when(nxt < MOE_NF)
            def _():
                for cp in gate_up_copies(expert, nxt, nxt & (MOE_RING - 1)):
                    cp.start()

            @pl.when(nxt >= MOE_NF)
            def _():
                down_copy(expert, nxt - MOE_NF, nxt - MOE_NF).start()

            def gate_up(r0, rows, wgu):
                xs = x_bf[pl.ds(r0, rows), :]
                gu = jnp.dot(xs, wgu, preferred_element_type=F32)
                gt = gu[:, :MOE_TF]
                up = gu[:, MOE_TF:]
                h_scr[f, pl.ds(r0, rows), :] = (jax.nn.silu(gt) * up).astype(BF16)

            wgu = jnp.concatenate([wgu_f32[slot, 0].astype(BF16), wgu_f32[slot, 1].astype(BF16)], axis=1)
            wgu_bf[...] = wgu
            gate_up(0, MOE_SB, wgu)
            _rest_blocks(nsub, lambda r0, rows: gate_up(r0, rows, wgu_bf[...]))
            return carry

        lax.fori_loop(0, MOE_NF, phase1, 0)

        def phase2(n, carry):
            slot = n & 1
            wslot = n & (MOE_RING - 1)
            down_copy(expert, n, wslot).wait()

            nxt = n + MOE_AHEAD

            @pl.when(nxt < MOE_NN)
            def _():
                down_copy(expert, nxt, nxt & (MOE_RING - 1)).start()

            @pl.when(jnp.logical_and(nxt >= MOE_NN, c + 1 < nact))
            def _():
                for cp in gate_up_copies(meta_ref[META_E, c + 1], nxt - MOE_NN, nxt - MOE_NN):
                    cp.start()

            @pl.when(n >= 2)
            def _():
                wait_out(slot, nsub)

            def down(r0, rows, wd):
                hs = jnp.concatenate([h_scr[f, pl.ds(r0, rows), :] for f in range(MOE_NF)], axis=1)
                obuf[slot, pl.ds(r0, rows), :] = jnp.dot(hs, wd, preferred_element_type=F32)

            wd = wd_f32[wslot].astype(BF16)
            wd_bf[...] = wd
            down(0, MOE_SB, wd)
            _rest_blocks(nsub, lambda r0, rows: down(r0, rows, wd_bf[...]))

            def send(i, carry2):
                out_copy(slot, i, n).start()
                return carry2

            lax.fori_loop(0, nsub, send, 0)
            return carry

        lax.fori_loop(0, MOE_NN, phase2, 0)
        wait_out(0, nsub)
        wait_out(1, nsub)

    pl.when(active)(active_chunk)

    @pl.when(jnp.logical_not(active))
    def _():
        per_chunk = MOE_RC // MOE_SB
        first = (c - nact) * per_chunk
        n_here = jnp.clip(meta_ref[META_MISC, MISC_TAIL_PIECES] - first, 0, per_chunk)
        tail0 = meta_ref[META_MISC, MISC_TAIL_ROW0]
        obuf[0, 0:MOE_SB, :] = jnp.zeros((MOE_SB, MOE_TN), F32)

        def zcopy(k, col):
            r = pl.multiple_of(tail0 + (first + k) * MOE_SB, MOE_SB)
            return pltpu.make_async_copy(
                obuf.at[0, pl.ds(0, MOE_SB), :],
                y_hbm.at[pl.ds(r, MOE_SB), pl.ds(col * MOE_TN, MOE_TN)],
                osem.at[0])

        def send(k, carry):
            for col in range(MOE_NN):
                zcopy(k, col).start()
            return carry

        def wait(k, carry):
            for col in range(MOE_NN):
                zcopy(k, col).wait()
            return carry

        lax.fori_loop(0, n_here, send, 0)
        lax.fori_loop(0, n_here, wait, 0)


def _ffn(meta, idx_tab, hp, w_gate, w_up, w_down, n_rows):
    d = D_MODEL
    nc = idx_tab.shape[0]
    any_spec = lambda: pl.BlockSpec(memory_space=pl.ANY)
    grid_spec = pltpu.PrefetchScalarGridSpec(
        num_scalar_prefetch=1,
        grid=(nc,),
        in_specs=[
            pl.BlockSpec((None, 1, MOE_RC), lambda c, m: (c, 0, 0), memory_space=pltpu.SMEM),
            pl.BlockSpec((None, 1, MOE_RC), lambda c, m: (jnp.minimum(c + 1, nc - 1), 0, 0),
                         memory_space=pltpu.SMEM),
            any_spec(), any_spec(), any_spec(), any_spec(),
        ],
        out_specs=any_spec(),
        scratch_shapes=[
            pltpu.VMEM((MOE_RC, d // 2), jnp.uint32),
            pltpu.VMEM((MOE_RC, d), BF16),
            pltpu.VMEM((MOE_NF, MOE_RC, MOE_TF), BF16),
            pltpu.VMEM((MOE_RING, 2, d, MOE_TF), F32),
            pltpu.VMEM((MOE_RING, EXPERT_FF, MOE_TN), F32),
            pltpu.VMEM((d, 2 * MOE_TF), BF16),
            pltpu.VMEM((EXPERT_FF, MOE_TN), BF16),
            pltpu.VMEM((2, MOE_RC, MOE_TN), F32),
            pltpu.SemaphoreType.DMA(()),
            pltpu.SemaphoreType.DMA((2,)),
            pltpu.SemaphoreType.DMA((MOE_RING,)),
            pltpu.SemaphoreType.DMA((MOE_RING,)),
        ],
    )
    return pl.pallas_call(
        _ffn_kernel,
        grid_spec=grid_spec,
        out_shape=jax.ShapeDtypeStruct((n_rows, d), F32),
        compiler_params=_cparams(("arbitrary",)),
        name="moe_ffn",
    )(meta, idx_tab, idx_tab, hp, w_gate, w_up, w_down)


def _combine_kernel(d0_ref, d1_ref, d0n_ref, d1n_ref, x_ref, route_ref, g2_ref, fg_ref, y_hbm, o_ref,
                    buf, sem):
    tm = COMB_TM
    i = pl.program_id(0)
    slot = i & 1

    def start_rows(a_ref, b_ref, sl):
        def issue(r, c):
            pltpu.make_async_copy(y_hbm.at[pl.ds(a_ref[0, r], 1), :], buf.at[sl, 0, pl.ds(r, 1), :],
                                  sem.at[sl, 0]).start()
            pltpu.make_async_copy(y_hbm.at[pl.ds(b_ref[0, r], 1), :], buf.at[sl, 1, pl.ds(r, 1), :],
                                  sem.at[sl, 1]).start()
            return c

        lax.fori_loop(0, tm, issue, 0)

    @pl.when(i == 0)
    def _():
        start_rows(d0_ref, d1_ref, 0)

    @pl.when(i + 1 < pl.num_programs(0))
    def _():
        start_rows(d0n_ref, d1n_ref, 1 - slot)

    for k in range(TOP_K):
        pltpu.make_async_copy(y_hbm.at[pl.ds(0, tm), :], buf.at[slot, k], sem.at[slot, k]).wait()
    route = route_ref[...]
    y = buf[slot, 0] * route[:, 4:5] + buf[slot, 1] * route[:, 5:6]
    x2 = x_ref[...] + g2_ref[...] * y
    ms = jnp.mean(x2 * x2, axis=-1, keepdims=True)
    o_ref[...] = (x2 * lax.rsqrt(ms + RMS_EPS)) * fg_ref[...]


def _combine(dest0, dest1, x1, route, mod3, final_g, y_sorted):
    t, d = x1.shape
    nb = t // COMB_TM
    smem_idx = lambda: pl.BlockSpec((None, 1, COMB_TM), lambda i: (i, 0, 0), memory_space=pltpu.SMEM)
    smem_next = lambda: pl.BlockSpec((None, 1, COMB_TM), lambda i: (jnp.minimum(i + 1, nb - 1), 0, 0),
                                     memory_space=pltpu.SMEM)
    d0 = dest0.reshape(nb, 1, COMB_TM)
    d1 = dest1.reshape(nb, 1, COMB_TM)
    return pl.pallas_call(
        _combine_kernel,
        grid=(nb,),
        in_specs=[
            smem_idx(),
            smem_idx(),
            smem_next(),
            smem_next(),
            pl.BlockSpec((COMB_TM, d), lambda i: (i, 0)),
            pl.BlockSpec((COMB_TM, ROUTE_LANES), lambda i: (i, 0)),
            _mod_spec(5),
            pl.BlockSpec((1, d), lambda i: (0, 0)),
            pl.BlockSpec(memory_space=pl.ANY),
        ],
        out_specs=pl.BlockSpec((COMB_TM, d), lambda i: (i, 0)),
        out_shape=jax.ShapeDtypeStruct((t, d), F32),
        scratch_shapes=[
            pltpu.VMEM((2, TOP_K, COMB_TM, d), F32),
            pltpu.SemaphoreType.DMA((2, TOP_K)),
        ],
        compiler_params=_cparams(("arbitrary",)),
        name="moe_combine",
    )(d0, d1, d0, d1, x1, route, mod3, final_g.reshape(1, d), y_sorted)


def _dispatch_plan(route, cnt, n_tokens):
    n_assign = n_tokens * TOP_K
    n_rows = n_assign + N_EXPERTS * MOE_SB
    nc = n_assign // MOE_RC + N_EXPERTS
    i32 = jnp.int32
    eid = route[:, 0:2].astype(i32)
    rank = route[:, 2:4].astype(i32)
    counts = cnt[0, EXPERT_LANE0:EXPERT_LANE0 + N_EXPERTS].astype(i32)
    seg_rows = (counts + MOE_SB - 1) // MOE_SB * MOE_SB
    seg_end = jnp.cumsum(seg_rows)
    seg_start = seg_end - seg_rows
    e_hot = eid[:, :, None] == jnp.arange(N_EXPERTS, dtype=i32)
    lookup = lambda tab: jnp.sum(jnp.where(e_hot, tab, 0), axis=-1)
    dest = lookup(seg_start) + rank
    nchunk = (seg_rows + MOE_RC - 1) // MOE_RC
    ch_end = jnp.cumsum(nchunk)
    ch_start = ch_end - nchunk
    nact = ch_end[-1]
    cidx = jnp.arange(nc, dtype=i32)
    ch_e = jnp.minimum(jnp.searchsorted(ch_end, cidx, side="right"), N_EXPERTS - 1).astype(i32)
    k = cidx - ch_start[ch_e]
    ch_row0 = jnp.where(cidx < nact, seg_start[ch_e] + k * MOE_RC, 0)
    ch_nsub = jnp.where(cidx < nact, jnp.clip(seg_rows[ch_e] - k * MOE_RC, 0, MOE_RC) // MOE_SB, 0)
    misc = jnp.zeros((nc,), i32).at[MISC_NACT].set(nact)
    misc = misc.at[MISC_TAIL_ROW0].set(seg_end[-1]).at[MISC_TAIL_PIECES].set((n_rows - seg_end[-1]) // MOE_SB)
    meta = jnp.stack([ch_e, ch_row0, ch_nsub, misc]).astype(i32)
    slot = (lookup(ch_start) + rank // MOE_RC) * MOE_RC + rank % MOE_RC
    tok = jnp.broadcast_to(jnp.arange(n_tokens, dtype=i32)[:, None], (n_tokens, TOP_K))
    idx_tab = (jnp.arange(nc * MOE_RC, dtype=i32) % n_tokens).at[slot.reshape(-1)].set(tok.reshape(-1))
    return dest, meta, idx_tab.reshape(nc, 1, MOE_RC), n_rows


def kernel(x, c, ada_w, ada_b, norm1_g, w_in, lambda_q1, lambda_k1, lambda_q2, lambda_k2, subln_g,
           pool_w, pool_scale, w_out, norm2_g, router_group_w, router_group_b, router_expert_w,
           router_expert_b, expert_w_gate, expert_w_up, expert_w_down, final_norm_g):
    b_, s_, d = x.shape
    assert b_ == 1 and d == D_MODEL and ada_w.shape[0] == 1
    t = b_ * s_
    x2d = x.reshape(t, d)

    mod3 = _ada_mod(c, ada_w[0], ada_b[0]).reshape(6, 1, d)

    h = _norm1(x2d, norm1_g[0], mod3)
    proj = _inproj(h, w_in[0])
    slopes = 2.0 ** (-8.0 * jnp.arange(1, N_HEADS + 1, dtype=F32) / N_HEADS)
    o_attn = _attention(proj, slopes, lambda_q1, lambda_k1, lambda_q2, lambda_k2, subln_g)
    o_pool = _pool(proj, pool_w[0].astype(BF16), pool_scale[0])
    x1 = _outproj(o_attn, o_pool, w_out[0], x2d, mod3)

    rw = jnp.zeros((d, ROUTE_LANES), F32)
    rw = rw.at[:, :N_EXPERT_GROUPS].set(router_group_w[0])
    rw = rw.at[:, EXPERT_LANE0:EXPERT_LANE0 + N_EXPERTS].set(router_expert_w[0])
    rw_hi = rw.astype(BF16)
    rw_lo = (rw - rw_hi.astype(F32)).astype(BF16)
    rbias = jnp.full((1, ROUTE_LANES), NEG_BIG, F32)
    rbias = rbias.at[0, :N_EXPERT_GROUPS].set(router_group_b[0])
    rbias = rbias.at[0, EXPERT_LANE0:EXPERT_LANE0 + N_EXPERTS].set(router_expert_b[0].reshape(-1))
    h2, route, cnt = _norm2_route(x1, norm2_g[0], mod3, rw_hi, rw_lo, rbias)

    dest, meta, idx_tab, n_rows = _dispatch_plan(route, cnt, t)
    y_sorted = _ffn(meta, idx_tab, h2, expert_w_gate[0], expert_w_up[0], expert_w_down[0], n_rows)
    out = _combine(dest[:, 0], dest[:, 1], x1, route, mod3, final_norm_g, y_sorted)
    return out.reshape(b_, s_, d)
```

```python
import functools
import math

import jax
import jax.numpy as jnp
from jax import lax
from jax.experimental import pallas as pl
from jax.experimental.pallas import tpu as pltpu

F32 = jnp.float32
BF16 = jnp.bfloat16

D_MODEL = 4096
ATTN_WIDTH = 2048
POOL_WIDTH = 2048
HEAD_DIM = 128
V_DIM = 2 * HEAD_DIM
N_HEADS = ATTN_WIDTH // V_DIM
QK_WIDTH = N_HEADS * 2 * HEAD_DIM
POOL_WINDOWS = (2, 4, 8, 16)
N_POOL_GROUPS = len(POOL_WINDOWS)
POOL_GROUP_WIDTH = POOL_WIDTH // N_POOL_GROUPS
IN_PROJ_WIDTH = 2 * QK_WIDTH + ATTN_WIDTH + POOL_WIDTH
N_EXPERT_GROUPS = 4
EXPERTS_PER_GROUP = 8
N_EXPERTS = N_EXPERT_GROUPS * EXPERTS_PER_GROUP
TOP_K = 2
EXPERT_FF = 1536
RMS_EPS = 1e-6
LAMBDA_INIT = 0.8 - 0.6 * math.exp(-0.3 * 0)

LANES = 128
VMEM_LIMIT = 60 * 1024 * 1024
NEG_BIG = -1e30
LOG2E = math.log2(math.e)

ADA_TN = 512
NORM_TM = 256
MM_TM = 2048
MM_TN = 512
ATT_T = 512
ATT_HP = 2
POOL_TM = 512
POOL_HALO = 16
OUT_TM = 1024
OUT_TN = 512
ROUTE_LANES = LANES
EXPERT_LANE0 = N_EXPERT_GROUPS
MOE_RC = 1024
MOE_SB = 128
MOE_TF = 128
MOE_NF = EXPERT_FF // MOE_TF
MOE_TN = 256
MOE_NN = D_MODEL // MOE_TN
MOE_RING = 4
MOE_AHEAD = MOE_RING - 1
COMB_TM = 256


def _cparams(sem):
    return pltpu.CompilerParams(dimension_semantics=sem, vmem_limit_bytes=VMEM_LIMIT)


def _ada_kernel(c_ref, w_ref, b_ref, o_ref):
    d, tn = w_ref.shape
    ch = 256
    acc = jnp.zeros((8, tn), F32)
    for r in range(d // ch):
        cc = c_ref[r * ch:(r + 1) * ch, :]
        cc = cc * jax.nn.sigmoid(cc)
        w = w_ref[r * ch:(r + 1) * ch, :]
        acc = acc + (w * cc).reshape(ch // 8, 8, tn).sum(axis=0)
    o_ref[...] = acc.sum(axis=0, keepdims=True) + b_ref[...]


def _ada_mod(c, ada_w, ada_b):
    d, n = ada_w.shape
    return pl.pallas_call(
        _ada_kernel,
        grid=(n // ADA_TN,),
        in_specs=[
            pl.BlockSpec((d, 1), lambda j: (0, 0)),
            pl.BlockSpec((d, ADA_TN), lambda j: (0, j)),
            pl.BlockSpec((1, ADA_TN), lambda j: (0, j)),
        ],
        out_specs=pl.BlockSpec((1, ADA_TN), lambda j: (0, j)),
        out_shape=jax.ShapeDtypeStruct((1, n), F32),
        compiler_params=_cparams(("arbitrary",)),
        name="ada_mod",
    )(c.reshape(d, 1), ada_w, ada_b.reshape(1, n))


def _norm_mod(x, g, sc, sh):
    ms = jnp.mean(x * x, axis=-1, keepdims=True)
    return (x * lax.rsqrt(ms + RMS_EPS)) * g * (1.0 + sc) + sh


def _norm1_kernel(x_ref, g_ref, sc_ref, sh_ref, o_ref):
    o_ref[...] = _norm_mod(x_ref[...], g_ref[...], sc_ref[...], sh_ref[...]).astype(o_ref.dtype)


def _mod_spec(row):
    return pl.BlockSpec((None, 1, D_MODEL), lambda i, row=row: (row, 0, 0))


def _norm1(x2d, g, mod3):
    t, d = x2d.shape
    return pl.pallas_call(
        _norm1_kernel,
        grid=(t // NORM_TM,),
        in_specs=[
            pl.BlockSpec((NORM_TM, d), lambda i: (i, 0)),
            pl.BlockSpec((1, d), lambda i: (0, 0)),
            _mod_spec(1),
            _mod_spec(0),
        ],
        out_specs=pl.BlockSpec((NORM_TM, d), lambda i: (i, 0)),
        out_shape=jax.ShapeDtypeStruct((t, d), BF16),
        compiler_params=_cparams(("arbitrary",)),
        name="norm1_mod",
    )(x2d, g.reshape(1, d), mod3, mod3)


def _inproj_kernel(h_ref, w_ref, o_ref, w_bf, *, n_q_tiles, scale):
    j = pl.program_id(0)

    @pl.when(pl.program_id(1) == 0)
    def _():
        w_bf[...] = w_ref[...].astype(BF16)

    acc = jnp.dot(h_ref[...], w_bf[...], preferred_element_type=F32)
    s = jnp.where(j < n_q_tiles, scale, 1.0).astype(F32)
    o_ref[...] = (acc * s).astype(o_ref.dtype)


def _inproj(h, w):
    t, d = h.shape
    n = w.shape[1]
    kern = functools.partial(_inproj_kernel, n_q_tiles=QK_WIDTH // MM_TN, scale=LOG2E * HEAD_DIM ** -0.5)
    return pl.pallas_call(
        kern,
        grid=(n // MM_TN, t // MM_TM),
        in_specs=[
            pl.BlockSpec((MM_TM, d), lambda j, i: (i, 0)),
            pl.BlockSpec((d, MM_TN), lambda j, i: (0, j)),
        ],
        out_specs=pl.BlockSpec((MM_TM, MM_TN), lambda j, i: (i, j)),
        out_shape=jax.ShapeDtypeStruct((t, n), BF16),
        scratch_shapes=[pltpu.VMEM((d, MM_TN), BF16)],
        compiler_params=_cparams(("arbitrary", "arbitrary")),
        name="in_proj",
    )(h, w)


def _attn_kernel(slopes_ref, q_ref, k_ref, vt_ref, lq1_ref, lk1_ref, lq2_ref, lk2_ref, sg_ref,
                 o_ref, m_scr, l_scr, acc_scr, s_scr, mb_scr, qaug_scr):
    g = pl.program_id(0)
    i = pl.program_id(1)
    t = ATT_T
    heads = range(ATT_HP)
    lane = lax.broadcasted_iota(jnp.int32, (t, LANES), 1)

    piece = lane % 3
    for hh in heads:
        sl = jnp.full((t, LANES), slopes_ref[g * ATT_HP + hh] * LOG2E, F32)
        c1 = sl.astype(BF16).astype(F32)
        r1 = sl - c1
        c2 = r1.astype(BF16).astype(F32)
        c3 = (r1 - c2).astype(BF16).astype(F32)
        qextra = jnp.where(lane < 6, jnp.where(piece == 0, c1, jnp.where(piece == 1, c2, c3)),
                           0.0).astype(BF16)
        for mp in range(2):
            c0 = hh * V_DIM + mp * HEAD_DIM
            qaug_scr[hh, mp, :, 0:HEAD_DIM] = q_ref[:, c0:c0 + HEAD_DIM]
            qaug_scr[hh, mp, :, HEAD_DIM:2 * HEAD_DIM] = qextra

    m_scr[...] = jnp.full(m_scr.shape, NEG_BIG, F32)
    l_scr[...] = jnp.zeros(l_scr.shape, F32)
    acc_scr[...] = jnp.zeros(acc_scr.shape, F32)

    def colreduce(x, op):
        part = op(x.reshape(t // 8, 8, t), axis=0)
        return jnp.broadcast_to(op(part, axis=0, keepdims=True), (8, t))

    def scores(j, buf):
        k0 = pl.multiple_of(j * t, t)
        kblk = k_ref[pl.ds(k0, t), :]
        pos = k0 + lax.broadcasted_iota(jnp.int32, (t, LANES), 0)
        kextra = jnp.where(lane < 3, (pos & ~255).astype(F32),
                           jnp.where(lane < 6, (pos & 255).astype(F32), 0.0)).astype(BF16)
        for hh in heads:
            for mp in range(2):
                c0 = hh * V_DIM + mp * HEAD_DIM
                kaug = jnp.concatenate([kblk[:, c0:c0 + HEAD_DIM], kextra], axis=1)
                s = lax.dot_general(kaug, qaug_scr[hh, mp], (((1,), (1,)), ((), ())),
                                    preferred_element_type=F32)
                s_scr[buf, hh, mp] = s
                mb_scr[buf, hh, mp] = colreduce(s, jnp.max)

    def update(j, buf, masked):
        if masked:
            keep = (lax.broadcasted_iota(jnp.int32, (t, t), 1)
                    >= lax.broadcasted_iota(jnp.int32, (t, t), 0))
        for hh in heads:
            vtb = vt_ref[hh, j]
            for mp in range(2):
                s = s_scr[buf, hh, mp]
                if masked:
                    s = jnp.where(keep, s, NEG_BIG)
                    m_blk = colreduce(s, jnp.max)
                else:
                    m_blk = mb_scr[buf, hh, mp]
                m_prev = m_scr[hh, mp]
                m_new = jnp.maximum(m_prev, m_blk)
                alpha = jnp.exp2(m_prev - m_new)
                p = jnp.exp2(s - jnp.tile(m_new, (t // 8, 1)))
                l_scr[hh, mp] = alpha * l_scr[hh, mp] + colreduce(p, jnp.sum)
                m_scr[hh, mp] = m_new
                acc_scr[hh, mp] = acc_scr[hh, mp] * jnp.tile(alpha, (V_DIM // 8, 1)) + jnp.dot(
                    vtb, p.astype(BF16), preferred_element_type=F32)

    scores(0, 0)

    def pair(p, c):
        j = 2 * p
        update(j, 0, False)
        scores(j + 1, 1)
        update(j + 1, 1, False)
        scores(j + 2, 0)
        return c

    lax.fori_loop(0, lax.shift_right_logical(i, 1), pair, 0)
    odd = (i & 1) == 1

    @pl.when(odd)
    def _():
        update(i - 1, 0, False)
        scores(i, 1)
        update(i, 1, True)

    @pl.when(jnp.logical_not(odd))
    def _():
        update(i, 0, True)

    lam = (jnp.exp(jnp.sum(lq1_ref[...] * lk1_ref[...], axis=1, keepdims=True))
           - jnp.exp(jnp.sum(lq2_ref[...] * lk2_ref[...], axis=1, keepdims=True))
           + LAMBDA_INIT)
    gain = jnp.tile(sg_ref[...], (1, t // LANES)) * (1.0 - LAMBDA_INIT)
    for hh in heads:
        o1 = acc_scr[hh, 0] / jnp.tile(l_scr[hh, 0], (V_DIM // 8, 1))
        o2 = acc_scr[hh, 1] / jnp.tile(l_scr[hh, 1], (V_DIM // 8, 1))
        o = o1 - lam * o2
        ms = jnp.mean(o * o, axis=0, keepdims=True)
        o = (o * lax.rsqrt(ms + RMS_EPS)) * gain
        o_ref[:, hh * V_DIM:(hh + 1) * V_DIM] = o.T.astype(o_ref.dtype)


def _attention(proj, slopes, lq1, lk1, lq2, lk2, subln_g):
    s_len = proj.shape[0]
    t = ATT_T
    nblk = s_len // t
    hp = ATT_HP
    kcol0 = QK_WIDTH // (hp * V_DIM)
    v = proj[:, 2 * QK_WIDTH:2 * QK_WIDTH + ATTN_WIDTH]
    vt = v.reshape(nblk, t, N_HEADS, V_DIM).transpose(2, 0, 3, 1)
    gain = jnp.broadcast_to(subln_g.reshape(V_DIM, 1), (V_DIM, LANES))
    vec = lambda: pl.BlockSpec((1, HEAD_DIM), lambda g, i, sl: (0, 0))
    grid_spec = pltpu.PrefetchScalarGridSpec(
        num_scalar_prefetch=1,
        grid=(N_HEADS // hp, nblk),
        in_specs=[
            pl.BlockSpec((t, hp * V_DIM), lambda g, i, sl: (i, g)),
            pl.BlockSpec((s_len, hp * V_DIM), lambda g, i, sl: (0, kcol0 + g)),
            pl.BlockSpec((hp, nblk, V_DIM, t), lambda g, i, sl: (g, 0, 0, 0)),
            vec(), vec(), vec(), vec(),
            pl.BlockSpec((V_DIM, LANES), lambda g, i, sl: (0, 0)),
        ],
        out_specs=pl.BlockSpec((t, hp * V_DIM), lambda g, i, sl: (i, g)),
        scratch_shapes=[
            pltpu.VMEM((hp, 2, 8, t), F32),
            pltpu.VMEM((hp, 2, 8, t), F32),
            pltpu.VMEM((hp, 2, V_DIM, t), F32),
            pltpu.VMEM((2, hp, 2, t, t), F32),
            pltpu.VMEM((2, hp, 2, 8, t), F32),
            pltpu.VMEM((hp, 2, t, V_DIM), BF16),
        ],
    )
    return pl.pallas_call(
        _attn_kernel,
        grid_spec=grid_spec,
        out_shape=jax.ShapeDtypeStruct((s_len, ATTN_WIDTH), BF16),
        compiler_params=_cparams(("arbitrary", "arbitrary")),
        name="diff_attn",
    )(slopes, proj, proj, vt, lq1, lk1, lq2, lk2, gain)


def _pool_kernel(u_ref, halo_ref, w_ref, sc_ref, o_ref, ext_scr):
    i = pl.program_id(0)
    tm = POOL_TM
    hl = POOL_HALO
    halo = halo_ref[...].astype(F32)
    ext_scr[0:hl, :] = jnp.where(i > 0, halo, 0.0)
    ext_scr[hl:hl + tm, :] = u_ref[...].astype(F32)
    pos = i * tm + lax.broadcasted_iota(jnp.int32, (tm, 1), 0)
    for g, win in enumerate(POOL_WINDOWS):
        c0, c1 = g * POOL_GROUP_WIDTH, (g + 1) * POOL_GROUP_WIDTH
        tok = ext_scr[hl:hl + tm, c0:c1]
        wsum = tok
        for dlt in range(1, win):
            wsum = wsum + ext_scr[hl - dlt:hl - dlt + tm, c0:c1]
        count = jnp.minimum(pos + 1, win).astype(F32)
        pooled = wsum / count - tok
        y = jnp.dot(pooled.astype(BF16), w_ref[g], preferred_element_type=F32)
        o_ref[:, c0:c1] = (y * sc_ref[:, c0:c1]).astype(o_ref.dtype)


def _pool(proj, pool_w_bf, pool_scale):
    s_len = proj.shape[0]
    ucol = (2 * QK_WIDTH + ATTN_WIDTH) // POOL_WIDTH
    rb = POOL_TM // POOL_HALO
    return pl.pallas_call(
        _pool_kernel,
        grid=(s_len // POOL_TM,),
        in_specs=[
            pl.BlockSpec((POOL_TM, POOL_WIDTH), lambda i: (i, ucol)),
            pl.BlockSpec((POOL_HALO, POOL_WIDTH), lambda i: (jnp.maximum(i * rb - 1, 0), ucol)),
            pl.BlockSpec((N_POOL_GROUPS, POOL_GROUP_WIDTH, POOL_GROUP_WIDTH), lambda i: (0, 0, 0)),
            pl.BlockSpec((1, POOL_WIDTH), lambda i: (0, 0)),
        ],
        out_specs=pl.BlockSpec((POOL_TM, POOL_WIDTH), lambda i: (i, 0)),
        out_shape=jax.ShapeDtypeStruct((s_len, POOL_WIDTH), BF16),
        scratch_shapes=[pltpu.VMEM((POOL_HALO + POOL_TM, POOL_WIDTH), F32)],
        compiler_params=_cparams(("arbitrary",)),
        name="pool_mixer",
    )(proj, proj, pool_w_bf, pool_scale.reshape(1, POOL_WIDTH))


def _outproj_kernel(a_ref, p_ref, wa_ref, wp_ref, x_ref, g_ref, o_ref, wa_bf, wp_bf):
    @pl.when(pl.program_id(1) == 0)
    def _():
        wa_bf[...] = wa_ref[...].astype(BF16)
        wp_bf[...] = wp_ref[...].astype(BF16)

    acc = jnp.dot(a_ref[...], wa_bf[...], preferred_element_type=F32)
    acc = acc + jnp.dot(p_ref[...], wp_bf[...], preferred_element_type=F32)
    o_ref[...] = x_ref[...] + g_ref[...] * acc


def _outproj(o_attn, o_pool, w_out, x2d, mod3):
    t, d = x2d.shape
    return pl.pallas_call(
        _outproj_kernel,
        grid=(d // OUT_TN, t // OUT_TM),
        in_specs=[
            pl.BlockSpec((OUT_TM, ATTN_WIDTH), lambda j, i: (i, 0)),
            pl.BlockSpec((OUT_TM, POOL_WIDTH), lambda j, i: (i, 0)),
            pl.BlockSpec((ATTN_WIDTH, OUT_TN), lambda j, i: (0, j)),
            pl.BlockSpec((POOL_WIDTH, OUT_TN), lambda j, i: (1, j)),
            pl.BlockSpec((OUT_TM, OUT_TN), lambda j, i: (i, j)),
            pl.BlockSpec((None, 1, OUT_TN), lambda j, i: (2, 0, j)),
        ],
        out_specs=pl.BlockSpec((OUT_TM, OUT_TN), lambda j, i: (i, j)),
        out_shape=jax.ShapeDtypeStruct((t, d), F32),
        scratch_shapes=[pltpu.VMEM((ATTN_WIDTH, OUT_TN), BF16), pltpu.VMEM((POOL_WIDTH, OUT_TN), BF16)],
        compiler_params=_cparams(("arbitrary", "arbitrary")),
        name="out_proj",
    )(o_attn, o_pool, w_out, w_out, x2d, mod3)


def _route_kernel(x_ref, g_ref, sc_ref, sh_ref, whi_ref, wlo_ref, rb_ref,
                  h_ref, route_ref, cnt_ref, base_scr):
    step = pl.program_id(0)
    tm = NORM_TM

    @pl.when(step == 0)
    def _():
        base_scr[...] = jnp.zeros(base_scr.shape, F32)

    h2 = _norm_mod(x_ref[...], g_ref[...], sc_ref[...], sh_ref[...])
    half = D_MODEL // 2
    lo = pltpu.bitcast(h2[:, :half].astype(BF16).astype(F32), jnp.uint32)
    hi = pltpu.bitcast(h2[:, half:].astype(BF16).astype(F32), jnp.uint32)
    h_ref[...] = hi | (lo >> 16)

    h_hi = h2.astype(BF16)
    h_lo = (h2 - h_hi.astype(F32)).astype(BF16)
    w_hi = whi_ref[...]
    lg = (jnp.dot(h_hi, w_hi, preferred_element_type=F32)
          + jnp.dot(h_lo, w_hi, preferred_element_type=F32)
          + jnp.dot(h_hi, wlo_ref[...], preferred_element_type=F32)) + rb_ref[...]

    lane = lax.broadcasted_iota(jnp.int32, (tm, ROUTE_LANES), 1)

    def first_max(vals):
        v = jnp.max(vals, axis=1, keepdims=True)
        idx = jnp.min(jnp.where(vals == v, lane, ROUTE_LANES), axis=1, keepdims=True)
        return v, idx

    gl = jnp.where(lane < N_EXPERT_GROUPS, lg, NEG_BIG)
    gmax, g_sel = first_max(gl)
    p_g = 1.0 / jnp.sum(jnp.exp(gl - gmax), axis=1, keepdims=True)

    e_lo = EXPERT_LANE0 + g_sel * EXPERTS_PER_GROUP
    el = jnp.where(jnp.logical_and(lane >= e_lo, lane < e_lo + EXPERTS_PER_GROUP), lg, NEG_BIG)
    v1, j1 = first_max(el)
    el2 = jnp.where(lane == j1, NEG_BIG, el)
    v2, j2 = first_max(el2)
    e2 = jnp.exp(v2 - v1)
    gate1 = p_g / (1.0 + e2)
    gate2 = p_g * e2 / (1.0 + e2)

    oh1 = (lane == j1).astype(BF16)
    oh2 = (lane == j2).astype(BF16)
    r_i = lax.broadcasted_iota(jnp.int32, (tm, tm), 0)
    c_i = lax.broadcasted_iota(jnp.int32, (tm, tm), 1)
    lower = (c_i < r_i).astype(BF16)
    before1 = jnp.dot(lower, oh1, preferred_element_type=F32)
    before2 = jnp.dot(lower, oh2, preferred_element_type=F32)
    oh1f = oh1.astype(F32)
    oh2f = oh2.astype(F32)
    tot1 = jnp.sum(oh1f, axis=0, keepdims=True)
    tot2 = jnp.sum(oh2f, axis=0, keepdims=True)
    base = base_scr[0:1, :]
    rank1 = jnp.sum((base + before1) * oh1f, axis=1, keepdims=True)
    rank2 = jnp.sum((base + tot1 + before2) * oh2f, axis=1, keepdims=True)
    new_base = base + tot1 + tot2
    base_scr[0:1, :] = new_base
    cnt_ref[...] = new_base

    eid1 = (j1 - EXPERT_LANE0).astype(F32)
    eid2 = (j2 - EXPERT_LANE0).astype(F32)
    packed = jnp.zeros((tm, ROUTE_LANES), F32)
    for k, val in enumerate((eid1, eid2, rank1, rank2, gate1, gate2)):
        packed = jnp.where(lane == k, val, packed)
    route_ref[...] = packed


def _norm2_route(x1, g, mod3, w_hi, w_lo, rbias):
    t, d = x1.shape
    const = lambda shape: pl.BlockSpec(shape, lambda i: (0,) * len(shape))
    return pl.pallas_call(
        _route_kernel,
        grid=(t // NORM_TM,),
        in_specs=[
            pl.BlockSpec((NORM_TM, d), lambda i: (i, 0)),
            const((1, d)),
            _mod_spec(4),
            _mod_spec(3),
            const((d, ROUTE_LANES)),
            const((d, ROUTE_LANES)),
            const((1, ROUTE_LANES)),
        ],
        out_specs=[
            pl.BlockSpec((NORM_TM, d // 2), lambda i: (i, 0)),
            pl.BlockSpec((NORM_TM, ROUTE_LANES), lambda i: (i, 0)),
            const((1, ROUTE_LANES)),
        ],
        out_shape=[
            jax.ShapeDtypeStruct((t, d // 2), jnp.uint32),
            jax.ShapeDtypeStruct((t, ROUTE_LANES), F32),
            jax.ShapeDtypeStruct((1, ROUTE_LANES), F32),
        ],
        scratch_shapes=[pltpu.VMEM((8, ROUTE_LANES), F32)],
        compiler_params=_cparams(("arbitrary",)),
        name="norm2_route",
    )(x1, g.reshape(1, d), mod3, mod3, w_hi, w_lo, rbias)


META_E, META_ROW0, META_NSUB, META_MISC = 0, 1, 2, 3
MISC_NACT, MISC_TAIL_ROW0, MISC_TAIL_PIECES = 0, 1, 2


def _rest_blocks(nsub, fn):
    rest = nsub - 1
    r0 = jnp.int32(MOE_SB)
    top = (MOE_RC // MOE_SB) // 2
    for k in [top >> b for b in range(top.bit_length())]:
        take = (rest & k) != 0

        @pl.when(take)
        def _(r0=r0, k=k):
            fn(pl.multiple_of(r0, MOE_SB), k * MOE_SB)

        r0 = r0 + jnp.where(take, k * MOE_SB, 0)


def _ffn_kernel(meta_ref, idx_ref, idxn_ref, hp_hbm, wg_hbm, wu_hbm, wd_hbm, y_hbm,
                xw, x_bf, h_scr, wgu_f32, wd_f32, wgu_bf, wd_bf, obuf, gsem, osem, wsem, dsem):
    c = pl.program_id(0)
    nact = meta_ref[META_MISC, MISC_NACT]
    active = c < nact
    nsub = meta_ref[META_NSUB, c]
    row0 = meta_ref[META_ROW0, c]
    expert = meta_ref[META_E, c]
    half = D_MODEL // 2
    sb_shift = int(math.log2(MOE_SB))

    def start_gather(ids_ref, n_sub):
        def issue(r, carry):
            pltpu.make_async_copy(hp_hbm.at[pl.ds(ids_ref[0, r], 1), :], xw.at[pl.ds(r, 1), :], gsem).start()
            return carry

        lax.fori_loop(0, lax.shift_left(n_sub, sb_shift), issue, 0)

    def wait_gather(n_sub):
        def wait(i, carry):
            pltpu.make_async_copy(xw.at[pl.ds(0, MOE_SB), :], xw.at[pl.ds(0, MOE_SB), :], gsem).wait()
            return carry

        lax.fori_loop(0, n_sub, wait, 0)

    def gate_up_copies(e, f, slot):
        col = pl.ds(pl.multiple_of(f * MOE_TF, MOE_TF), MOE_TF)
        return (pltpu.make_async_copy(wg_hbm.at[e, :, col], wgu_f32.at[slot, 0], wsem.at[slot]),
                pltpu.make_async_copy(wu_hbm.at[e, :, col], wgu_f32.at[slot, 1], wsem.at[slot]))

    def down_copy(e, n, slot):
        col = pl.ds(pl.multiple_of(n * MOE_TN, MOE_TN), MOE_TN)
        return pltpu.make_async_copy(wd_hbm.at[e, :, col], wd_f32.at[slot], dsem.at[slot])

    def out_copy(slot, i, n):
        r = pl.multiple_of(i * MOE_SB, MOE_SB)
        col = pl.ds(pl.multiple_of(n * MOE_TN, MOE_TN), MOE_TN)
        return pltpu.make_async_copy(
            obuf.at[slot, pl.ds(r, MOE_SB), :],
            y_hbm.at[pl.ds(pl.multiple_of(row0 + r, MOE_SB), MOE_SB), col],
            osem.at[slot])

    def wait_out(slot, n_sub):
        def wait(i, carry):
            out_copy(slot, 0, 0).wait()
            return carry

        lax.fori_loop(0, n_sub, wait, 0)

    def active_chunk():
        @pl.when(c == 0)
        def _():
            start_gather(idx_ref, nsub)
            for f in range(MOE_AHEAD):
                for cp in gate_up_copies(expert, f, f):
                    cp.start()

        wait_gather(nsub)

        def unpack(i, carry):
            r0 = pl.multiple_of(i * MOE_SB, MOE_SB)
            w = xw[pl.ds(r0, MOE_SB), :]
            x_bf[pl.ds(r0, MOE_SB), 0:half] = pltpu.bitcast(w << 16, F32).astype(BF16)
            x_bf[pl.ds(r0, MOE_SB), half:D_MODEL] = pltpu.bitcast(w & jnp.uint32(0xFFFF0000), F32).astype(BF16)
            return carry

        lax.fori_loop(0, nsub, unpack, 0)

        @pl.when(c + 1 < nact)
        def _():
            start_gather(idxn_ref, meta_ref[META_NSUB, c + 1])

        def phase1(f, carry):
            slot = f & (MOE_RING - 1)
            for cp in gate_up_copies(expert, f, slot):
                cp.wait()

            nxt = f + MOE_AHEAD

            @pl.when(nxt < MOE_NF)
            def _():
                for cp in gate_up_copies(expert, nxt, nxt & (MOE_RING - 1)):
                    cp.start()

            @pl.when(nxt >= MOE_NF)
            def _():
                down_copy(expert, nxt - MOE_NF, nxt - MOE_NF).start()

            def gate_up(r0, rows, wgu):
                xs = x_bf[pl.ds(r0, rows), :]
                gu = jnp.dot(xs, wgu, preferred_element_type=F32)
                gt = gu[:, :MOE_TF]
                up = gu[:, MOE_TF:]
                h_scr[f, pl.ds(r0, rows), :] = (jax.nn.silu(gt) * up).astype(BF16)

            wgu = jnp.concatenate([wgu_f32[slot, 0].astype(BF16), wgu_f32[slot, 1].astype(BF16)], axis=1)
            wgu_bf[...] = wgu
            gate_up(0, MOE_SB, wgu)
            _rest_blocks(nsub, lambda r0, rows: gate_up(r0, rows, wgu_bf[...]))
            return carry

        lax.fori_loop(0, MOE_NF, phase1, 0)

        def phase2(n, carry):
            slot = n & 1
            wslot = n & (MOE_RING - 1)
            down_copy(expert, n, wslot).wait()

            nxt = n + MOE_AHEAD

            @pl.when(nxt < MOE_NN)
            def _():
                down_copy(expert, nxt, nxt & (MOE_RING - 1)).start()

            @pl.when(jnp.logical_and(nxt >= MOE_NN, c + 1 < nact))
            def _():
                for cp in gate_up_copies(meta_ref[META_E, c + 1], nxt - MOE_NN, nxt - MOE_NN):
                    cp.start()

            @pl.when(n >= 2)
            def _():
                wait_out(slot, nsub)

            def down(r0, rows, wd):
                hs = jnp.concatenate([h_scr[f, pl.ds(r0, rows), :] for f in range(MOE_NF)], axis=1)
                obuf[slot, pl.ds(r0, rows), :] = jnp.dot(hs, wd, preferred_element_type=F32)

            wd = wd_f32[wslot].astype(BF16)
            wd_bf[...] = wd
            down(0, MOE_SB, wd)
            _rest_blocks(nsub, lambda r0, rows: down(r0, rows, wd_bf[...]))

            def send(i, carry2):
                out_copy(slot, i, n).start()
                return carry2

            lax.fori_loop(0, nsub, send, 0)
            return carry

        lax.fori_loop(0, MOE_NN, phase2, 0)
        wait_out(0, nsub)
        wait_out(1, nsub)

    pl.when(active)(active_chunk)

    @pl.when(jnp.logical_not(active))
    def _():
        per_chunk = MOE_RC // MOE_SB
        first = (c - nact) * per_chunk
        n_here = jnp.clip(meta_ref[META_MISC, MISC_TAIL_PIECES] - first, 0, per_chunk)
        tail0 = meta_ref[META_MISC, MISC_TAIL_ROW0]
        obuf[0, 0:MOE_SB, :] = jnp.zeros((MOE_SB, MOE_TN), F32)

        def zcopy(k, col):
            r = pl.multiple_of(tail0 + (first + k) * MOE_SB, MOE_SB)
            return pltpu.make_async_copy(
                obuf.at[0, pl.ds(0, MOE_SB), :],
                y_hbm.at[pl.ds(r, MOE_SB), pl.ds(col * MOE_TN, MOE_TN)],
                osem.at[0])

        def send(k, carry):
            for col in range(MOE_NN):
                zcopy(k, col).start()
            return carry

        def wait(k, carry):
            for col in range(MOE_NN):
                zcopy(k, col).wait()
            return carry

        lax.fori_loop(0, n_here, send, 0)
        lax.fori_loop(0, n_here, wait, 0)


def _ffn(meta, idx_tab, hp, w_gate, w_up, w_down, n_rows):
    d = D_MODEL
    nc = idx_tab.shape[0]
    any_spec = lambda: pl.BlockSpec(memory_space=pl.ANY)
    grid_spec = pltpu.PrefetchScalarGridSpec(
        num_scalar_prefetch=1,
        grid=(nc,),
        in_specs=[
            pl.BlockSpec((None, 1, MOE_RC), lambda c, m: (c, 0, 0), memory_space=pltpu.SMEM),
            pl.BlockSpec((None, 1, MOE_RC), lambda c, m: (jnp.minimum(c + 1, nc - 1), 0, 0),
                         memory_space=pltpu.SMEM),
            any_spec(), any_spec(), any_spec(), any_spec(),
        ],
        out_specs=any_spec(),
        scratch_shapes=[
            pltpu.VMEM((MOE_RC, d // 2), jnp.uint32),
            pltpu.VMEM((MOE_RC, d), BF16),
            pltpu.VMEM((MOE_NF, MOE_RC, MOE_TF), BF16),
            pltpu.VMEM((MOE_RING, 2, d, MOE_TF), F32),
            pltpu.VMEM((MOE_RING, EXPERT_FF, MOE_TN), F32),
            pltpu.VMEM((d, 2 * MOE_TF), BF16),
            pltpu.VMEM((EXPERT_FF, MOE_TN), BF16),
            pltpu.VMEM((2, MOE_RC, MOE_TN), F32),
            pltpu.SemaphoreType.DMA(()),
            pltpu.SemaphoreType.DMA((2,)),
            pltpu.SemaphoreType.DMA((MOE_RING,)),
            pltpu.SemaphoreType.DMA((MOE_RING,)),
        ],
    )
    return pl.pallas_call(
        _ffn_kernel,
        grid_spec=grid_spec,
        out_shape=jax.ShapeDtypeStruct((n_rows, d), F32),
        compiler_params=_cparams(("arbitrary",)),
        name="moe_ffn",
    )(meta, idx_tab, idx_tab, hp, w_gate, w_up, w_down)


def _combine_kernel(d0_ref, d1_ref, d0n_ref, d1n_ref, x_ref, route_ref, g2_ref, fg_ref, y_hbm, o_ref,
                    buf, sem):
    tm = COMB_TM
    i = pl.program_id(0)
    slot = i & 1

    def start_rows(a_ref, b_ref, sl):
        def issue(r, c):
            pltpu.make_async_copy(y_hbm.at[pl.ds(a_ref[0, r], 1), :], buf.at[sl, 0, pl.ds(r, 1), :],
                                  sem.at[sl, 0]).start()
            pltpu.make_async_copy(y_hbm.at[pl.ds(b_ref[0, r], 1), :], buf.at[sl, 1, pl.ds(r, 1), :],
                                  sem.at[sl, 1]).start()
            return c

        lax.fori_loop(0, tm, issue, 0)

    @pl.when(i == 0)
    def _():
        start_rows(d0_ref, d1_ref, 0)

    @pl.when(i + 1 < pl.num_programs(0))
    def _():
        start_rows(d0n_ref, d1n_ref, 1 - slot)

    for k in range(TOP_K):
        pltpu.make_async_copy(y_hbm.at[pl.ds(0, tm), :], buf.at[slot, k], sem.at[slot, k]).wait()
    route = route_ref[...]
    y = buf[slot, 0] * route[:, 4:5] + buf[slot, 1] * route[:, 5:6]
    x2 = x_ref[...] + g2_ref[...] * y
    ms = jnp.mean(x2 * x2, axis=-1, keepdims=True)
    o_ref[...] = (x2 * lax.rsqrt(ms + RMS_EPS)) * fg_ref[...]


def _combine(dest0, dest1, x1, route, mod3, final_g, y_sorted):
    t, d = x1.shape
    nb = t // COMB_TM
    smem_idx = lambda: pl.BlockSpec((None, 1, COMB_TM), lambda i: (i, 0, 0), memory_space=pltpu.SMEM)
    smem_next = lambda: pl.BlockSpec((None, 1, COMB_TM), lambda i: (jnp.minimum(i + 1, nb - 1), 0, 0),
                                     memory_space=pltpu.SMEM)
    d0 = dest0.reshape(nb, 1, COMB_TM)
    d1 = dest1.reshape(nb, 1, COMB_TM)
    return pl.pallas_call(
        _combine_kernel,
        grid=(nb,),
        in_specs=[
            smem_idx(),
            smem_idx(),
            smem_next(),
            smem_next(),
            pl.BlockSpec((COMB_TM, d), lambda i: (i, 0)),
            pl.BlockSpec((COMB_TM, ROUTE_LANES), lambda i: (i, 0)),
            _mod_spec(5),
            pl.BlockSpec((1, d), lambda i: (0, 0)),
            pl.BlockSpec(memory_space=pl.ANY),
        ],
        out_specs=pl.BlockSpec((COMB_TM, d), lambda i: (i, 0)),
        out_shape=jax.ShapeDtypeStruct((t, d), F32),
        scratch_shapes=[
            pltpu.VMEM((2, TOP_K, COMB_TM, d), F32),
            pltpu.SemaphoreType.DMA((2, TOP_K)),
        ],
        compiler_params=_cparams(("arbitrary",)),
        name="moe_combine",
    )(d0, d1, d0, d1, x1, route, mod3, final_g.reshape(1, d), y_sorted)


def _dispatch_plan(route, cnt, n_tokens):
    n_assign = n_tokens * TOP_K
    n_rows = n_assign + N_EXPERTS * MOE_SB
    nc = n_assign // MOE_RC + N_EXPERTS
    i32 = jnp.int32
    eid = route[:, 0:2].astype(i32)
    rank = route[:, 2:4].astype(i32)
    counts = cnt[0, EXPERT_LANE0:EXPERT_LANE0 + N_EXPERTS].astype(i32)
    seg_rows = (counts + MOE_SB - 1) // MOE_SB * MOE_SB
    seg_end = jnp.cumsum(seg_rows)
    seg_start = seg_end - seg_rows
    e_hot = eid[:, :, None] == jnp.arange(N_EXPERTS, dtype=i32)
    lookup = lambda tab: jnp.sum(jnp.where(e_hot, tab, 0), axis=-1)
    dest = lookup(seg_start) + rank
    nchunk = (seg_rows + MOE_RC - 1) // MOE_RC
    ch_end = jnp.cumsum(nchunk)
    ch_start = ch_end - nchunk
    nact = ch_end[-1]
    cidx = jnp.arange(nc, dtype=i32)
    ch_e = jnp.minimum(jnp.searchsorted(ch_end, cidx, side="right"), N_EXPERTS - 1).astype(i32)
    k = cidx - ch_start[ch_e]
    ch_row0 = jnp.where(cidx < nact, seg_start[ch_e] + k * MOE_RC, 0)
    ch_nsub = jnp.where(cidx < nact, jnp.clip(seg_rows[ch_e] - k * MOE_RC, 0, MOE_RC) // MOE_SB, 0)
    misc = jnp.zeros((nc,), i32).at[MISC_NACT].set(nact)
    misc = misc.at[MISC_TAIL_ROW0].set(seg_end[-1]).at[MISC_TAIL_PIECES].set((n_rows - seg_end[-1]) // MOE_SB)
    meta = jnp.stack([ch_e, ch_row0, ch_nsub, misc]).astype(i32)
    slot = (lookup(ch_start) + rank // MOE_RC) * MOE_RC + rank % MOE_RC
    tok = jnp.broadcast_to(jnp.arange(n_tokens, dtype=i32)[:, None], (n_tokens, TOP_K))
    idx_tab = (jnp.arange(nc * MOE_RC, dtype=i32) % n_tokens).at[slot.reshape(-1)].set(tok.reshape(-1))
    return dest, meta, idx_tab.reshape(nc, 1, MOE_RC), n_rows


def kernel(x, c, ada_w, ada_b, norm1_g, w_in, lambda_q1, lambda_k1, lambda_q2, lambda_k2, subln_g,
           pool_w, pool_scale, w_out, norm2_g, router_group_w, router_group_b, router_expert_w,
           router_expert_b, expert_w_gate, expert_w_up, expert_w_down, final_norm_g):
    b_, s_, d = x.shape
    assert b_ == 1 and d == D_MODEL and ada_w.shape[0] == 1
    t = b_ * s_
    x2d = x.reshape(t, d)

    mod3 = _ada_mod(c, ada_w[0], ada_b[0]).reshape(6, 1, d)

    h = _norm1(x2d, norm1_g[0], mod3)
    proj = _inproj(h, w_in[0])
    slopes = 2.0 ** (-8.0 * jnp.arange(1, N_HEADS + 1, dtype=F32) / N_HEADS)
    o_attn = _attention(proj, slopes, lambda_q1, lambda_k1, lambda_q2, lambda_k2, subln_g)
    o_pool = _pool(proj, pool_w[0].astype(BF16), pool_scale[0])
    x1 = _outproj(o_attn, o_pool, w_out[0], x2d, mod3)

    rw = jnp.zeros((d, ROUTE_LANES), F32)
    rw = rw.at[:, :N_EXPERT_GROUPS].set(router_group_w[0])
    rw = rw.at[:, EXPERT_LANE0:EXPERT_LANE0 + N_EXPERTS].set(router_expert_w[0])
    rw_hi = rw.astype(BF16)
    rw_lo = (rw - rw_hi.astype(F32)).astype(BF16)
    rbias = jnp.full((1, ROUTE_LANES), NEG_BIG, F32)
    rbias = rbias.at[0, :N_EXPERT_GROUPS].set(router_group_b[0])
    rbias = rbias.at[0, EXPERT_LANE0:EXPERT_LANE0 + N_EXPERTS].set(router_expert_b[0].reshape(-1))
    h2, route, cnt = _norm2_route(x1, norm2_g[0], mod3, rw_hi, rw_lo, rbias)

    dest, meta, idx_tab, n_rows = _dispatch_plan(route, cnt, t)
    y_sorted = _ffn(meta, idx_tab, h2, expert_w_gate[0], expert_w_up[0], expert_w_down[0], n_rows)
    out = _combine(dest[:, 0], dest[:, 1], x1, route, mod3, final_norm_g, y_sorted)
    return out.reshape(b_, s_, d)
```

```python
import functools
import math

import jax
import jax.numpy as jnp
from jax import lax
from jax.experimental import pallas as pl
from jax.experimental.pallas import tpu as pltpu

F32 = jnp.float32
BF16 = jnp.bfloat16

D_MODEL = 4096
ATTN_WIDTH = 2048
POOL_WIDTH = 2048
HEAD_DIM = 128
V_DIM = 2 * HEAD_DIM
N_HEADS = ATTN_WIDTH // V_DIM
QK_WIDTH = N_HEADS * 2 * HEAD_DIM
POOL_WINDOWS = (2, 4, 8, 16)
N_POOL_GROUPS = len(POOL_WINDOWS)
POOL_GROUP_WIDTH = POOL_WIDTH // N_POOL_GROUPS
IN_PROJ_WIDTH = 2 * QK_WIDTH + ATTN_WIDTH + POOL_WIDTH
N_EXPERT_GROUPS = 4
EXPERTS_PER_GROUP = 8
N_EXPERTS = N_EXPERT_GROUPS * EXPERTS_PER_GROUP
TOP_K = 2
EXPERT_FF = 1536
RMS_EPS = 1e-6
LAMBDA_INIT = 0.8 - 0.6 * math.exp(-0.3 * 0)

LANES = 128
VMEM_LIMIT = 60 * 1024 * 1024
NEG_BIG = -1e30
LOG2E = math.log2(math.e)

ADA_TN = 512
NORM_TM = 256
MM_TM = 2048
MM_TN = 512
ATT_T = 512
ATT_HP = 2
POOL_TM = 512
POOL_HALO = 16
OUT_TM = 1024
OUT_TN = 512
ROUTE_LANES = LANES
EXPERT_LANE0 = N_EXPERT_GROUPS
MOE_RC = 1024
MOE_SB = 128
MOE_TF = 128
MOE_NF = EXPERT_FF // MOE_TF
MOE_TN = 256
MOE_NN = D_MODEL // MOE_TN
MOE_RING = 4
MOE_AHEAD = MOE_RING - 1
COMB_TM = 256


def _cparams(sem):
    return pltpu.CompilerParams(dimension_semantics=sem, vmem_limit_bytes=VMEM_LIMIT)


def _ada_kernel(c_ref, w_ref, b_ref, o_ref):
    d, tn = w_ref.shape
    ch = 256
    acc = jnp.zeros((8, tn), F32)
    for r in range(d // ch):
        cc = c_ref[r * ch:(r + 1) * ch, :]
        cc = cc * jax.nn.sigmoid(cc)
        w = w_ref[r * ch:(r + 1) * ch, :]
        acc = acc + (w * cc).reshape(ch // 8, 8, tn).sum(axis=0)
    o_ref[...] = acc.sum(axis=0, keepdims=True) + b_ref[...]


def _ada_mod(c, ada_w, ada_b):
    d, n = ada_w.shape
    return pl.pallas_call(
        _ada_kernel,
        grid=(n // ADA_TN,),
        in_specs=[
            pl.BlockSpec((d, 1), lambda j: (0, 0)),
            pl.BlockSpec((d, ADA_TN), lambda j: (0, j)),
            pl.BlockSpec((1, ADA_TN), lambda j: (0, j)),
        ],
        out_specs=pl.BlockSpec((1, ADA_TN), lambda j: (0, j)),
        out_shape=jax.ShapeDtypeStruct((1, n), F32),
        compiler_params=_cparams(("arbitrary",)),
        name="ada_mod",
    )(c.reshape(d, 1), ada_w, ada_b.reshape(1, n))


def _norm_mod(x, g, sc, sh):
    ms = jnp.mean(x * x, axis=-1, keepdims=True)
    return (x * lax.rsqrt(ms + RMS_EPS)) * g * (1.0 + sc) + sh


def _norm1_kernel(x_ref, g_ref, sc_ref, sh_ref, o_ref):
    o_ref[...] = _norm_mod(x_ref[...], g_ref[...], sc_ref[...], sh_ref[...]).astype(o_ref.dtype)


def _mod_spec(row):
    return pl.BlockSpec((None, 1, D_MODEL), lambda i, row=row: (row, 0, 0))


def _norm1(x2d, g, mod3):
    t, d = x2d.shape
    return pl.pallas_call(
        _norm1_kernel,
        grid=(t // NORM_TM,),
        in_specs=[
            pl.BlockSpec((NORM_TM, d), lambda i: (i, 0)),
            pl.BlockSpec((1, d), lambda i: (0, 0)),
            _mod_spec(1),
            _mod_spec(0),
        ],
        out_specs=pl.BlockSpec((NORM_TM, d), lambda i: (i, 0)),
        out_shape=jax.ShapeDtypeStruct((t, d), BF16),
        compiler_params=_cparams(("arbitrary",)),
        name="norm1_mod",
    )(x2d, g.reshape(1, d), mod3, mod3)


def _inproj_kernel(h_ref, w_ref, o_ref, w_bf, *, n_q_tiles, scale):
    j = pl.program_id(0)

    @pl.when(pl.program_id(1) == 0)
    def _():
        w_bf[...] = w_ref[...].astype(BF16)

    acc = jnp.dot(h_ref[...], w_bf[...], preferred_element_type=F32)
    s = jnp.where(j < n_q_tiles, scale, 1.0).astype(F32)
    o_ref[...] = (acc * s).astype(o_ref.dtype)


def _inproj(h, w):
    t, d = h.shape
    n = w.shape[1]
    kern = functools.partial(_inproj_kernel, n_q_tiles=QK_WIDTH // MM_TN, scale=LOG2E * HEAD_DIM ** -0.5)
    return pl.pallas_call(
        kern,
        grid=(n // MM_TN, t // MM_TM),
        in_specs=[
            pl.BlockSpec((MM_TM, d), lambda j, i: (i, 0)),
            pl.BlockSpec((d, MM_TN), lambda j, i: (0, j)),
        ],
        out_specs=pl.BlockSpec((MM_TM, MM_TN), lambda j, i: (i, j)),
        out_shape=jax.ShapeDtypeStruct((t, n), BF16),
        scratch_shapes=[pltpu.VMEM((d, MM_TN), BF16)],
        compiler_params=_cparams(("arbitrary", "arbitrary")),
        name="in_proj",
    )(h, w)


def _attn_kernel(slopes_ref, q_ref, k_ref, vt_ref, lq1_ref, lk1_ref, lq2_ref, lk2_ref, sg_ref,
                 o_ref, m_scr, l_scr, acc_scr, s_scr, mb_scr, qaug_scr):
    g = pl.program_id(0)
    i = pl.program_id(1)
    t = ATT_T
    heads = range(ATT_HP)
    lane = lax.broadcasted_iota(jnp.int32, (t, LANES), 1)

    piece = lane % 3
    for hh in heads:
        sl = jnp.full((t, LANES), slopes_ref[g * ATT_HP + hh] * LOG2E, F32)
        c1 = sl.astype(BF16).astype(F32)
        r1 = sl - c1
        c2 = r1.astype(BF16).astype(F32)
        c3 = (r1 - c2).astype(BF16).astype(F32)
        qextra = jnp.where(lane < 6, jnp.where(piece == 0, c1, jnp.where(piece == 1, c2, c3)),
                           0.0).astype(BF16)
        for mp in range(2):
            c0 = hh * V_DIM + mp * HEAD_DIM
            qaug_scr[hh, mp, :, 0:HEAD_DIM] = q_ref[:, c0:c0 + HEAD_DIM]
            qaug_scr[hh, mp, :, HEAD_DIM:2 * HEAD_DIM] = qextra

    m_scr[...] = jnp.full(m_scr.shape, NEG_BIG, F32)
    l_scr[...] = jnp.zeros(l_scr.shape, F32)
    acc_scr[...] = jnp.zeros(acc_scr.shape, F32)

    def colreduce(x, op):
        part = op(x.reshape(t // 8, 8, t), axis=0)
        return jnp.broadcast_to(op(part, axis=0, keepdims=True), (8, t))

    def scores(j, buf):
        k0 = pl.multiple_of(j * t, t)
        kblk = k_ref[pl.ds(k0, t), :]
        pos = k0 + lax.broadcasted_iota(jnp.int32, (t, LANES), 0)
        kextra = jnp.where(lane < 3, (pos & ~255).astype(F32),
                           jnp.where(lane < 6, (pos & 255).astype(F32), 0.0)).astype(BF16)
        for hh in heads:
            for mp in range(2):
                c0 = hh * V_DIM + mp * HEAD_DIM
                kaug = jnp.concatenate([kblk[:, c0:c0 + HEAD_DIM], kextra], axis=1)
                s = lax.dot_general(kaug, qaug_scr[hh, mp], (((1,), (1,)), ((), ())),
                                    preferred_element_type=F32)
                s_scr[buf, hh, mp] = s
                mb_scr[buf, hh, mp] = colreduce(s, jnp.max)

    def update(j, buf, masked):
        if masked:
            keep = (lax.broadcasted_iota(jnp.int32, (t, t), 1)
                    >= lax.broadcasted_iota(jnp.int32, (t, t), 0))
        for hh in heads:
            vtb = vt_ref[hh, j]
            for mp in range(2):
                s = s_scr[buf, hh, mp]
                if masked:
                    s = jnp.where(keep, s, NEG_BIG)
                    m_blk = colreduce(s, jnp.max)
                else:
                    m_blk = mb_scr[buf, hh, mp]
                m_prev = m_scr[hh, mp]
                m_new = jnp.maximum(m_prev, m_blk)
                alpha = jnp.exp2(m_prev - m_new)
                p = jnp.exp2(s - jnp.tile(m_new, (t // 8, 1)))
                l_scr[hh, mp] = alpha * l_scr[hh, mp] + colreduce(p, jnp.sum)
                m_scr[hh, mp] = m_new
                acc_scr[hh, mp] = acc_scr[hh, mp] * jnp.tile(alpha, (V_DIM // 8, 1)) + jnp.dot(
                    vtb, p.astype(BF16), preferred_element_type=F32)

    scores(0, 0)

    def pair(p, c):
        j = 2 * p
        update(j, 0, False)
        scores(j + 1, 1)
        update(j + 1, 1, False)
        scores(j + 2, 0)
        return c

    lax.fori_loop(0, lax.shift_right_logical(i, 1), pair, 0)
    odd = (i & 1) == 1

    @pl.when(odd)
    def _():
        update(i - 1, 0, False)
        scores(i, 1)
        update(i, 1, True)

    @pl.when(jnp.logical_not(odd))
    def _():
        update(i, 0, True)

    lam = (jnp.exp(jnp.sum(lq1_ref[...] * lk1_ref[...], axis=1, keepdims=True))
           - jnp.exp(jnp.sum(lq2_ref[...] * lk2_ref[...], axis=1, keepdims=True))
           + LAMBDA_INIT)
    gain = jnp.tile(sg_ref[...], (1, t // LANES)) * (1.0 - LAMBDA_INIT)
    for hh in heads:
        o1 = acc_scr[hh, 0] / jnp.tile(l_scr[hh, 0], (V_DIM // 8, 1))
        o2 = acc_scr[hh, 1] / jnp.tile(l_scr[hh, 1], (V_DIM // 8, 1))
        o = o1 - lam * o2
        ms = jnp.mean(o * o, axis=0, keepdims=True)
        o = (o * lax.rsqrt(ms + RMS_EPS)) * gain
        o_ref[:, hh * V_DIM:(hh + 1) * V_DIM] = o.T.astype(o_ref.dtype)


def _attention(proj, slopes, lq1, lk1, lq2, lk2, subln_g):
    s_len = proj.shape[0]
    t = ATT_T
    nblk = s_len // t
    hp = ATT_HP
    kcol0 = QK_WIDTH // (hp * V_DIM)
    v = proj[:, 2 * QK_WIDTH:2 * QK_WIDTH + ATTN_WIDTH]
    vt = v.reshape(nblk, t, N_HEADS, V_DIM).transpose(2, 0, 3, 1)
    gain = jnp.broadcast_to(subln_g.reshape(V_DIM, 1), (V_DIM, LANES))
    vec = lambda: pl.BlockSpec((1, HEAD_DIM), lambda g, i, sl: (0, 0))
    grid_spec = pltpu.PrefetchScalarGridSpec(
        num_scalar_prefetch=1,
        grid=(N_HEADS // hp, nblk),
        in_specs=[
            pl.BlockSpec((t, hp * V_DIM), lambda g, i, sl: (i, g)),
            pl.BlockSpec((s_len, hp * V_DIM), lambda g, i, sl: (0, kcol0 + g)),
            pl.BlockSpec((hp, nblk, V_DIM, t), lambda g, i, sl: (g, 0, 0, 0)),
            vec(), vec(), vec(), vec(),
            pl.BlockSpec((V_DIM, LANES), lambda g, i, sl: (0, 0)),
        ],
        out_specs=pl.BlockSpec((t, hp * V_DIM), lambda g, i, sl: (i, g)),
        scratch_shapes=[
            pltpu.VMEM((hp, 2, 8, t), F32),
            pltpu.VMEM((hp, 2, 8, t), F32),
            pltpu.VMEM((hp, 2, V_DIM, t), F32),
            pltpu.VMEM((2, hp, 2, t, t), F32),
            pltpu.VMEM((2, hp, 2, 8, t), F32),
            pltpu.VMEM((hp, 2, t, V_DIM), BF16),
        ],
    )
    return pl.pallas_call(
        _attn_kernel,
        grid_spec=grid_spec,
        out_shape=jax.ShapeDtypeStruct((s_len, ATTN_WIDTH), BF16),
        compiler_params=_cparams(("arbitrary", "arbitrary")),
        name="diff_attn",
    )(slopes, proj, proj, vt, lq1, lk1, lq2, lk2, gain)


def _pool_kernel(u_ref, halo_ref, w_ref, sc_ref, o_ref, ext_scr):
    i = pl.program_id(0)
    tm = POOL_TM
    hl = POOL_HALO
    halo = halo_ref[...].astype(F32)
    ext_scr[0:hl, :] = jnp.where(i > 0, halo, 0.0)
    ext_scr[hl:hl + tm, :] = u_ref[...].astype(F32)
    pos = i * tm + lax.broadcasted_iota(jnp.int32, (tm, 1), 0)
    for g, win in enumerate(POOL_WINDOWS):
        c0, c1 = g * POOL_GROUP_WIDTH, (g + 1) * POOL_GROUP_WIDTH
        tok = ext_scr[hl:hl + tm, c0:c1]
        wsum = tok
        for dlt in range(1, win):
            wsum = wsum + ext_scr[hl - dlt:hl - dlt + tm, c0:c1]
        count = jnp.minimum(pos + 1, win).astype(F32)
        pooled = wsum / count - tok
        y = jnp.dot(pooled.astype(BF16), w_ref[g], preferred_element_type=F32)
        o_ref[:, c0:c1] = (y * sc_ref[:, c0:c1]).astype(o_ref.dtype)


def _pool(proj, pool_w_bf, pool_scale):
    s_len = proj.shape[0]
    ucol = (2 * QK_WIDTH + ATTN_WIDTH) // POOL_WIDTH
    rb = POOL_TM // POOL_HALO
    return pl.pallas_call(
        _pool_kernel,
        grid=(s_len // POOL_TM,),
        in_specs=[
            pl.BlockSpec((POOL_TM, POOL_WIDTH), lambda i: (i, ucol)),
            pl.BlockSpec((POOL_HALO, POOL_WIDTH), lambda i: (jnp.maximum(i * rb - 1, 0), ucol)),
            pl.BlockSpec((N_POOL_GROUPS, POOL_GROUP_WIDTH, POOL_GROUP_WIDTH), lambda i: (0, 0, 0)),
            pl.BlockSpec((1, POOL_WIDTH), lambda i: (0, 0)),
        ],
        out_specs=pl.BlockSpec((POOL_TM, POOL_WIDTH), lambda i: (i, 0)),
        out_shape=jax.ShapeDtypeStruct((s_len, POOL_WIDTH), BF16),
        scratch_shapes=[pltpu.VMEM((POOL_HALO + POOL_TM, POOL_WIDTH), F32)],
        compiler_params=_cparams(("arbitrary",)),
        name="pool_mixer",
    )(proj, proj, pool_w_bf, pool_scale.reshape(1, POOL_WIDTH))


def _outproj_kernel(a_ref, p_ref, wa_ref, wp_ref, x_ref, g_ref, o_ref, wa_bf, wp_bf):
    @pl.when(pl.program_id(1) == 0)
    def _():
        wa_bf[...] = wa_ref[...].astype(BF16)
        wp_bf[...] = wp_ref[...].astype(BF16)

    acc = jnp.dot(a_ref[...], wa_bf[...], preferred_element_type=F32)
    acc = acc + jnp.dot(p_ref[...], wp_bf[...], preferred_element_type=F32)
    o_ref[...] = x_ref[...] + g_ref[...] * acc


def _outproj(o_attn, o_pool, w_out, x2d, mod3):
    t, d = x2d.shape
    return pl.pallas_call(
        _outproj_kernel,
        grid=(d // OUT_TN, t // OUT_TM),
        in_specs=[
            pl.BlockSpec((OUT_TM, ATTN_WIDTH), lambda j, i: (i, 0)),
            pl.BlockSpec((OUT_TM, POOL_WIDTH), lambda j, i: (i, 0)),
            pl.BlockSpec((ATTN_WIDTH, OUT_TN), lambda j, i: (0, j)),
            pl.BlockSpec((POOL_WIDTH, OUT_TN), lambda j, i: (1, j)),
            pl.BlockSpec((OUT_TM, OUT_TN), lambda j, i: (i, j)),
            pl.BlockSpec((None, 1, OUT_TN), lambda j, i: (2, 0, j)),
        ],
        out_specs=pl.BlockSpec((OUT_TM, OUT_TN), lambda j, i: (i, j)),
        out_shape=jax.ShapeDtypeStruct((t, d), F32),
        scratch_shapes=[pltpu.VMEM((ATTN_WIDTH, OUT_TN), BF16), pltpu.VMEM((POOL_WIDTH, OUT_TN), BF16)],
        compiler_params=_cparams(("arbitrary", "arbitrary")),
        name="out_proj",
    )(o_attn, o_pool, w_out, w_out, x2d, mod3)


def _route_kernel(x_ref, g_ref, sc_ref, sh_ref, whi_ref, wlo_ref, rb_ref,
                  h_ref, route_ref, cnt_ref, base_scr):
    step = pl.program_id(0)
    tm = NORM_TM

    @pl.when(step == 0)
    def _():
        base_scr[...] = jnp.zeros(base_scr.shape, F32)

    h2 = _norm_mod(x_ref[...], g_ref[...], sc_ref[...], sh_ref[...])
    half = D_MODEL // 2
    lo = pltpu.bitcast(h2[:, :half].astype(BF16).astype(F32), jnp.uint32)
    hi = pltpu.bitcast(h2[:, half:].astype(BF16).astype(F32), jnp.uint32)
    h_ref[...] = hi | (lo >> 16)

    h_hi = h2.astype(BF16)
    h_lo = (h2 - h_hi.astype(F32)).astype(BF16)
    w_hi = whi_ref[...]
    lg = (jnp.dot(h_hi, w_hi, preferred_element_type=F32)
          + jnp.dot(h_lo, w_hi, preferred_element_type=F32)
          + jnp.dot(h_hi, wlo_ref[...], preferred_element_type=F32)) + rb_ref[...]

    lane = lax.broadcasted_iota(jnp.int32, (tm, ROUTE_LANES), 1)

    def first_max(vals):
        v = jnp.max(vals, axis=1, keepdims=True)
        idx = jnp.min(jnp.where(vals == v, lane, ROUTE_LANES), axis=1, keepdims=True)
        return v, idx

    gl = jnp.where(lane < N_EXPERT_GROUPS, lg, NEG_BIG)
    gmax, g_sel = first_max(gl)
    p_g = 1.0 / jnp.sum(jnp.exp(gl - gmax), axis=1, keepdims=True)

    e_lo = EXPERT_LANE0 + g_sel * EXPERTS_PER_GROUP
    el = jnp.where(jnp.logical_and(lane >= e_lo, lane < e_lo + EXPERTS_PER_GROUP), lg, NEG_BIG)
    v1, j1 = first_max(el)
    el2 = jnp.where(lane == j1, NEG_BIG, el)
    v2, j2 = first_max(el2)
    e2 = jnp.exp(v2 - v1)
    gate1 = p_g / (1.0 + e2)
    gate2 = p_g * e2 / (1.0 + e2)

    oh1 = (lane == j1).astype(BF16)
    oh2 = (lane == j2).astype(BF16)
    r_i = lax.broadcasted_iota(jnp.int32, (tm, tm), 0)
    c_i = lax.broadcasted_iota(jnp.int32, (tm, tm), 1)
    lower = (c_i < r_i).astype(BF16)
    before1 = jnp.dot(lower, oh1, preferred_element_type=F32)
    before2 = jnp.dot(lower, oh2, preferred_element_type=F32)
    oh1f = oh1.astype(F32)
    oh2f = oh2.astype(F32)
    tot1 = jnp.sum(oh1f, axis=0, keepdims=True)
    tot2 = jnp.sum(oh2f, axis=0, keepdims=True)
    base = base_scr[0:1, :]
    rank1 = jnp.sum((base + before1) * oh1f, axis=1, keepdims=True)
    rank2 = jnp.sum((base + tot1 + before2) * oh2f, axis=1, keepdims=True)
    new_base = base + tot1 + tot2
    base_scr[0:1, :] = new_base
    cnt_ref[...] = new_base

    eid1 = (j1 - EXPERT_LANE0).astype(F32)
    eid2 = (j2 - EXPERT_LANE0).astype(F32)
    packed = jnp.zeros((tm, ROUTE_LANES), F32)
    for k, val in enumerate((eid1, eid2, rank1, rank2, gate1, gate2)):
        packed = jnp.where(lane == k, val, packed)
    route_ref[...] = packed


def _norm2_route(x1, g, mod3, w_hi, w_lo, rbias):
    t, d = x1.shape
    const = lambda shape: pl.BlockSpec(shape, lambda i: (0,) * len(shape))
    return pl.pallas_call(
        _route_kernel,
        grid=(t // NORM_TM,),
        in_specs=[
            pl.BlockSpec((NORM_TM, d), lambda i: (i, 0)),
            const((1, d)),
            _mod_spec(4),
            _mod_spec(3),
            const((d, ROUTE_LANES)),
            const((d, ROUTE_LANES)),
            const((1, ROUTE_LANES)),
        ],
        out_specs=[
            pl.BlockSpec((NORM_TM, d // 2), lambda i: (i, 0)),
            pl.BlockSpec((NORM_TM, ROUTE_LANES), lambda i: (i, 0)),
            const((1, ROUTE_LANES)),
        ],
        out_shape=[
            jax.ShapeDtypeStruct((t, d // 2), jnp.uint32),
            jax.ShapeDtypeStruct((t, ROUTE_LANES), F32),
            jax.ShapeDtypeStruct((1, ROUTE_LANES), F32),
        ],
        scratch_shapes=[pltpu.VMEM((8, ROUTE_LANES), F32)],
        compiler_params=_cparams(("arbitrary",)),
        name="norm2_route",
    )(x1, g.reshape(1, d), mod3, mod3, w_hi, w_lo, rbias)


META_E, META_ROW0, META_NSUB, META_MISC = 0, 1, 2, 3
MISC_NACT, MISC_TAIL_ROW0, MISC_TAIL_PIECES = 0, 1, 2


def _rest_blocks(nsub, fn):
    rest = nsub
    r0 = jnp.int32(0)
    top = MOE_RC // MOE_SB
    for k in [top >> b for b in range(top.bit_length())]:
        take = (rest & k) != 0

        @pl.when(take)
        def _(r0=r0, k=k):
            fn(pl.multiple_of(r0, MOE_SB), k * MOE_SB)

        r0 = r0 + jnp.where(take, k * MOE_SB, 0)


def _ffn_kernel(meta_ref, idx_ref, idxn_ref, hp_hbm, wg_hbm, wu_hbm, wd_hbm, y_hbm,
                xw, x_bf, h_scr, wgu_f32, wd_f32, wgu_bf, wd_bf, obuf, gsem, osem, wsem, dsem):
    c = pl.program_id(0)
    nact = meta_ref[META_MISC, MISC_NACT]
    active = c < nact
    nsub = meta_ref[META_NSUB, c]
    row0 = meta_ref[META_ROW0, c]
    expert = meta_ref[META_E, c]
    half = D_MODEL // 2
    sb_shift = int(math.log2(MOE_SB))

    def start_gather(ids_ref, n_sub):
        def issue(r, carry):
            pltpu.make_async_copy(hp_hbm.at[pl.ds(ids_ref[0, r], 1), :], xw.at[pl.ds(r, 1), :], gsem).start()
            return carry

        lax.fori_loop(0, lax.shift_left(n_sub, sb_shift), issue, 0)

    def wait_gather(n_sub):
        def wait(i, carry):
            pltpu.make_async_copy(xw.at[pl.ds(0, MOE_SB), :], xw.at[pl.ds(0, MOE_SB), :], gsem).wait()
            return carry

        lax.fori_loop(0, n_sub, wait, 0)

    def gate_up_copies(e, f, slot):
        col = pl.ds(pl.multiple_of(f * MOE_TF, MOE_TF), MOE_TF)
        return (pltpu.make_async_copy(wg_hbm.at[e, :, col], wgu_f32.at[slot, 0], wsem.at[slot]),
                pltpu.make_async_copy(wu_hbm.at[e, :, col], wgu_f32.at[slot, 1], wsem.at[slot]))

    def down_copy(e, n, slot):
        col = pl.ds(pl.multiple_of(n * MOE_TN, MOE_TN), MOE_TN)
        return pltpu.make_async_copy(wd_hbm.at[e, :, col], wd_f32.at[slot], dsem.at[slot])

    def out_copy(slot, i, n):
        r = pl.multiple_of(i * MOE_SB, MOE_SB)
        col = pl.ds(pl.multiple_of(n * MOE_TN, MOE_TN), MOE_TN)
        return pltpu.make_async_copy(
            obuf.at[slot, pl.ds(r, MOE_SB), :],
            y_hbm.at[pl.ds(pl.multiple_of(row0 + r, MOE_SB), MOE_SB), col],
            osem.at[slot])

    def wait_out(slot, n_sub):
        def wait(i, carry):
            out_copy(slot, 0, 0).wait()
            return carry

        lax.fori_loop(0, n_sub, wait, 0)

    def active_chunk():
        @pl.when(c == 0)
        def _():
            start_gather(idx_ref, nsub)
            for f in range(MOE_AHEAD):
                for cp in gate_up_copies(expert, f, f):
                    cp.start()

        wait_gather(nsub)

        def unpack(i, carry):
            r0 = pl.multiple_of(i * MOE_SB, MOE_SB)
            w = xw[pl.ds(r0, MOE_SB), :]
            x_bf[pl.ds(r0, MOE_SB), 0:half] = pltpu.bitcast(w << 16, F32).astype(BF16)
            x_bf[pl.ds(r0, MOE_SB), half:D_MODEL] = pltpu.bitcast(w & jnp.uint32(0xFFFF0000), F32).astype(BF16)
            return carry

        lax.fori_loop(0, nsub, unpack, 0)

        @pl.when(c + 1 < nact)
        def _():
            start_gather(idxn_ref, meta_ref[META_NSUB, c + 1])

        def phase1(f, carry):
            slot = f & (MOE_RING - 1)
            for cp in gate_up_copies(expert, f, slot):
                cp.wait()

            nxt = f + MOE_AHEAD

            @pl.when(nxt < MOE_NF)
            def _():
                for cp in gate_up_copies(expert, nxt, nxt & (MOE_RING - 1)):
                    cp.start()

            @pl.when(nxt >= MOE_NF)
            def _():
                down_copy(expert, nxt - MOE_NF, nxt - MOE_NF).start()

            def gate_up(r0, rows, wgu):
                xs = x_bf[pl.ds(r0, rows), :]
                gu = jnp.dot(xs, wgu, preferred_element_type=F32)
                gt = gu[:, :MOE_TF]
                up = gu[:, MOE_TF:]
                h_scr[f, pl.ds(r0, rows), :] = (jax.nn.silu(gt) * up).astype(BF16)

            wgu = jnp.concatenate([wgu_f32[slot, 0].astype(BF16), wgu_f32[slot, 1].astype(BF16)], axis=1)
            wgu_bf[...] = wgu
            _rest_blocks(nsub, lambda r0, rows: gate_up(r0, rows, wgu_bf[...]))
            return carry

        lax.fori_loop(0, MOE_NF, phase1, 0)

        def phase2(n, carry):
            slot = n & 1
            wslot = n & (MOE_RING - 1)
            down_copy(expert, n, wslot).wait()

            nxt = n + MOE_AHEAD

            @pl.when(nxt < MOE_NN)
            def _():
                down_copy(expert, nxt, nxt & (MOE_RING - 1)).start()

            @pl.when(jnp.logical_and(nxt >= MOE_NN, c + 1 < nact))
            def _():
                for cp in gate_up_copies(meta_ref[META_E, c + 1], nxt - MOE_NN, nxt - MOE_NN):
                    cp.start()

            @pl.when(n >= 2)
            def _():
                wait_out(slot, nsub)

            def down(r0, rows, wd):
                hs = jnp.concatenate([h_scr[f, pl.ds(r0, rows), :] for f in range(MOE_NF)], axis=1)
                obuf[slot, pl.ds(r0, rows), :] = jnp.dot(hs, wd, preferred_element_type=F32)

            wd = wd_f32[wslot].astype(BF16)
            wd_bf[...] = wd
            _rest_blocks(nsub, lambda r0, rows: down(r0, rows, wd_bf[...]))

            def send(i, carry2):
                out_copy(slot, i, n).start()
                return carry2

            lax.fori_loop(0, nsub, send, 0)
            return carry

        lax.fori_loop(0, MOE_NN, phase2, 0)
        wait_out(0, nsub)
        wait_out(1, nsub)

    pl.when(active)(active_chunk)

    @pl.when(jnp.logical_not(active))
    def _():
        per_chunk = MOE_RC // MOE_SB
        first = (c - nact) * per_chunk
        n_here = jnp.clip(meta_ref[META_MISC, MISC_TAIL_PIECES] - first, 0, per_chunk)
        tail0 = meta_ref[META_MISC, MISC_TAIL_ROW0]
        obuf[0, 0:MOE_SB, :] = jnp.zeros((MOE_SB, MOE_TN), F32)

        def zcopy(k, col):
            r = pl.multiple_of(tail0 + (first + k) * MOE_SB, MOE_SB)
            return pltpu.make_async_copy(
                obuf.at[0, pl.ds(0, MOE_SB), :],
                y_hbm.at[pl.ds(r, MOE_SB), pl.ds(col * MOE_TN, MOE_TN)],
                osem.at[0])

        def send(k, carry):
            for col in range(MOE_NN):
                zcopy(k, col).start()
            return carry

        def wait(k, carry):
            for col in range(MOE_NN):
                zcopy(k, col).wait()
            return carry

        lax.fori_loop(0, n_here, send, 0)
        lax.fori_loop(0, n_here, wait, 0)


def _ffn(meta, idx_tab, hp, w_gate, w_up, w_down, n_rows):
    d = D_MODEL
    nc = idx_tab.shape[0]
    any_spec = lambda: pl.BlockSpec(memory_space=pl.ANY)
    grid_spec = pltpu.PrefetchScalarGridSpec(
        num_scalar_prefetch=1,
        grid=(nc,),
        in_specs=[
            pl.BlockSpec((None, 1, MOE_RC), lambda c, m: (c, 0, 0), memory_space=pltpu.SMEM),
            pl.BlockSpec((None, 1, MOE_RC), lambda c, m: (jnp.minimum(c + 1, nc - 1), 0, 0),
                         memory_space=pltpu.SMEM),
            any_spec(), any_spec(), any_spec(), any_spec(),
        ],
        out_specs=any_spec(),
        scratch_shapes=[
            pltpu.VMEM((MOE_RC, d // 2), jnp.uint32),
            pltpu.VMEM((MOE_RC, d), BF16),
            pltpu.VMEM((MOE_NF, MOE_RC, MOE_TF), BF16),
            pltpu.VMEM((MOE_RING, 2, d, MOE_TF), F32),
            pltpu.VMEM((MOE_RING, EXPERT_FF, MOE_TN), F32),
            pltpu.VMEM((d, 2 * MOE_TF), BF16),
            pltpu.VMEM((EXPERT_FF, MOE_TN), BF16),
            pltpu.VMEM((2, MOE_RC, MOE_TN), F32),
            pltpu.SemaphoreType.DMA(()),
            pltpu.SemaphoreType.DMA((2,)),
            pltpu.SemaphoreType.DMA((MOE_RING,)),
            pltpu.SemaphoreType.DMA((MOE_RING,)),
        ],
    )
    return pl.pallas_call(
        _ffn_kernel,
        grid_spec=grid_spec,
        out_shape=jax.ShapeDtypeStruct((n_rows, d), F32),
        compiler_params=_cparams(("arbitrary",)),
        name="moe_ffn",
    )(meta, idx_tab, idx_tab, hp, w_gate, w_up, w_down)


def _combine_kernel(d0_ref, d1_ref, d0n_ref, d1n_ref, x_ref, route_ref, g2_ref, fg_ref, y_hbm, o_ref,
                    buf, sem):
    tm = COMB_TM
    i = pl.program_id(0)
    slot = i & 1

    def start_rows(a_ref, b_ref, sl):
        def issue(r, c):
            pltpu.make_async_copy(y_hbm.at[pl.ds(a_ref[0, r], 1), :], buf.at[sl, 0, pl.ds(r, 1), :],
                                  sem.at[sl, 0]).start()
            pltpu.make_async_copy(y_hbm.at[pl.ds(b_ref[0, r], 1), :], buf.at[sl, 1, pl.ds(r, 1), :],
                                  sem.at[sl, 1]).start()
            return c

        lax.fori_loop(0, tm, issue, 0)

    @pl.when(i == 0)
    def _():
        start_rows(d0_ref, d1_ref, 0)

    @pl.when(i + 1 < pl.num_programs(0))
    def _():
        start_rows(d0n_ref, d1n_ref, 1 - slot)

    for k in range(TOP_K):
        pltpu.make_async_copy(y_hbm.at[pl.ds(0, tm), :], buf.at[slot, k], sem.at[slot, k]).wait()
    route = route_ref[...]
    y = buf[slot, 0] * route[:, 4:5] + buf[slot, 1] * route[:, 5:6]
    x2 = x_ref[...] + g2_ref[...] * y
    ms = jnp.mean(x2 * x2, axis=-1, keepdims=True)
    o_ref[...] = (x2 * lax.rsqrt(ms + RMS_EPS)) * fg_ref[...]


def _combine(dest0, dest1, x1, route, mod3, final_g, y_sorted):
    t, d = x1.shape
    nb = t // COMB_TM
    smem_idx = lambda: pl.BlockSpec((None, 1, COMB_TM), lambda i: (i, 0, 0), memory_space=pltpu.SMEM)
    smem_next = lambda: pl.BlockSpec((None, 1, COMB_TM), lambda i: (jnp.minimum(i + 1, nb - 1), 0, 0),
                                     memory_space=pltpu.SMEM)
    d0 = dest0.reshape(nb, 1, COMB_TM)
    d1 = dest1.reshape(nb, 1, COMB_TM)
    return pl.pallas_call(
        _combine_kernel,
        grid=(nb,),
        in_specs=[
            smem_idx(),
            smem_idx(),
            smem_next(),
            smem_next(),
            pl.BlockSpec((COMB_TM, d), lambda i: (i, 0)),
            pl.BlockSpec((COMB_TM, ROUTE_LANES), lambda i: (i, 0)),
            _mod_spec(5),
            pl.BlockSpec((1, d), lambda i: (0, 0)),
            pl.BlockSpec(memory_space=pl.ANY),
        ],
        out_specs=pl.BlockSpec((COMB_TM, d), lambda i: (i, 0)),
        out_shape=jax.ShapeDtypeStruct((t, d), F32),
        scratch_shapes=[
            pltpu.VMEM((2, TOP_K, COMB_TM, d), F32),
            pltpu.SemaphoreType.DMA((2, TOP_K)),
        ],
        compiler_params=_cparams(("arbitrary",)),
        name="moe_combine",
    )(d0, d1, d0, d1, x1, route, mod3, final_g.reshape(1, d), y_sorted)


def _dispatch_plan(route, cnt, n_tokens):
    n_assign = n_tokens * TOP_K
    n_rows = n_assign + N_EXPERTS * MOE_SB
    nc = n_assign // MOE_RC + N_EXPERTS
    i32 = jnp.int32
    eid = route[:, 0:2].astype(i32)
    rank = route[:, 2:4].astype(i32)
    counts = cnt[0, EXPERT_LANE0:EXPERT_LANE0 + N_EXPERTS].astype(i32)
    seg_rows = (counts + MOE_SB - 1) // MOE_SB * MOE_SB
    seg_end = jnp.cumsum(seg_rows)
    seg_start = seg_end - seg_rows
    e_hot = eid[:, :, None] == jnp.arange(N_EXPERTS, dtype=i32)
    lookup = lambda tab: jnp.sum(jnp.where(e_hot, tab, 0), axis=-1)
    dest = lookup(seg_start) + rank
    nchunk = (seg_rows + MOE_RC - 1) // MOE_RC
    ch_end = jnp.cumsum(nchunk)
    ch_start = ch_end - nchunk
    nact = ch_end[-1]
    cidx = jnp.arange(nc, dtype=i32)
    ch_e = jnp.minimum(jnp.searchsorted(ch_end, cidx, side="right"), N_EXPERTS - 1).astype(i32)
    k = cidx - ch_start[ch_e]
    ch_row0 = jnp.where(cidx < nact, seg_start[ch_e] + k * MOE_RC, 0)
    ch_nsub = jnp.where(cidx < nact, jnp.clip(seg_rows[ch_e] - k * MOE_RC, 0, MOE_RC) // MOE_SB, 0)
    misc = jnp.zeros((nc,), i32).at[MISC_NACT].set(nact)
    misc = misc.at[MISC_TAIL_ROW0].set(seg_end[-1]).at[MISC_TAIL_PIECES].set((n_rows - seg_end[-1]) // MOE_SB)
    meta = jnp.stack([ch_e, ch_row0, ch_nsub, misc]).astype(i32)
    slot = (lookup(ch_start) + rank // MOE_RC) * MOE_RC + rank % MOE_RC
    tok = jnp.broadcast_to(jnp.arange(n_tokens, dtype=i32)[:, None], (n_tokens, TOP_K))
    idx_tab = (jnp.arange(nc * MOE_RC, dtype=i32) % n_tokens).at[slot.reshape(-1)].set(tok.reshape(-1))
    return dest, meta, idx_tab.reshape(nc, 1, MOE_RC), n_rows


def kernel(x, c, ada_w, ada_b, norm1_g, w_in, lambda_q1, lambda_k1, lambda_q2, lambda_k2, subln_g,
           pool_w, pool_scale, w_out, norm2_g, router_group_w, router_group_b, router_expert_w,
           router_expert_b, expert_w_gate, expert_w_up, expert_w_down, final_norm_g):
    b_, s_, d = x.shape
    assert b_ == 1 and d == D_MODEL and ada_w.shape[0] == 1
    t = b_ * s_
    x2d = x.reshape(t, d)

    mod3 = _ada_mod(c, ada_w[0], ada_b[0]).reshape(6, 1, d)

    h = _norm1(x2d, norm1_g[0], mod3)
    proj = _inproj(h, w_in[0])
    slopes = 2.0 ** (-8.0 * jnp.arange(1, N_HEADS + 1, dtype=F32) / N_HEADS)
    o_attn = _attention(proj, slopes, lambda_q1, lambda_k1, lambda_q2, lambda_k2, subln_g)
    o_pool = _pool(proj, pool_w[0].astype(BF16), pool_scale[0])
    x1 = _outproj(o_attn, o_pool, w_out[0], x2d, mod3)

    rw = jnp.zeros((d, ROUTE_LANES), F32)
    rw = rw.at[:, :N_EXPERT_GROUPS].set(router_group_w[0])
    rw = rw.at[:, EXPERT_LANE0:EXPERT_LANE0 + N_EXPERTS].set(router_expert_w[0])
    rw_hi = rw.astype(BF16)
    rw_lo = (rw - rw_hi.astype(F32)).astype(BF16)
    rbias = jnp.full((1, ROUTE_LANES), NEG_BIG, F32)
    rbias = rbias.at[0, :N_EXPERT_GROUPS].set(router_group_b[0])
    rbias = rbias.at[0, EXPERT_LANE0:EXPERT_LANE0 + N_EXPERTS].set(router_expert_b[0].reshape(-1))
    h2, route, cnt = _norm2_route(x1, norm2_g[0], mod3, rw_hi, rw_lo, rbias)

    dest, meta, idx_tab, n_rows = _dispatch_plan(route, cnt, t)
    y_sorted = _ffn(meta, idx_tab, h2, expert_w_gate[0], expert_w_up[0], expert_w_down[0], n_rows)
    out = _combine(dest[:, 0], dest[:, 1], x1, route, mod3, final_norm_g, y_sorted)
    return out.reshape(b_, s_, d)
```

```python
import functools
import math

import jax
import jax.numpy as jnp
from jax import lax
from jax.experimental import pallas as pl
from jax.experimental.pallas import tpu as pltpu

F32 = jnp.float32
BF16 = jnp.bfloat16

D_MODEL = 4096
ATTN_WIDTH = 2048
POOL_WIDTH = 2048
HEAD_DIM = 128
V_DIM = 2 * HEAD_DIM
N_HEADS = ATTN_WIDTH // V_DIM
QK_WIDTH = N_HEADS * 2 * HEAD_DIM
POOL_WINDOWS = (2, 4, 8, 16)
N_POOL_GROUPS = len(POOL_WINDOWS)
POOL_GROUP_WIDTH = POOL_WIDTH // N_POOL_GROUPS
IN_PROJ_WIDTH = 2 * QK_WIDTH + ATTN_WIDTH + POOL_WIDTH
N_EXPERT_GROUPS = 4
EXPERTS_PER_GROUP = 8
N_EXPERTS = N_EXPERT_GROUPS * EXPERTS_PER_GROUP
TOP_K = 2
EXPERT_FF = 1536
RMS_EPS = 1e-6
LAMBDA_INIT = 0.8 - 0.6 * math.exp(-0.3 * 0)

LANES = 128
VMEM_LIMIT = 60 * 1024 * 1024
NEG_BIG = -1e30
LOG2E = math.log2(math.e)

ADA_TN = 512
NORM_TM = 256
MM_TM = 2048
MM_TN = 512
ATT_T = 512
ATT_HP = 2
POOL_TM = 512
POOL_HALO = 16
OUT_TM = 1024
OUT_TN = 512
ROUTE_LANES = LANES
EXPERT_LANE0 = N_EXPERT_GROUPS
MOE_RC = 1024
MOE_SB = 128
MOE_TF = 128
MOE_NF = EXPERT_FF // MOE_TF
MOE_TN = 256
MOE_NN = D_MODEL // MOE_TN
MOE_RING = 4
MOE_AHEAD = MOE_RING - 1
COMB_TM = 256


def _cparams(sem):
    return pltpu.CompilerParams(dimension_semantics=sem, vmem_limit_bytes=VMEM_LIMIT)


def _ada_kernel(c_ref, w_ref, b_ref, o_ref):
    d, tn = w_ref.shape
    ch = 256
    acc = jnp.zeros((8, tn), F32)
    for r in range(d // ch):
        cc = c_ref[r * ch:(r + 1) * ch, :]
        cc = cc * jax.nn.sigmoid(cc)
        w = w_ref[r * ch:(r + 1) * ch, :]
        acc = acc + (w * cc).reshape(ch // 8, 8, tn).sum(axis=0)
    o_ref[...] = acc.sum(axis=0, keepdims=True) + b_ref[...]


def _ada_mod(c, ada_w, ada_b):
    d, n = ada_w.shape
    return pl.pallas_call(
        _ada_kernel,
        grid=(n // ADA_TN,),
        in_specs=[
            pl.BlockSpec((d, 1), lambda j: (0, 0)),
            pl.BlockSpec((d, ADA_TN), lambda j: (0, j)),
            pl.BlockSpec((1, ADA_TN), lambda j: (0, j)),
        ],
        out_specs=pl.BlockSpec((1, ADA_TN), lambda j: (0, j)),
        out_shape=jax.ShapeDtypeStruct((1, n), F32),
        compiler_params=_cparams(("arbitrary",)),
        name="ada_mod",
    )(c.reshape(d, 1), ada_w, ada_b.reshape(1, n))


def _norm_mod(x, g, sc, sh):
    ms = jnp.mean(x * x, axis=-1, keepdims=True)
    return (x * lax.rsqrt(ms + RMS_EPS)) * g * (1.0 + sc) + sh


def _norm1_kernel(x_ref, g_ref, sc_ref, sh_ref, o_ref):
    o_ref[...] = _norm_mod(x_ref[...], g_ref[...], sc_ref[...], sh_ref[...]).astype(o_ref.dtype)


def _mod_spec(row):
    return pl.BlockSpec((None, 1, D_MODEL), lambda i, row=row: (row, 0, 0))


def _norm1(x2d, g, mod3):
    t, d = x2d.shape
    return pl.pallas_call(
        _norm1_kernel,
        grid=(t // NORM_TM,),
        in_specs=[
            pl.BlockSpec((NORM_TM, d), lambda i: (i, 0)),
            pl.BlockSpec((1, d), lambda i: (0, 0)),
            _mod_spec(1),
            _mod_spec(0),
        ],
        out_specs=pl.BlockSpec((NORM_TM, d), lambda i: (i, 0)),
        out_shape=jax.ShapeDtypeStruct((t, d), BF16),
        compiler_params=_cparams(("arbitrary",)),
        name="norm1_mod",
    )(x2d, g.reshape(1, d), mod3, mod3)


def _inproj_kernel(h_ref, w_ref, o_ref, w_bf, *, n_q_tiles, scale):
    j = pl.program_id(0)

    @pl.when(pl.program_id(1) == 0)
    def _():
        w_bf[...] = w_ref[...].astype(BF16)

    acc = jnp.dot(h_ref[...], w_bf[...], preferred_element_type=F32)
    s = jnp.where(j < n_q_tiles, scale, 1.0).astype(F32)
    o_ref[...] = (acc * s).astype(o_ref.dtype)


def _inproj(h, w):
    t, d = h.shape
    n = w.shape[1]
    kern = functools.partial(_inproj_kernel, n_q_tiles=QK_WIDTH // MM_TN, scale=LOG2E * HEAD_DIM ** -0.5)
    return pl.pallas_call(
        kern,
        grid=(n // MM_TN, t // MM_TM),
        in_specs=[
            pl.BlockSpec((MM_TM, d), lambda j, i: (i, 0)),
            pl.BlockSpec((d, MM_TN), lambda j, i: (0, j)),
        ],
        out_specs=pl.BlockSpec((MM_TM, MM_TN), lambda j, i: (i, j)),
        out_shape=jax.ShapeDtypeStruct((t, n), BF16),
        scratch_shapes=[pltpu.VMEM((d, MM_TN), BF16)],
        compiler_params=_cparams(("arbitrary", "arbitrary")),
        name="in_proj",
    )(h, w)


def _attn_kernel(slopes_ref, q_ref, k_ref, vt_ref, lq1_ref, lk1_ref, lq2_ref, lk2_ref, sg_ref,
                 o_ref, m_scr, l_scr, acc_scr, s_scr, mb_scr, qaug_scr):
    g = pl.program_id(0)
    i = pl.program_id(1)
    t = ATT_T
    heads = range(ATT_HP)
    lane = lax.broadcasted_iota(jnp.int32, (t, LANES), 1)

    piece = lane % 3
    for hh in heads:
        sl = jnp.full((t, LANES), slopes_ref[g * ATT_HP + hh] * LOG2E, F32)
        c1 = sl.astype(BF16).astype(F32)
        r1 = sl - c1
        c2 = r1.astype(BF16).astype(F32)
        c3 = (r1 - c2).astype(BF16).astype(F32)
        qextra = jnp.where(lane < 6, jnp.where(piece == 0, c1, jnp.where(piece == 1, c2, c3)),
                           0.0).astype(BF16)
        for mp in range(2):
            c0 = hh * V_DIM + mp * HEAD_DIM
            qaug_scr[hh, mp, :, 0:HEAD_DIM] = q_ref[:, c0:c0 + HEAD_DIM]
            qaug_scr[hh, mp, :, HEAD_DIM:2 * HEAD_DIM] = qextra

    m_scr[...] = jnp.full(m_scr.shape, NEG_BIG, F32)
    l_scr[...] = jnp.zeros(l_scr.shape, F32)
    acc_scr[...] = jnp.zeros(acc_scr.shape, F32)

    def colreduce(x, op):
        part = op(x.reshape(t // 8, 8, t), axis=0)
        return jnp.broadcast_to(op(part, axis=0, keepdims=True), (8, t))

    def scores(j, buf):
        k0 = pl.multiple_of(j * t, t)
        kblk = k_ref[pl.ds(k0, t), :]
        pos = k0 + lax.broadcasted_iota(jnp.int32, (t, LANES), 0)
        kextra = jnp.where(lane < 3, (pos & ~255).astype(F32),
                           jnp.where(lane < 6, (pos & 255).astype(F32), 0.0)).astype(BF16)
        for hh in heads:
            for mp in range(2):
                c0 = hh * V_DIM + mp * HEAD_DIM
                kaug = jnp.concatenate([kblk[:, c0:c0 + HEAD_DIM], kextra], axis=1)
                s = lax.dot_general(kaug, qaug_scr[hh, mp], (((1,), (1,)), ((), ())),
                                    preferred_element_type=F32)
                s_scr[buf, hh, mp] = s
                mb_scr[buf, hh, mp] = colreduce(s, jnp.max)

    def update(j, buf, masked):
        if masked:
            keep = (lax.broadcasted_iota(jnp.int32, (t, t), 1)
                    >= lax.broadcasted_iota(jnp.int32, (t, t), 0))
        for hh in heads:
            vtb = vt_ref[hh, j]
            for mp in range(2):
                s = s_scr[buf, hh, mp]
                if masked:
                    s = jnp.where(keep, s, NEG_BIG)
                    m_blk = colreduce(s, jnp.max)
                else:
                    m_blk = mb_scr[buf, hh, mp]
                m_prev = m_scr[hh, mp]
                m_new = jnp.maximum(m_prev, m_blk)
                alpha = jnp.exp2(m_prev - m_new)
                p = jnp.exp2(s - jnp.tile(m_new, (t // 8, 1)))
                l_scr[hh, mp] = alpha * l_scr[hh, mp] + colreduce(p, jnp.sum)
                m_scr[hh, mp] = m_new
                acc_scr[hh, mp] = acc_scr[hh, mp] * jnp.tile(alpha, (V_DIM // 8, 1)) + jnp.dot(
                    vtb, p.astype(BF16), preferred_element_type=F32)

    scores(0, 0)

    def pairs(j, n_pairs):
        for q in range(n_pairs):
            update(j + 2 * q, 0, False)
            scores(j + 2 * q + 1, 1)
            update(j + 2 * q + 1, 1, False)
            scores(j + 2 * q + 2, 0)

    def quad(p, c):
        pairs(4 * p, 2)
        return c

    n_quads = lax.shift_right_logical(i, 2)
    lax.fori_loop(0, n_quads, quad, 0)

    @pl.when((i & 2) != 0)
    def _():
        pairs(4 * n_quads, 1)

    odd = (i & 1) == 1

    @pl.when(odd)
    def _():
        update(i - 1, 0, False)
        scores(i, 1)
        update(i, 1, True)

    @pl.when(jnp.logical_not(odd))
    def _():
        update(i, 0, True)

    lam = (jnp.exp(jnp.sum(lq1_ref[...] * lk1_ref[...], axis=1, keepdims=True))
           - jnp.exp(jnp.sum(lq2_ref[...] * lk2_ref[...], axis=1, keepdims=True))
           + LAMBDA_INIT)
    gain = jnp.tile(sg_ref[...], (1, t // LANES)) * (1.0 - LAMBDA_INIT)
    for hh in heads:
        o1 = acc_scr[hh, 0] / jnp.tile(l_scr[hh, 0], (V_DIM // 8, 1))
        o2 = acc_scr[hh, 1] / jnp.tile(l_scr[hh, 1], (V_DIM // 8, 1))
        o = o1 - lam * o2
        ms = jnp.mean(o * o, axis=0, keepdims=True)
        o = (o * lax.rsqrt(ms + RMS_EPS)) * gain
        o_ref[:, hh * V_DIM:(hh + 1) * V_DIM] = o.T.astype(o_ref.dtype)


def _attention(proj, slopes, lq1, lk1, lq2, lk2, subln_g):
    s_len = proj.shape[0]
    t = ATT_T
    nblk = s_len // t
    hp = ATT_HP
    kcol0 = QK_WIDTH // (hp * V_DIM)
    v = proj[:, 2 * QK_WIDTH:2 * QK_WIDTH + ATTN_WIDTH]
    vt = v.reshape(nblk, t, N_HEADS, V_DIM).transpose(2, 0, 3, 1)
    gain = jnp.broadcast_to(subln_g.reshape(V_DIM, 1), (V_DIM, LANES))
    vec = lambda: pl.BlockSpec((1, HEAD_DIM), lambda g, i, sl: (0, 0))
    grid_spec = pltpu.PrefetchScalarGridSpec(
        num_scalar_prefetch=1,
        grid=(N_HEADS // hp, nblk),
        in_specs=[
            pl.BlockSpec((t, hp * V_DIM), lambda g, i, sl: (i, g)),
            pl.BlockSpec((s_len, hp * V_DIM), lambda g, i, sl: (0, kcol0 + g)),
            pl.BlockSpec((hp, nblk, V_DIM, t), lambda g, i, sl: (g, 0, 0, 0)),
            vec(), vec(), vec(), vec(),
            pl.BlockSpec((V_DIM, LANES), lambda g, i, sl: (0, 0)),
        ],
        out_specs=pl.BlockSpec((t, hp * V_DIM), lambda g, i, sl: (i, g)),
        scratch_shapes=[
            pltpu.VMEM((hp, 2, 8, t), F32),
            pltpu.VMEM((hp, 2, 8, t), F32),
            pltpu.VMEM((hp, 2, V_DIM, t), F32),
            pltpu.VMEM((2, hp, 2, t, t), F32),
            pltpu.VMEM((2, hp, 2, 8, t), F32),
            pltpu.VMEM((hp, 2, t, V_DIM), BF16),
        ],
    )
    return pl.pallas_call(
        _attn_kernel,
        grid_spec=grid_spec,
        out_shape=jax.ShapeDtypeStruct((s_len, ATTN_WIDTH), BF16),
        compiler_params=_cparams(("arbitrary", "arbitrary")),
        name="diff_attn",
    )(slopes, proj, proj, vt, lq1, lk1, lq2, lk2, gain)


def _pool_kernel(u_ref, halo_ref, w_ref, sc_ref, o_ref, ext_scr):
    i = pl.program_id(0)
    tm = POOL_TM
    hl = POOL_HALO
    halo = halo_ref[...].astype(F32)
    ext_scr[0:hl, :] = jnp.where(i > 0, halo, 0.0)
    ext_scr[hl:hl + tm, :] = u_ref[...].astype(F32)
    pos = i * tm + lax.broadcasted_iota(jnp.int32, (tm, 1), 0)
    for g, win in enumerate(POOL_WINDOWS):
        c0, c1 = g * POOL_GROUP_WIDTH, (g + 1) * POOL_GROUP_WIDTH
        tok = ext_scr[hl:hl + tm, c0:c1]
        wsum = tok
        for dlt in range(1, win):
            wsum = wsum + ext_scr[hl - dlt:hl - dlt + tm, c0:c1]
        count = jnp.minimum(pos + 1, win).astype(F32)
        pooled = wsum / count - tok
        y = jnp.dot(pooled.astype(BF16), w_ref[g], preferred_element_type=F32)
        o_ref[:, c0:c1] = (y * sc_ref[:, c0:c1]).astype(o_ref.dtype)


def _pool(proj, pool_w_bf, pool_scale):
    s_len = proj.shape[0]
    ucol = (2 * QK_WIDTH + ATTN_WIDTH) // POOL_WIDTH
    rb = POOL_TM // POOL_HALO
    return pl.pallas_call(
        _pool_kernel,
        grid=(s_len // POOL_TM,),
        in_specs=[
            pl.BlockSpec((POOL_TM, POOL_WIDTH), lambda i: (i, ucol)),
            pl.BlockSpec((POOL_HALO, POOL_WIDTH), lambda i: (jnp.maximum(i * rb - 1, 0), ucol)),
            pl.BlockSpec((N_POOL_GROUPS, POOL_GROUP_WIDTH, POOL_GROUP_WIDTH), lambda i: (0, 0, 0)),
            pl.BlockSpec((1, POOL_WIDTH), lambda i: (0, 0)),
        ],
        out_specs=pl.BlockSpec((POOL_TM, POOL_WIDTH), lambda i: (i, 0)),
        out_shape=jax.ShapeDtypeStruct((s_len, POOL_WIDTH), BF16),
        scratch_shapes=[pltpu.VMEM((POOL_HALO + POOL_TM, POOL_WIDTH), F32)],
        compiler_params=_cparams(("arbitrary",)),
        name="pool_mixer",
    )(proj, proj, pool_w_bf, pool_scale.reshape(1, POOL_WIDTH))


def _outproj_kernel(a_ref, p_ref, wa_ref, wp_ref, x_ref, g_ref, o_ref, wa_bf, wp_bf):
    @pl.when(pl.program_id(1) == 0)
    def _():
        wa_bf[...] = wa_ref[...].astype(BF16)
        wp_bf[...] = wp_ref[...].astype(BF16)

    acc = jnp.dot(a_ref[...], wa_bf[...], preferred_element_type=F32)
    acc = acc + jnp.dot(p_ref[...], wp_bf[...], preferred_element_type=F32)
    o_ref[...] = x_ref[...] + g_ref[...] * acc


def _outproj(o_attn, o_pool, w_out, x2d, mod3):
    t, d = x2d.shape
    return pl.pallas_call(
        _outproj_kernel,
        grid=(d // OUT_TN, t // OUT_TM),
        in_specs=[
            pl.BlockSpec((OUT_TM, ATTN_WIDTH), lambda j, i: (i, 0)),
            pl.BlockSpec((OUT_TM, POOL_WIDTH), lambda j, i: (i, 0)),
            pl.BlockSpec((ATTN_WIDTH, OUT_TN), lambda j, i: (0, j)),
            pl.BlockSpec((POOL_WIDTH, OUT_TN), lambda j, i: (1, j)),
            pl.BlockSpec((OUT_TM, OUT_TN), lambda j, i: (i, j)),
            pl.BlockSpec((None, 1, OUT_TN), lambda j, i: (2, 0, j)),
        ],
        out_specs=pl.BlockSpec((OUT_TM, OUT_TN), lambda j, i: (i, j)),
        out_shape=jax.ShapeDtypeStruct((t, d), F32),
        scratch_shapes=[pltpu.VMEM((ATTN_WIDTH, OUT_TN), BF16), pltpu.VMEM((POOL_WIDTH, OUT_TN), BF16)],
        compiler_params=_cparams(("arbitrary", "arbitrary")),
        name="out_proj",
    )(o_attn, o_pool, w_out, w_out, x2d, mod3)


def _route_kernel(x_ref, g_ref, sc_ref, sh_ref, whi_ref, wlo_ref, rb_ref,
                  h_ref, route_ref, cnt_ref, base_scr):
    step = pl.program_id(0)
    tm = NORM_TM

    @pl.when(step == 0)
    def _():
        base_scr[...] = jnp.zeros(base_scr.shape, F32)

    h2 = _norm_mod(x_ref[...], g_ref[...], sc_ref[...], sh_ref[...])
    half = D_MODEL // 2
    lo = pltpu.bitcast(h2[:, :half].astype(BF16).astype(F32), jnp.uint32)
    hi = pltpu.bitcast(h2[:, half:].astype(BF16).astype(F32), jnp.uint32)
    h_ref[...] = hi | (lo >> 16)

    h_hi = h2.astype(BF16)
    h_lo = (h2 - h_hi.astype(F32)).astype(BF16)
    w_hi = whi_ref[...]
    lg = (jnp.dot(h_hi, w_hi, preferred_element_type=F32)
          + jnp.dot(h_lo, w_hi, preferred_element_type=F32)
          + jnp.dot(h_hi, wlo_ref[...], preferred_element_type=F32)) + rb_ref[...]

    lane = lax.broadcasted_iota(jnp.int32, (tm, ROUTE_LANES), 1)

    def first_max(vals):
        v = jnp.max(vals, axis=1, keepdims=True)
        idx = jnp.min(jnp.where(vals == v, lane, ROUTE_LANES), axis=1, keepdims=True)
        return v, idx

    gl = jnp.where(lane < N_EXPERT_GROUPS, lg, NEG_BIG)
    gmax, g_sel = first_max(gl)
    p_g = 1.0 / jnp.sum(jnp.exp(gl - gmax), axis=1, keepdims=True)

    e_lo = EXPERT_LANE0 + g_sel * EXPERTS_PER_GROUP
    el = jnp.where(jnp.logical_and(lane >= e_lo, lane < e_lo + EXPERTS_PER_GROUP), lg, NEG_BIG)
    v1, j1 = first_max(el)
    el2 = jnp.where(lane == j1, NEG_BIG, el)
    v2, j2 = first_max(el2)
    e2 = jnp.exp(v2 - v1)
    gate1 = p_g / (1.0 + e2)
    gate2 = p_g * e2 / (1.0 + e2)

    oh1 = (lane == j1).astype(BF16)
    oh2 = (lane == j2).astype(BF16)
    r_i = lax.broadcasted_iota(jnp.int32, (tm, tm), 0)
    c_i = lax.broadcasted_iota(jnp.int32, (tm, tm), 1)
    lower = (c_i < r_i).astype(BF16)
    before1 = jnp.dot(lower, oh1, preferred_element_type=F32)
    before2 = jnp.dot(lower, oh2, preferred_element_type=F32)
    oh1f = oh1.astype(F32)
    oh2f = oh2.astype(F32)
    tot1 = jnp.sum(oh1f, axis=0, keepdims=True)
    tot2 = jnp.sum(oh2f, axis=0, keepdims=True)
    base = base_scr[0:1, :]
    rank1 = jnp.sum((base + before1) * oh1f, axis=1, keepdims=True)
    rank2 = jnp.sum((base + tot1 + before2) * oh2f, axis=1, keepdims=True)
    new_base = base + tot1 + tot2
    base_scr[0:1, :] = new_base
    cnt_ref[...] = new_base

    eid1 = (j1 - EXPERT_LANE0).astype(F32)
    eid2 = (j2 - EXPERT_LANE0).astype(F32)
    packed = jnp.zeros((tm, ROUTE_LANES), F32)
    for k, val in enumerate((eid1, eid2, rank1, rank2, gate1, gate2)):
        packed = jnp.where(lane == k, val, packed)
    route_ref[...] = packed


def _norm2_route(x1, g, mod3, w_hi, w_lo, rbias):
    t, d = x1.shape
    const = lambda shape: pl.BlockSpec(shape, lambda i: (0,) * len(shape))
    return pl.pallas_call(
        _route_kernel,
        grid=(t // NORM_TM,),
        in_specs=[
            pl.BlockSpec((NORM_TM, d), lambda i: (i, 0)),
            const((1, d)),
            _mod_spec(4),
            _mod_spec(3),
            const((d, ROUTE_LANES)),
            const((d, ROUTE_LANES)),
            const((1, ROUTE_LANES)),
        ],
        out_specs=[
            pl.BlockSpec((NORM_TM, d // 2), lambda i: (i, 0)),
            pl.BlockSpec((NORM_TM, ROUTE_LANES), lambda i: (i, 0)),
            const((1, ROUTE_LANES)),
        ],
        out_shape=[
            jax.ShapeDtypeStruct((t, d // 2), jnp.uint32),
            jax.ShapeDtypeStruct((t, ROUTE_LANES), F32),
            jax.ShapeDtypeStruct((1, ROUTE_LANES), F32),
        ],
        scratch_shapes=[pltpu.VMEM((8, ROUTE_LANES), F32)],
        compiler_params=_cparams(("arbitrary",)),
        name="norm2_route",
    )(x1, g.reshape(1, d), mod3, mod3, w_hi, w_lo, rbias)


META_E, META_ROW0, META_NSUB, META_MISC = 0, 1, 2, 3
MISC_NACT, MISC_TAIL_ROW0, MISC_TAIL_PIECES = 0, 1, 2


def _rest_blocks(nsub, fn):
    rest = nsub
    r0 = jnp.int32(0)
    top = MOE_RC // MOE_SB
    for k in [top >> b for b in range(top.bit_length())]:
        take = (rest & k) != 0

        @pl.when(take)
        def _(r0=r0, k=k):
            fn(pl.multiple_of(r0, MOE_SB), k * MOE_SB)

        r0 = r0 + jnp.where(take, k * MOE_SB, 0)


def _ffn_kernel(meta_ref, idx_ref, idxn_ref, hp_hbm, wg_hbm, wu_hbm, wd_hbm, y_hbm,
                xw, x_bf, h_scr, wgu_f32, wd_f32, wgu_bf, wd_bf, obuf, gsem, osem, wsem, dsem):
    c = pl.program_id(0)
    nact = meta_ref[META_MISC, MISC_NACT]
    active = c < nact
    nsub = meta_ref[META_NSUB, c]
    row0 = meta_ref[META_ROW0, c]
    expert = meta_ref[META_E, c]
    half = D_MODEL // 2
    sb_shift = int(math.log2(MOE_SB))

    def start_gather(ids_ref, n_sub):
        def issue(r, carry):
            pltpu.make_async_copy(hp_hbm.at[pl.ds(ids_ref[0, r], 1), :], xw.at[pl.ds(r, 1), :], gsem).start()
            return carry

        lax.fori_loop(0, lax.shift_left(n_sub, sb_shift), issue, 0)

    def wait_gather(n_sub):
        def wait(i, carry):
            pltpu.make_async_copy(xw.at[pl.ds(0, MOE_SB), :], xw.at[pl.ds(0, MOE_SB), :], gsem).wait()
            return carry

        lax.fori_loop(0, n_sub, wait, 0)

    def gate_up_copies(e, f, slot):
        col = pl.ds(pl.multiple_of(f * MOE_TF, MOE_TF), MOE_TF)
        return (pltpu.make_async_copy(wg_hbm.at[e, :, col], wgu_f32.at[slot, 0], wsem.at[slot]),
                pltpu.make_async_copy(wu_hbm.at[e, :, col], wgu_f32.at[slot, 1], wsem.at[slot]))

    def down_copy(e, n, slot):
        col = pl.ds(pl.multiple_of(n * MOE_TN, MOE_TN), MOE_TN)
        return pltpu.make_async_copy(wd_hbm.at[e, :, col], wd_f32.at[slot], dsem.at[slot])

    def out_copy(slot, i, n):
        r = pl.multiple_of(i * MOE_SB, MOE_SB)
        col = pl.ds(pl.multiple_of(n * MOE_TN, MOE_TN), MOE_TN)
        return pltpu.make_async_copy(
            obuf.at[slot, pl.ds(r, MOE_SB), :],
            y_hbm.at[pl.ds(pl.multiple_of(row0 + r, MOE_SB), MOE_SB), col],
            osem.at[slot])

    def wait_out(slot, n_sub):
        def wait(i, carry):
            out_copy(slot, 0, 0).wait()
            return carry

        lax.fori_loop(0, n_sub, wait, 0)

    def active_chunk():
        @pl.when(c == 0)
        def _():
            start_gather(idx_ref, nsub)
            for f in range(MOE_AHEAD):
                for cp in gate_up_copies(expert, f, f):
                    cp.start()

        wait_gather(nsub)

        def unpack(i, carry):
            r0 = pl.multiple_of(i * MOE_SB, MOE_SB)
            w = xw[pl.ds(r0, MOE_SB), :]
            x_bf[pl.ds(r0, MOE_SB), 0:half] = pltpu.bitcast(w << 16, F32).astype(BF16)
            x_bf[pl.ds(r0, MOE_SB), half:D_MODEL] = pltpu.bitcast(w & jnp.uint32(0xFFFF0000), F32).astype(BF16)
            return carry

        lax.fori_loop(0, nsub, unpack, 0)

        @pl.when(c + 1 < nact)
        def _():
            start_gather(idxn_ref, meta_ref[META_NSUB, c + 1])

        def phase1(f, carry):
            slot = f & (MOE_RING - 1)
            for cp in gate_up_copies(expert, f, slot):
                cp.wait()

            nxt = f + MOE_AHEAD

            @pl.when(nxt < MOE_NF)
            def _():
                for cp in gate_up_copies(expert, nxt, nxt & (MOE_RING - 1)):
                    cp.start()

            @pl.when(nxt >= MOE_NF)
            def _():
                down_copy(expert, nxt - MOE_NF, nxt - MOE_NF).start()

            def gate_up(r0, rows, wgu):
                xs = x_bf[pl.ds(r0, rows), :]
                gu = jnp.dot(xs, wgu, preferred_element_type=F32)
                gt = gu[:, :MOE_TF]
                up = gu[:, MOE_TF:]
                h_scr[f, pl.ds(r0, rows), :] = (jax.nn.silu(gt) * up).astype(BF16)

            wgu = jnp.concatenate([wgu_f32[slot, 0].astype(BF16), wgu_f32[slot, 1].astype(BF16)], axis=1)
            wgu_bf[...] = wgu
            _rest_blocks(nsub, lambda r0, rows: gate_up(r0, rows, wgu_bf[...]))
            return carry

        lax.fori_loop(0, MOE_NF, phase1, 0)

        def phase2(n, carry):
            slot = n & 1
            wslot = n & (MOE_RING - 1)
            down_copy(expert, n, wslot).wait()

            nxt = n + MOE_AHEAD

            @pl.when(nxt < MOE_NN)
            def _():
                down_copy(expert, nxt, nxt & (MOE_RING - 1)).start()

            @pl.when(jnp.logical_and(nxt >= MOE_NN, c + 1 < nact))
            def _():
                for cp in gate_up_copies(meta_ref[META_E, c + 1], nxt - MOE_NN, nxt - MOE_NN):
                    cp.start()

            @pl.when(n >= 2)
            def _():
                wait_out(slot, nsub)

            def down(r0, rows, wd):
                hs = jnp.concatenate([h_scr[f, pl.ds(r0, rows), :] for f in range(MOE_NF)], axis=1)
                obuf[slot, pl.ds(r0, rows), :] = jnp.dot(hs, wd, preferred_element_type=F32)

            wd = wd_f32[wslot].astype(BF16)
            wd_bf[...] = wd
            _rest_blocks(nsub, lambda r0, rows: down(r0, rows, wd_bf[...]))

            def send(i, carry2):
                out_copy(slot, i, n).start()
                return carry2

            lax.fori_loop(0, nsub, send, 0)
            return carry

        lax.fori_loop(0, MOE_NN, phase2, 0)
        wait_out(0, nsub)
        wait_out(1, nsub)

    pl.when(active)(active_chunk)

    @pl.when(jnp.logical_not(active))
    def _():
        per_chunk = MOE_RC // MOE_SB
        first = (c - nact) * per_chunk
        n_here = jnp.clip(meta_ref[META_MISC, MISC_TAIL_PIECES] - first, 0, per_chunk)
        tail0 = meta_ref[META_MISC, MISC_TAIL_ROW0]
        obuf[0, 0:MOE_SB, :] = jnp.zeros((MOE_SB, MOE_TN), F32)

        def zcopy(k, col):
            r = pl.multiple_of(tail0 + (first + k) * MOE_SB, MOE_SB)
            return pltpu.make_async_copy(
                obuf.at[0, pl.ds(0, MOE_SB), :],
                y_hbm.at[pl.ds(r, MOE_SB), pl.ds(col * MOE_TN, MOE_TN)],
                osem.at[0])

        def send(k, carry):
            for col in range(MOE_NN):
                zcopy(k, col).start()
            return carry

        def wait(k, carry):
            for col in range(MOE_NN):
                zcopy(k, col).wait()
            return carry

        lax.fori_loop(0, n_here, send, 0)
        lax.fori_loop(0, n_here, wait, 0)


def _ffn(meta, idx_tab, hp, w_gate, w_up, w_down, n_rows):
    d = D_MODEL
    nc = idx_tab.shape[0]
    any_spec = lambda: pl.BlockSpec(memory_space=pl.ANY)
    grid_spec = pltpu.PrefetchScalarGridSpec(
        num_scalar_prefetch=1,
        grid=(nc,),
        in_specs=[
            pl.BlockSpec((None, 1, MOE_RC), lambda c, m: (c, 0, 0), memory_space=pltpu.SMEM),
            pl.BlockSpec((None, 1, MOE_RC), lambda c, m: (jnp.minimum(c + 1, nc - 1), 0, 0),
                         memory_space=pltpu.SMEM),
            any_spec(), any_spec(), any_spec(), any_spec(),
        ],
        out_specs=any_spec(),
        scratch_shapes=[
            pltpu.VMEM((MOE_RC, d // 2), jnp.uint32),
            pltpu.VMEM((MOE_RC, d), BF16),
            pltpu.VMEM((MOE_NF, MOE_RC, MOE_TF), BF16),
            pltpu.VMEM((MOE_RING, 2, d, MOE_TF), F32),
            pltpu.VMEM((MOE_RING, EXPERT_FF, MOE_TN), F32),
            pltpu.VMEM((d, 2 * MOE_TF), BF16),
            pltpu.VMEM((EXPERT_FF, MOE_TN), BF16),
            pltpu.VMEM((2, MOE_RC, MOE_TN), F32),
            pltpu.SemaphoreType.DMA(()),
            pltpu.SemaphoreType.DMA((2,)),
            pltpu.SemaphoreType.DMA((MOE_RING,)),
            pltpu.SemaphoreType.DMA((MOE_RING,)),
        ],
    )
    return pl.pallas_call(
        _ffn_kernel,
        grid_spec=grid_spec,
        out_shape=jax.ShapeDtypeStruct((n_rows, d), F32),
        compiler_params=_cparams(("arbitrary",)),
        name="moe_ffn",
    )(meta, idx_tab, idx_tab, hp, w_gate, w_up, w_down)


def _combine_kernel(d0_ref, d1_ref, d0n_ref, d1n_ref, x_ref, route_ref, g2_ref, fg_ref, y_hbm, o_ref,
                    buf, sem):
    tm = COMB_TM
    i = pl.program_id(0)
    slot = i & 1

    def start_rows(a_ref, b_ref, sl):
        def issue(r, c):
            pltpu.make_async_copy(y_hbm.at[pl.ds(a_ref[0, r], 1), :], buf.at[sl, 0, pl.ds(r, 1), :],
                                  sem.at[sl, 0]).start()
            pltpu.make_async_copy(y_hbm.at[pl.ds(b_ref[0, r], 1), :], buf.at[sl, 1, pl.ds(r, 1), :],
                                  sem.at[sl, 1]).start()
            return c

        lax.fori_loop(0, tm, issue, 0)

    @pl.when(i == 0)
    def _():
        start_rows(d0_ref, d1_ref, 0)

    @pl.when(i + 1 < pl.num_programs(0))
    def _():
        start_rows(d0n_ref, d1n_ref, 1 - slot)

    for k in range(TOP_K):
        pltpu.make_async_copy(y_hbm.at[pl.ds(0, tm), :], buf.at[slot, k], sem.at[slot, k]).wait()
    route = route_ref[...]
    y = buf[slot, 0] * route[:, 4:5] + buf[slot, 1] * route[:, 5:6]
    x2 = x_ref[...] + g2_ref[...] * y
    ms = jnp.mean(x2 * x2, axis=-1, keepdims=True)
    o_ref[...] = (x2 * lax.rsqrt(ms + RMS_EPS)) * fg_ref[...]


def _combine(dest0, dest1, x1, route, mod3, final_g, y_sorted):
    t, d = x1.shape
    nb = t // COMB_TM
    smem_idx = lambda: pl.BlockSpec((None, 1, COMB_TM), lambda i: (i, 0, 0), memory_space=pltpu.SMEM)
    smem_next = lambda: pl.BlockSpec((None, 1, COMB_TM), lambda i: (jnp.minimum(i + 1, nb - 1), 0, 0),
                                     memory_space=pltpu.SMEM)
    d0 = dest0.reshape(nb, 1, COMB_TM)
    d1 = dest1.reshape(nb, 1, COMB_TM)
    return pl.pallas_call(
        _combine_kernel,
        grid=(nb,),
        in_specs=[
            smem_idx(),
            smem_idx(),
            smem_next(),
            smem_next(),
            pl.BlockSpec((COMB_TM, d), lambda i: (i, 0)),
            pl.BlockSpec((COMB_TM, ROUTE_LANES), lambda i: (i, 0)),
            _mod_spec(5),
            pl.BlockSpec((1, d), lambda i: (0, 0)),
            pl.BlockSpec(memory_space=pl.ANY),
        ],
        out_specs=pl.BlockSpec((COMB_TM, d), lambda i: (i, 0)),
        out_shape=jax.ShapeDtypeStruct((t, d), F32),
        scratch_shapes=[
            pltpu.VMEM((2, TOP_K, COMB_TM, d), F32),
            pltpu.SemaphoreType.DMA((2, TOP_K)),
        ],
        compiler_params=_cparams(("arbitrary",)),
        name="moe_combine",
    )(d0, d1, d0, d1, x1, route, mod3, final_g.reshape(1, d), y_sorted)


def _dispatch_plan(route, cnt, n_tokens):
    n_assign = n_tokens * TOP_K
    n_rows = n_assign + N_EXPERTS * MOE_SB
    nc = n_assign // MOE_RC + N_EXPERTS
    i32 = jnp.int32
    eid = route[:, 0:2].astype(i32)
    rank = route[:, 2:4].astype(i32)
    counts = cnt[0, EXPERT_LANE0:EXPERT_LANE0 + N_EXPERTS].astype(i32)
    seg_rows = (counts + MOE_SB - 1) // MOE_SB * MOE_SB
    seg_end = jnp.cumsum(seg_rows)
    seg_start = seg_end - seg_rows
    e_hot = eid[:, :, None] == jnp.arange(N_EXPERTS, dtype=i32)
    lookup = lambda tab: jnp.sum(jnp.where(e_hot, tab, 0), axis=-1)
    dest = lookup(seg_start) + rank
    nchunk = (seg_rows + MOE_RC - 1) // MOE_RC
    ch_end = jnp.cumsum(nchunk)
    ch_start = ch_end - nchunk
    nact = ch_end[-1]
    cidx = jnp.arange(nc, dtype=i32)
    ch_e = jnp.minimum(jnp.searchsorted(ch_end, cidx, side="right"), N_EXPERTS - 1).astype(i32)
    k = cidx - ch_start[ch_e]
    ch_row0 = jnp.where(cidx < nact, seg_start[ch_e] + k * MOE_RC, 0)
    ch_nsub = jnp.where(cidx < nact, jnp.clip(seg_rows[ch_e] - k * MOE_RC, 0, MOE_RC) // MOE_SB, 0)
    misc = jnp.zeros((nc,), i32).at[MISC_NACT].set(nact)
    misc = misc.at[MISC_TAIL_ROW0].set(seg_end[-1]).at[MISC_TAIL_PIECES].set((n_rows - seg_end[-1]) // MOE_SB)
    meta = jnp.stack([ch_e, ch_row0, ch_nsub, misc]).astype(i32)
    slot = (lookup(ch_start) + rank // MOE_RC) * MOE_RC + rank % MOE_RC
    tok = jnp.broadcast_to(jnp.arange(n_tokens, dtype=i32)[:, None], (n_tokens, TOP_K))
    idx_tab = (jnp.arange(nc * MOE_RC, dtype=i32) % n_tokens).at[slot.reshape(-1)].set(tok.reshape(-1))
    return dest, meta, idx_tab.reshape(nc, 1, MOE_RC), n_rows


def kernel(x, c, ada_w, ada_b, norm1_g, w_in, lambda_q1, lambda_k1, lambda_q2, lambda_k2, subln_g,
           pool_w, pool_scale, w_out, norm2_g, router_group_w, router_group_b, router_expert_w,
           router_expert_b, expert_w_gate, expert_w_up, expert_w_down, final_norm_g):
    b_, s_, d = x.shape
    assert b_ == 1 and d == D_MODEL and ada_w.shape[0] == 1
    t = b_ * s_
    x2d = x.reshape(t, d)

    mod3 = _ada_mod(c, ada_w[0], ada_b[0]).reshape(6, 1, d)

    h = _norm1(x2d, norm1_g[0], mod3)
    proj = _inproj(h, w_in[0])
    slopes = 2.0 ** (-8.0 * jnp.arange(1, N_HEADS + 1, dtype=F32) / N_HEADS)
    o_attn = _attention(proj, slopes, lambda_q1, lambda_k1, lambda_q2, lambda_k2, subln_g)
    o_pool = _pool(proj, pool_w[0].astype(BF16), pool_scale[0])
    x1 = _outproj(o_attn, o_pool, w_out[0], x2d, mod3)

    rw = jnp.zeros((d, ROUTE_LANES), F32)
    rw = rw.at[:, :N_EXPERT_GROUPS].set(router_group_w[0])
    rw = rw.at[:, EXPERT_LANE0:EXPERT_LANE0 + N_EXPERTS].set(router_expert_w[0])
    rw_hi = rw.astype(BF16)
    rw_lo = (rw - rw_hi.astype(F32)).astype(BF16)
    rbias = jnp.full((1, ROUTE_LANES), NEG_BIG, F32)
    rbias = rbias.at[0, :N_EXPERT_GROUPS].set(router_group_b[0])
    rbias = rbias.at[0, EXPERT_LANE0:EXPERT_LANE0 + N_EXPERTS].set(router_expert_b[0].reshape(-1))
    h2, route, cnt = _norm2_route(x1, norm2_g[0], mod3, rw_hi, rw_lo, rbias)

    dest, meta, idx_tab, n_rows = _dispatch_plan(route, cnt, t)
    y_sorted = _ffn(meta, idx_tab, h2, expert_w_gate[0], expert_w_up[0], expert_w_down[0], n_rows)
    out = _combine(dest[:, 0], dest[:, 1], x1, route, mod3, final_norm_g, y_sorted)
    return out.reshape(b_, s_, d)
```

```python
import functools
import math

import jax
import jax.numpy as jnp
from jax import lax
from jax.experimental import pallas as pl
from jax.experimental.pallas import tpu as pltpu

F32 = jnp.float32
BF16 = jnp.bfloat16

D_MODEL = 4096
ATTN_WIDTH = 2048
POOL_WIDTH = 2048
HEAD_DIM = 128
V_DIM = 2 * HEAD_DIM
N_HEADS = ATTN_WIDTH // V_DIM
QK_WIDTH = N_HEADS * 2 * HEAD_DIM
POOL_WINDOWS = (2, 4, 8, 16)
N_POOL_GROUPS = len(POOL_WINDOWS)
POOL_GROUP_WIDTH = POOL_WIDTH // N_POOL_GROUPS
IN_PROJ_WIDTH = 2 * QK_WIDTH + ATTN_WIDTH + POOL_WIDTH
N_EXPERT_GROUPS = 4
EXPERTS_PER_GROUP = 8
N_EXPERTS = N_EXPERT_GROUPS * EXPERTS_PER_GROUP
TOP_K = 2
EXPERT_FF = 1536
RMS_EPS = 1e-6
LAMBDA_INIT = 0.8 - 0.6 * math.exp(-0.3 * 0)

LANES = 128
VMEM_LIMIT = 60 * 1024 * 1024
NEG_BIG = -1e30
LOG2E = math.log2(math.e)

ADA_TN = 512
NORM_TM = 256
MM_TM = 2048
MM_TN = 512
ATT_T = 512
ATT_HP = 2
POOL_TM = 512
POOL_HALO = 16
OUT_TM = 1024
OUT_TN = 512
ROUTE_LANES = LANES
EXPERT_LANE0 = N_EXPERT_GROUPS
MOE_RC = 1024
MOE_SB = 128
MOE_TF = 128
MOE_NF = EXPERT_FF // MOE_TF
MOE_TN = 256
MOE_NN = D_MODEL // MOE_TN
MOE_RING = 4
MOE_AHEAD = MOE_RING - 1
WEIGHT_DMA_PRIORITY = 1
COMB_TM = 256


def _cparams(sem):
    return pltpu.CompilerParams(dimension_semantics=sem, vmem_limit_bytes=VMEM_LIMIT)


def _ada_kernel(c_ref, w_ref, b_ref, o_ref):
    d, tn = w_ref.shape
    ch = 256
    acc = jnp.zeros((8, tn), F32)
    for r in range(d // ch):
        cc = c_ref[r * ch:(r + 1) * ch, :]
        cc = cc * jax.nn.sigmoid(cc)
        w = w_ref[r * ch:(r + 1) * ch, :]
        acc = acc + (w * cc).reshape(ch // 8, 8, tn).sum(axis=0)
    o_ref[...] = acc.sum(axis=0, keepdims=True) + b_ref[...]


def _ada_mod(c, ada_w, ada_b):
    d, n = ada_w.shape
    return pl.pallas_call(
        _ada_kernel,
        grid=(n // ADA_TN,),
        in_specs=[
            pl.BlockSpec((d, 1), lambda j: (0, 0)),
            pl.BlockSpec((d, ADA_TN), lambda j: (0, j)),
            pl.BlockSpec((1, ADA_TN), lambda j: (0, j)),
        ],
        out_specs=pl.BlockSpec((1, ADA_TN), lambda j: (0, j)),
        out_shape=jax.ShapeDtypeStruct((1, n), F32),
        compiler_params=_cparams(("arbitrary",)),
        name="ada_mod",
    )(c.reshape(d, 1), ada_w, ada_b.reshape(1, n))


def _norm_mod(x, g, sc, sh):
    ms = jnp.mean(x * x, axis=-1, keepdims=True)
    return (x * lax.rsqrt(ms + RMS_EPS)) * g * (1.0 + sc) + sh


def _norm1_kernel(x_ref, g_ref, sc_ref, sh_ref, o_ref):
    o_ref[...] = _norm_mod(x_ref[...], g_ref[...], sc_ref[...], sh_ref[...]).astype(o_ref.dtype)


def _mod_spec(row):
    return pl.BlockSpec((None, 1, D_MODEL), lambda i, row=row: (row, 0, 0))


def _norm1(x2d, g, mod3):
    t, d = x2d.shape
    return pl.pallas_call(
        _norm1_kernel,
        grid=(t // NORM_TM,),
        in_specs=[
            pl.BlockSpec((NORM_TM, d), lambda i: (i, 0)),
            pl.BlockSpec((1, d), lambda i: (0, 0)),
            _mod_spec(1),
            _mod_spec(0),
        ],
        out_specs=pl.BlockSpec((NORM_TM, d), lambda i: (i, 0)),
        out_shape=jax.ShapeDtypeStruct((t, d), BF16),
        compiler_params=_cparams(("arbitrary",)),
        name="norm1_mod",
    )(x2d, g.reshape(1, d), mod3, mod3)


def _inproj_kernel(h_ref, w_ref, o_ref, w_bf, *, n_q_tiles, scale):
    j = pl.program_id(0)

    @pl.when(pl.program_id(1) == 0)
    def _():
        w_bf[...] = w_ref[...].astype(BF16)

    acc = jnp.dot(h_ref[...], w_bf[...], preferred_element_type=F32)
    s = jnp.where(j < n_q_tiles, scale, 1.0).astype(F32)
    o_ref[...] = (acc * s).astype(o_ref.dtype)


def _inproj(h, w):
    t, d = h.shape
    n = w.shape[1]
    kern = functools.partial(_inproj_kernel, n_q_tiles=QK_WIDTH // MM_TN, scale=LOG2E * HEAD_DIM ** -0.5)
    return pl.pallas_call(
        kern,
        grid=(n // MM_TN, t // MM_TM),
        in_specs=[
            pl.BlockSpec((MM_TM, d), lambda j, i: (i, 0)),
            pl.BlockSpec((d, MM_TN), lambda j, i: (0, j)),
        ],
        out_specs=pl.BlockSpec((MM_TM, MM_TN), lambda j, i: (i, j)),
        out_shape=jax.ShapeDtypeStruct((t, n), BF16),
        scratch_shapes=[pltpu.VMEM((d, MM_TN), BF16)],
        compiler_params=_cparams(("arbitrary", "arbitrary")),
        name="in_proj",
    )(h, w)


def _attn_kernel(slopes_ref, q_ref, k_ref, vt_ref, lq1_ref, lk1_ref, lq2_ref, lk2_ref, sg_ref,
                 o_ref, m_scr, l_scr, acc_scr, s_scr, mb_scr, qaug_scr):
    g = pl.program_id(0)
    i = pl.program_id(1)
    t = ATT_T
    heads = range(ATT_HP)
    lane = lax.broadcasted_iota(jnp.int32, (t, LANES), 1)

    piece = lane % 3
    for hh in heads:
        sl = jnp.full((t, LANES), slopes_ref[g * ATT_HP + hh] * LOG2E, F32)
        c1 = sl.astype(BF16).astype(F32)
        r1 = sl - c1
        c2 = r1.astype(BF16).astype(F32)
        c3 = (r1 - c2).astype(BF16).astype(F32)
        qextra = jnp.where(lane < 6, jnp.where(piece == 0, c1, jnp.where(piece == 1, c2, c3)),
                           0.0).astype(BF16)
        for mp in range(2):
            c0 = hh * V_DIM + mp * HEAD_DIM
            qaug_scr[hh, mp, :, 0:HEAD_DIM] = q_ref[:, c0:c0 + HEAD_DIM]
            qaug_scr[hh, mp, :, HEAD_DIM:2 * HEAD_DIM] = qextra

    m_scr[...] = jnp.full(m_scr.shape, NEG_BIG, F32)
    l_scr[...] = jnp.zeros(l_scr.shape, F32)
    acc_scr[...] = jnp.zeros(acc_scr.shape, F32)

    def colreduce(x, op):
        part = op(x.reshape(t // 8, 8, t), axis=0)
        return jnp.broadcast_to(op(part, axis=0, keepdims=True), (8, t))

    def scores(j, buf):
        k0 = pl.multiple_of(j * t, t)
        kblk = k_ref[pl.ds(k0, t), :]
        pos = k0 + lax.broadcasted_iota(jnp.int32, (t, LANES), 0)
        kextra = jnp.where(lane < 3, (pos & ~255).astype(F32),
                           jnp.where(lane < 6, (pos & 255).astype(F32), 0.0)).astype(BF16)
        for hh in heads:
            for mp in range(2):
                c0 = hh * V_DIM + mp * HEAD_DIM
                kaug = jnp.concatenate([kblk[:, c0:c0 + HEAD_DIM], kextra], axis=1)
                s = lax.dot_general(kaug, qaug_scr[hh, mp], (((1,), (1,)), ((), ())),
                                    preferred_element_type=F32)
                s_scr[buf, hh, mp] = s
                mb_scr[buf, hh, mp] = colreduce(s, jnp.max)

    def update(j, buf, masked):
        if masked:
            keep = (lax.broadcasted_iota(jnp.int32, (t, t), 1)
                    >= lax.broadcasted_iota(jnp.int32, (t, t), 0))
        for hh in heads:
            vtb = vt_ref[hh, j]
            for mp in range(2):
                s = s_scr[buf, hh, mp]
                if masked:
                    s = jnp.where(keep, s, NEG_BIG)
                    m_blk = colreduce(s, jnp.max)
                else:
                    m_blk = mb_scr[buf, hh, mp]
                m_prev = m_scr[hh, mp]
                m_new = jnp.maximum(m_prev, m_blk)
                alpha = jnp.exp2(m_prev - m_new)
                p = jnp.exp2(s - jnp.tile(m_new, (t // 8, 1)))
                l_scr[hh, mp] = alpha * l_scr[hh, mp] + colreduce(p, jnp.sum)
                m_scr[hh, mp] = m_new
                acc_scr[hh, mp] = acc_scr[hh, mp] * jnp.tile(alpha, (V_DIM // 8, 1)) + jnp.dot(
                    vtb, p.astype(BF16), preferred_element_type=F32)

    scores(0, 0)

    def pairs(j, n_pairs):
        for q in range(n_pairs):
            update(j + 2 * q, 0, False)
            scores(j + 2 * q + 1, 1)
            update(j + 2 * q + 1, 1, False)
            scores(j + 2 * q + 2, 0)

    def quad(p, c):
        pairs(4 * p, 2)
        return c

    n_quads = lax.shift_right_logical(i, 2)
    lax.fori_loop(0, n_quads, quad, 0)

    @pl.when((i & 2) != 0)
    def _():
        pairs(4 * n_quads, 1)

    odd = (i & 1) == 1

    @pl.when(odd)
    def _():
        update(i - 1, 0, False)
        scores(i, 1)
        update(i, 1, True)

    @pl.when(jnp.logical_not(odd))
    def _():
        update(i, 0, True)

    lam = (jnp.exp(jnp.sum(lq1_ref[...] * lk1_ref[...], axis=1, keepdims=True))
           - jnp.exp(jnp.sum(lq2_ref[...] * lk2_ref[...], axis=1, keepdims=True))
           + LAMBDA_INIT)
    gain = jnp.tile(sg_ref[...], (1, t // LANES)) * (1.0 - LAMBDA_INIT)
    for hh in heads:
        o1 = acc_scr[hh, 0] / jnp.tile(l_scr[hh, 0], (V_DIM // 8, 1))
        o2 = acc_scr[hh, 1] / jnp.tile(l_scr[hh, 1], (V_DIM // 8, 1))
        o = o1 - lam * o2
        ms = jnp.mean(o * o, axis=0, keepdims=True)
        o = (o * lax.rsqrt(ms + RMS_EPS)) * gain
        o_ref[:, hh * V_DIM:(hh + 1) * V_DIM] = o.T.astype(o_ref.dtype)


def _attention(proj, slopes, lq1, lk1, lq2, lk2, subln_g):
    s_len = proj.shape[0]
    t = ATT_T
    nblk = s_len // t
    hp = ATT_HP
    kcol0 = QK_WIDTH // (hp * V_DIM)
    v = proj[:, 2 * QK_WIDTH:2 * QK_WIDTH + ATTN_WIDTH]
    vt = v.reshape(nblk, t, N_HEADS, V_DIM).transpose(2, 0, 3, 1)
    gain = jnp.broadcast_to(subln_g.reshape(V_DIM, 1), (V_DIM, LANES))
    vec = lambda: pl.BlockSpec((1, HEAD_DIM), lambda g, i, sl: (0, 0))
    grid_spec = pltpu.PrefetchScalarGridSpec(
        num_scalar_prefetch=1,
        grid=(N_HEADS // hp, nblk),
        in_specs=[
            pl.BlockSpec((t, hp * V_DIM), lambda g, i, sl: (i, g)),
            pl.BlockSpec((s_len, hp * V_DIM), lambda g, i, sl: (0, kcol0 + g)),
            pl.BlockSpec((hp, nblk, V_DIM, t), lambda g, i, sl: (g, 0, 0, 0)),
            vec(), vec(), vec(), vec(),
            pl.BlockSpec((V_DIM, LANES), lambda g, i, sl: (0, 0)),
        ],
        out_specs=pl.BlockSpec((t, hp * V_DIM), lambda g, i, sl: (i, g)),
        scratch_shapes=[
            pltpu.VMEM((hp, 2, 8, t), F32),
            pltpu.VMEM((hp, 2, 8, t), F32),
            pltpu.VMEM((hp, 2, V_DIM, t), F32),
            pltpu.VMEM((2, hp, 2, t, t), F32),
            pltpu.VMEM((2, hp, 2, 8, t), F32),
            pltpu.VMEM((hp, 2, t, V_DIM), BF16),
        ],
    )
    return pl.pallas_call(
        _attn_kernel,
        grid_spec=grid_spec,
        out_shape=jax.ShapeDtypeStruct((s_len, ATTN_WIDTH), BF16),
        compiler_params=_cparams(("arbitrary", "arbitrary")),
        name="diff_attn",
    )(slopes, proj, proj, vt, lq1, lk1, lq2, lk2, gain)


def _pool_kernel(u_ref, halo_ref, w_ref, sc_ref, o_ref, ext_scr):
    i = pl.program_id(0)
    tm = POOL_TM
    hl = POOL_HALO
    halo = halo_ref[...].astype(F32)
    ext_scr[0:hl, :] = jnp.where(i > 0, halo, 0.0)
    ext_scr[hl:hl + tm, :] = u_ref[...].astype(F32)
    pos = i * tm + lax.broadcasted_iota(jnp.int32, (tm, 1), 0)
    for g, win in enumerate(POOL_WINDOWS):
        c0, c1 = g * POOL_GROUP_WIDTH, (g + 1) * POOL_GROUP_WIDTH
        tok = ext_scr[hl:hl + tm, c0:c1]
        wsum = tok
        for dlt in range(1, win):
            wsum = wsum + ext_scr[hl - dlt:hl - dlt + tm, c0:c1]
        count = jnp.minimum(pos + 1, win).astype(F32)
        pooled = wsum / count - tok
        y = jnp.dot(pooled.astype(BF16), w_ref[g], preferred_element_type=F32)
        o_ref[:, c0:c1] = (y * sc_ref[:, c0:c1]).astype(o_ref.dtype)


def _pool(proj, pool_w_bf, pool_scale):
    s_len = proj.shape[0]
    ucol = (2 * QK_WIDTH + ATTN_WIDTH) // POOL_WIDTH
    rb = POOL_TM // POOL_HALO
    return pl.pallas_call(
        _pool_kernel,
        grid=(s_len // POOL_TM,),
        in_specs=[
            pl.BlockSpec((POOL_TM, POOL_WIDTH), lambda i: (i, ucol)),
            pl.BlockSpec((POOL_HALO, POOL_WIDTH), lambda i: (jnp.maximum(i * rb - 1, 0), ucol)),
            pl.BlockSpec((N_POOL_GROUPS, POOL_GROUP_WIDTH, POOL_GROUP_WIDTH), lambda i: (0, 0, 0)),
            pl.BlockSpec((1, POOL_WIDTH), lambda i: (0, 0)),
        ],
        out_specs=pl.BlockSpec((POOL_TM, POOL_WIDTH), lambda i: (i, 0)),
        out_shape=jax.ShapeDtypeStruct((s_len, POOL_WIDTH), BF16),
        scratch_shapes=[pltpu.VMEM((POOL_HALO + POOL_TM, POOL_WIDTH), F32)],
        compiler_params=_cparams(("arbitrary",)),
        name="pool_mixer",
    )(proj, proj, pool_w_bf, pool_scale.reshape(1, POOL_WIDTH))


def _outproj_kernel(a_ref, p_ref, wa_ref, wp_ref, x_ref, g_ref, o_ref, wa_bf, wp_bf):
    @pl.when(pl.program_id(1) == 0)
    def _():
        wa_bf[...] = wa_ref[...].astype(BF16)
        wp_bf[...] = wp_ref[...].astype(BF16)

    acc = jnp.dot(a_ref[...], wa_bf[...], preferred_element_type=F32)
    acc = acc + jnp.dot(p_ref[...], wp_bf[...], preferred_element_type=F32)
    o_ref[...] = x_ref[...] + g_ref[...] * acc


def _outproj(o_attn, o_pool, w_out, x2d, mod3):
    t, d = x2d.shape
    return pl.pallas_call(
        _outproj_kernel,
        grid=(d // OUT_TN, t // OUT_TM),
        in_specs=[
            pl.BlockSpec((OUT_TM, ATTN_WIDTH), lambda j, i: (i, 0)),
            pl.BlockSpec((OUT_TM, POOL_WIDTH), lambda j, i: (i, 0)),
            pl.BlockSpec((ATTN_WIDTH, OUT_TN), lambda j, i: (0, j)),
            pl.BlockSpec((POOL_WIDTH, OUT_TN), lambda j, i: (1, j)),
            pl.BlockSpec((OUT_TM, OUT_TN), lambda j, i: (i, j)),
            pl.BlockSpec((None, 1, OUT_TN), lambda j, i: (2, 0, j)),
        ],
        out_specs=pl.BlockSpec((OUT_TM, OUT_TN), lambda j, i: (i, j)),
        out_shape=jax.ShapeDtypeStruct((t, d), F32),
        scratch_shapes=[pltpu.VMEM((ATTN_WIDTH, OUT_TN), BF16), pltpu.VMEM((POOL_WIDTH, OUT_TN), BF16)],
        compiler_params=_cparams(("arbitrary", "arbitrary")),
        name="out_proj",
    )(o_attn, o_pool, w_out, w_out, x2d, mod3)


def _route_kernel(x_ref, g_ref, sc_ref, sh_ref, whi_ref, wlo_ref, rb_ref,
                  h_ref, route_ref, cnt_ref, base_scr):
    step = pl.program_id(0)
    tm = NORM_TM

    @pl.when(step == 0)
    def _():
        base_scr[...] = jnp.zeros(base_scr.shape, F32)

    h2 = _norm_mod(x_ref[...], g_ref[...], sc_ref[...], sh_ref[...])
    half = D_MODEL // 2
    lo = pltpu.bitcast(h2[:, :half].astype(BF16).astype(F32), jnp.uint32)
    hi = pltpu.bitcast(h2[:, half:].astype(BF16).astype(F32), jnp.uint32)
    h_ref[...] = hi | (lo >> 16)

    h_hi = h2.astype(BF16)
    h_lo = (h2 - h_hi.astype(F32)).astype(BF16)
    w_hi = whi_ref[...]
    lg = (jnp.dot(h_hi, w_hi, preferred_element_type=F32)
          + jnp.dot(h_lo, w_hi, preferred_element_type=F32)
          + jnp.dot(h_hi, wlo_ref[...], preferred_element_type=F32)) + rb_ref[...]

    lane = lax.broadcasted_iota(jnp.int32, (tm, ROUTE_LANES), 1)

    def first_max(vals):
        v = jnp.max(vals, axis=1, keepdims=True)
        idx = jnp.min(jnp.where(vals == v, lane, ROUTE_LANES), axis=1, keepdims=True)
        return v, idx

    gl = jnp.where(lane < N_EXPERT_GROUPS, lg, NEG_BIG)
    gmax, g_sel = first_max(gl)
    p_g = 1.0 / jnp.sum(jnp.exp(gl - gmax), axis=1, keepdims=True)

    e_lo = EXPERT_LANE0 + g_sel * EXPERTS_PER_GROUP
    el = jnp.where(jnp.logical_and(lane >= e_lo, lane < e_lo + EXPERTS_PER_GROUP), lg, NEG_BIG)
    v1, j1 = first_max(el)
    el2 = jnp.where(lane == j1, NEG_BIG, el)
    v2, j2 = first_max(el2)
    e2 = jnp.exp(v2 - v1)
    gate1 = p_g / (1.0 + e2)
    gate2 = p_g * e2 / (1.0 + e2)

    oh1 = (lane == j1).astype(BF16)
    oh2 = (lane == j2).astype(BF16)
    r_i = lax.broadcasted_iota(jnp.int32, (tm, tm), 0)
    c_i = lax.broadcasted_iota(jnp.int32, (tm, tm), 1)
    lower = (c_i < r_i).astype(BF16)
    before1 = jnp.dot(lower, oh1, preferred_element_type=F32)
    before2 = jnp.dot(lower, oh2, preferred_element_type=F32)
    oh1f = oh1.astype(F32)
    oh2f = oh2.astype(F32)
    tot1 = jnp.sum(oh1f, axis=0, keepdims=True)
    tot2 = jnp.sum(oh2f, axis=0, keepdims=True)
    base = base_scr[0:1, :]
    rank1 = jnp.sum((base + before1) * oh1f, axis=1, keepdims=True)
    rank2 = jnp.sum((base + tot1 + before2) * oh2f, axis=1, keepdims=True)
    new_base = base + tot1 + tot2
    base_scr[0:1, :] = new_base
    cnt_ref[...] = new_base

    eid1 = (j1 - EXPERT_LANE0).astype(F32)
    eid2 = (j2 - EXPERT_LANE0).astype(F32)
    packed = jnp.zeros((tm, ROUTE_LANES), F32)
    for k, val in enumerate((eid1, eid2, rank1, rank2, gate1, gate2)):
        packed = jnp.where(lane == k, val, packed)
    route_ref[...] = packed


def _norm2_route(x1, g, mod3, w_hi, w_lo, rbias):
    t, d = x1.shape
    const = lambda shape: pl.BlockSpec(shape, lambda i: (0,) * len(shape))
    return pl.pallas_call(
        _route_kernel,
        grid=(t // NORM_TM,),
        in_specs=[
            pl.BlockSpec((NORM_TM, d), lambda i: (i, 0)),
            const((1, d)),
            _mod_spec(4),
            _mod_spec(3),
            const((d, ROUTE_LANES)),
            const((d, ROUTE_LANES)),
            const((1, ROUTE_LANES)),
        ],
        out_specs=[
            pl.BlockSpec((NORM_TM, d // 2), lambda i: (i, 0)),
            pl.BlockSpec((NORM_TM, ROUTE_LANES), lambda i: (i, 0)),
            const((1, ROUTE_LANES)),
        ],
        out_shape=[
            jax.ShapeDtypeStruct((t, d // 2), jnp.uint32),
            jax.ShapeDtypeStruct((t, ROUTE_LANES), F32),
            jax.ShapeDtypeStruct((1, ROUTE_LANES), F32),
        ],
        scratch_shapes=[pltpu.VMEM((8, ROUTE_LANES), F32)],
        compiler_params=_cparams(("arbitrary",)),
        name="norm2_route",
    )(x1, g.reshape(1, d), mod3, mod3, w_hi, w_lo, rbias)


META_E, META_ROW0, META_NSUB, META_MISC = 0, 1, 2, 3
MISC_NACT, MISC_TAIL_ROW0, MISC_TAIL_PIECES = 0, 1, 2


def _rest_blocks(nsub, fn):
    rest = nsub
    r0 = jnp.int32(0)
    top = MOE_RC // MOE_SB
    for k in [top >> b for b in range(top.bit_length())]:
        take = (rest & k) != 0

        @pl.when(take)
        def _(r0=r0, k=k):
            fn(pl.multiple_of(r0, MOE_SB), k * MOE_SB)

        r0 = r0 + jnp.where(take, k * MOE_SB, 0)


def _ffn_kernel(meta_ref, idx_ref, idxn_ref, hp_hbm, wg_hbm, wu_hbm, wd_hbm, y_hbm,
                xw, x_bf, h_scr, wgu_f32, wd_f32, wgu_bf, wd_bf, obuf, gsem, osem, wsem, dsem):
    c = pl.program_id(0)
    nact = meta_ref[META_MISC, MISC_NACT]
    active = c < nact
    nsub = meta_ref[META_NSUB, c]
    row0 = meta_ref[META_ROW0, c]
    expert = meta_ref[META_E, c]
    half = D_MODEL // 2
    sb_shift = int(math.log2(MOE_SB))

    def start_gather(ids_ref, n_sub):
        def issue(r, carry):
            pltpu.make_async_copy(hp_hbm.at[pl.ds(ids_ref[0, r], 1), :], xw.at[pl.ds(r, 1), :], gsem).start()
            return carry

        lax.fori_loop(0, lax.shift_left(n_sub, sb_shift), issue, 0)

    def wait_gather(n_sub):
        def wait(i, carry):
            pltpu.make_async_copy(xw.at[pl.ds(0, MOE_SB), :], xw.at[pl.ds(0, MOE_SB), :], gsem).wait()
            return carry

        lax.fori_loop(0, n_sub, wait, 0)

    def gate_up_copies(e, f, slot):
        col = pl.ds(pl.multiple_of(f * MOE_TF, MOE_TF), MOE_TF)
        return (pltpu.make_async_copy(wg_hbm.at[e, :, col], wgu_f32.at[slot, 0], wsem.at[slot]),
                pltpu.make_async_copy(wu_hbm.at[e, :, col], wgu_f32.at[slot, 1], wsem.at[slot]))

    def down_copy(e, n, slot):
        col = pl.ds(pl.multiple_of(n * MOE_TN, MOE_TN), MOE_TN)
        return pltpu.make_async_copy(wd_hbm.at[e, :, col], wd_f32.at[slot], dsem.at[slot])

    def out_copy(slot, i, n):
        r = pl.multiple_of(i * MOE_SB, MOE_SB)
        col = pl.ds(pl.multiple_of(n * MOE_TN, MOE_TN), MOE_TN)
        return pltpu.make_async_copy(
            obuf.at[slot, pl.ds(r, MOE_SB), :],
            y_hbm.at[pl.ds(pl.multiple_of(row0 + r, MOE_SB), MOE_SB), col],
            osem.at[slot])

    def wait_out(slot, n_sub):
        def wait(i, carry):
            out_copy(slot, 0, 0).wait()
            return carry

        lax.fori_loop(0, n_sub, wait, 0)

    def active_chunk():
        @pl.when(c == 0)
        def _():
            start_gather(idx_ref, nsub)
            for f in range(MOE_AHEAD):
                for cp in gate_up_copies(expert, f, f):
                    cp.start(priority=WEIGHT_DMA_PRIORITY)

        wait_gather(nsub)

        def unpack(i, carry):
            r0 = pl.multiple_of(i * MOE_SB, MOE_SB)
            w = xw[pl.ds(r0, MOE_SB), :]
            x_bf[pl.ds(r0, MOE_SB), 0:half] = pltpu.bitcast(w << 16, F32).astype(BF16)
            x_bf[pl.ds(r0, MOE_SB), half:D_MODEL] = pltpu.bitcast(w & jnp.uint32(0xFFFF0000), F32).astype(BF16)
            return carry

        lax.fori_loop(0, nsub, unpack, 0)

        @pl.when(c + 1 < nact)
        def _():
            start_gather(idxn_ref, meta_ref[META_NSUB, c + 1])

        def phase1(f, carry):
            slot = f & (MOE_RING - 1)
            for cp in gate_up_copies(expert, f, slot):
                cp.wait()

            nxt = f + MOE_AHEAD

            @pl.when(nxt < MOE_NF)
            def _():
                for cp in gate_up_copies(expert, nxt, nxt & (MOE_RING - 1)):
                    cp.start(priority=WEIGHT_DMA_PRIORITY)

            @pl.when(nxt >= MOE_NF)
            def _():
                down_copy(expert, nxt - MOE_NF, nxt - MOE_NF).start(priority=WEIGHT_DMA_PRIORITY)

            def gate_up(r0, rows, wgu):
                xs = x_bf[pl.ds(r0, rows), :]
                gu = jnp.dot(xs, wgu, preferred_element_type=F32)
                gt = gu[:, :MOE_TF]
                up = gu[:, MOE_TF:]
                h_scr[f, pl.ds(r0, rows), :] = (jax.nn.silu(gt) * up).astype(BF16)

            wgu = jnp.concatenate([wgu_f32[slot, 0].astype(BF16), wgu_f32[slot, 1].astype(BF16)], axis=1)
            wgu_bf[...] = wgu
            _rest_blocks(nsub, lambda r0, rows: gate_up(r0, rows, wgu_bf[...]))
            return carry

        lax.fori_loop(0, MOE_NF, phase1, 0)

        def phase2(n, carry):
            slot = n & 1
            wslot = n & (MOE_RING - 1)
            down_copy(expert, n, wslot).wait()

            nxt = n + MOE_AHEAD

            @pl.when(nxt < MOE_NN)
            def _():
                down_copy(expert, nxt, nxt & (MOE_RING - 1)).start(priority=WEIGHT_DMA_PRIORITY)

            @pl.when(jnp.logical_and(nxt >= MOE_NN, c + 1 < nact))
            def _():
                for cp in gate_up_copies(meta_ref[META_E, c + 1], nxt - MOE_NN, nxt - MOE_NN):
                    cp.start(priority=WEIGHT_DMA_PRIORITY)

            @pl.when(n >= 2)
            def _():
                wait_out(slot, nsub)

            def down(r0, rows, wd):
                hs = jnp.concatenate([h_scr[f, pl.ds(r0, rows), :] for f in range(MOE_NF)], axis=1)
                obuf[slot, pl.ds(r0, rows), :] = jnp.dot(hs, wd, preferred_element_type=F32)

            wd = wd_f32[wslot].astype(BF16)
            wd_bf[...] = wd
            _rest_blocks(nsub, lambda r0, rows: down(r0, rows, wd_bf[...]))

            def send(i, carry2):
                out_copy(slot, i, n).start()
                return carry2

            lax.fori_loop(0, nsub, send, 0)
            return carry

        lax.fori_loop(0, MOE_NN, phase2, 0)
        wait_out(0, nsub)
        wait_out(1, nsub)

    pl.when(active)(active_chunk)

    @pl.when(jnp.logical_not(active))
    def _():
        per_chunk = MOE_RC // MOE_SB
        first = (c - nact) * per_chunk
        n_here = jnp.clip(meta_ref[META_MISC, MISC_TAIL_PIECES] - first, 0, per_chunk)
        tail0 = meta_ref[META_MISC, MISC_TAIL_ROW0]
        obuf[0, 0:MOE_SB, :] = jnp.zeros((MOE_SB, MOE_TN), F32)

        def zcopy(k, col):
            r = pl.multiple_of(tail0 + (first + k) * MOE_SB, MOE_SB)
            return pltpu.make_async_copy(
                obuf.at[0, pl.ds(0, MOE_SB), :],
                y_hbm.at[pl.ds(r, MOE_SB), pl.ds(col * MOE_TN, MOE_TN)],
                osem.at[0])

        def send(k, carry):
            for col in range(MOE_NN):
                zcopy(k, col).start()
            return carry

        def wait(k, carry):
            for col in range(MOE_NN):
                zcopy(k, col).wait()
            return carry

        lax.fori_loop(0, n_here, send, 0)
        lax.fori_loop(0, n_here, wait, 0)


def _ffn(meta, idx_tab, hp, w_gate, w_up, w_down, n_rows):
    d = D_MODEL
    nc = idx_tab.shape[0]
    any_spec = lambda: pl.BlockSpec(memory_space=pl.ANY)
    grid_spec = pltpu.PrefetchScalarGridSpec(
        num_scalar_prefetch=1,
        grid=(nc,),
        in_specs=[
            pl.BlockSpec((None, 1, MOE_RC), lambda c, m: (c, 0, 0), memory_space=pltpu.SMEM),
            pl.BlockSpec((None, 1, MOE_RC), lambda c, m: (jnp.minimum(c + 1, nc - 1), 0, 0),
                         memory_space=pltpu.SMEM),
            any_spec(), any_spec(), any_spec(), any_spec(),
        ],
        out_specs=any_spec(),
        scratch_shapes=[
            pltpu.VMEM((MOE_RC, d // 2), jnp.uint32),
            pltpu.VMEM((MOE_RC, d), BF16),
            pltpu.VMEM((MOE_NF, MOE_RC, MOE_TF), BF16),
            pltpu.VMEM((MOE_RING, 2, d, MOE_TF), F32),
            pltpu.VMEM((MOE_RING, EXPERT_FF, MOE_TN), F32),
            pltpu.VMEM((d, 2 * MOE_TF), BF16),
            pltpu.VMEM((EXPERT_FF, MOE_TN), BF16),
            pltpu.VMEM((2, MOE_RC, MOE_TN), F32),
            pltpu.SemaphoreType.DMA(()),
            pltpu.SemaphoreType.DMA((2,)),
            pltpu.SemaphoreType.DMA((MOE_RING,)),
            pltpu.SemaphoreType.DMA((MOE_RING,)),
        ],
    )
    return pl.pallas_call(
        _ffn_kernel,
        grid_spec=grid_spec,
        out_shape=jax.ShapeDtypeStruct((n_rows, d), F32),
        compiler_params=_cparams(("arbitrary",)),
        name="moe_ffn",
    )(meta, idx_tab, idx_tab, hp, w_gate, w_up, w_down)


def _combine_kernel(d0_ref, d1_ref, d0n_ref, d1n_ref, x_ref, route_ref, g2_ref, fg_ref, y_hbm, o_ref,
                    buf, sem):
    tm = COMB_TM
    i = pl.program_id(0)
    slot = i & 1

    def start_rows(a_ref, b_ref, sl):
        def issue(r, c):
            pltpu.make_async_copy(y_hbm.at[pl.ds(a_ref[0, r], 1), :], buf.at[sl, 0, pl.ds(r, 1), :],
                                  sem.at[sl, 0]).start()
            pltpu.make_async_copy(y_hbm.at[pl.ds(b_ref[0, r], 1), :], buf.at[sl, 1, pl.ds(r, 1), :],
                                  sem.at[sl, 1]).start(priority=1)
            return c

        lax.fori_loop(0, tm, issue, 0)

    @pl.when(i == 0)
    def _():
        start_rows(d0_ref, d1_ref, 0)

    @pl.when(i + 1 < pl.num_programs(0))
    def _():
        start_rows(d0n_ref, d1n_ref, 1 - slot)

    for k in range(TOP_K):
        pltpu.make_async_copy(y_hbm.at[pl.ds(0, tm), :], buf.at[slot, k], sem.at[slot, k]).wait()
    route = route_ref[...]
    y = buf[slot, 0] * route[:, 4:5] + buf[slot, 1] * route[:, 5:6]
    x2 = x_ref[...] + g2_ref[...] * y
    ms = jnp.mean(x2 * x2, axis=-1, keepdims=True)
    o_ref[...] = (x2 * lax.rsqrt(ms + RMS_EPS)) * fg_ref[...]


def _combine(dest0, dest1, x1, route, mod3, final_g, y_sorted):
    t, d = x1.shape
    nb = t // COMB_TM
    smem_idx = lambda: pl.BlockSpec((None, 1, COMB_TM), lambda i: (i, 0, 0), memory_space=pltpu.SMEM)
    smem_next = lambda: pl.BlockSpec((None, 1, COMB_TM), lambda i: (jnp.minimum(i + 1, nb - 1), 0, 0),
                                     memory_space=pltpu.SMEM)
    d0 = dest0.reshape(nb, 1, COMB_TM)
    d1 = dest1.reshape(nb, 1, COMB_TM)
    return pl.pallas_call(
        _combine_kernel,
        grid=(nb,),
        in_specs=[
            smem_idx(),
            smem_idx(),
            smem_next(),
            smem_next(),
            pl.BlockSpec((COMB_TM, d), lambda i: (i, 0)),
            pl.BlockSpec((COMB_TM, ROUTE_LANES), lambda i: (i, 0)),
            _mod_spec(5),
            pl.BlockSpec((1, d), lambda i: (0, 0)),
            pl.BlockSpec(memory_space=pl.ANY),
        ],
        out_specs=pl.BlockSpec((COMB_TM, d), lambda i: (i, 0)),
        out_shape=jax.ShapeDtypeStruct((t, d), F32),
        scratch_shapes=[
            pltpu.VMEM((2, TOP_K, COMB_TM, d), F32),
            pltpu.SemaphoreType.DMA((2, TOP_K)),
        ],
        compiler_params=_cparams(("arbitrary",)),
        name="moe_combine",
    )(d0, d1, d0, d1, x1, route, mod3, final_g.reshape(1, d), y_sorted)


def _dispatch_plan(route, cnt, n_tokens):
    n_assign = n_tokens * TOP_K
    n_rows = n_assign + N_EXPERTS * MOE_SB
    nc = n_assign // MOE_RC + N_EXPERTS
    i32 = jnp.int32
    eid = route[:, 0:2].astype(i32)
    rank = route[:, 2:4].astype(i32)
    counts = cnt[0, EXPERT_LANE0:EXPERT_LANE0 + N_EXPERTS].astype(i32)
    seg_rows = (counts + MOE_SB - 1) // MOE_SB * MOE_SB
    seg_end = jnp.cumsum(seg_rows)
    seg_start = seg_end - seg_rows
    e_hot = eid[:, :, None] == jnp.arange(N_EXPERTS, dtype=i32)
    lookup = lambda tab: jnp.sum(jnp.where(e_hot, tab, 0), axis=-1)
    dest = lookup(seg_start) + rank
    nchunk = (seg_rows + MOE_RC - 1) // MOE_RC
    ch_end = jnp.cumsum(nchunk)
    ch_start = ch_end - nchunk
    nact = ch_end[-1]
    cidx = jnp.arange(nc, dtype=i32)
    ch_e = jnp.minimum(jnp.searchsorted(ch_end, cidx, side="right"), N_EXPERTS - 1).astype(i32)
    k = cidx - ch_start[ch_e]
    ch_row0 = jnp.where(cidx < nact, seg_start[ch_e] + k * MOE_RC, 0)
    ch_nsub = jnp.where(cidx < nact, jnp.clip(seg_rows[ch_e] - k * MOE_RC, 0, MOE_RC) // MOE_SB, 0)
    misc = jnp.zeros((nc,), i32).at[MISC_NACT].set(nact)
    misc = misc.at[MISC_TAIL_ROW0].set(seg_end[-1]).at[MISC_TAIL_PIECES].set((n_rows - seg_end[-1]) // MOE_SB)
    meta = jnp.stack([ch_e, ch_row0, ch_nsub, misc]).astype(i32)
    slot = (lookup(ch_start) + rank // MOE_RC) * MOE_RC + rank % MOE_RC
    tok = jnp.broadcast_to(jnp.arange(n_tokens, dtype=i32)[:, None], (n_tokens, TOP_K))
    idx_tab = (jnp.arange(nc * MOE_RC, dtype=i32) % n_tokens).at[slot.reshape(-1)].set(tok.reshape(-1))
    return dest, meta, idx_tab.reshape(nc, 1, MOE_RC), n_rows


def kernel(x, c, ada_w, ada_b, norm1_g, w_in, lambda_q1, lambda_k1, lambda_q2, lambda_k2, subln_g,
           pool_w, pool_scale, w_out, norm2_g, router_group_w, router_group_b, router_expert_w,
           router_expert_b, expert_w_gate, expert_w_up, expert_w_down, final_norm_g):
    b_, s_, d = x.shape
    assert b_ == 1 and d == D_MODEL and ada_w.shape[0] == 1
    t = b_ * s_
    x2d = x.reshape(t, d)

    mod3 = _ada_mod(c, ada_w[0], ada_b[0]).reshape(6, 1, d)

    h = _norm1(x2d, norm1_g[0], mod3)
    proj = _inproj(h, w_in[0])
    slopes = 2.0 ** (-8.0 * jnp.arange(1, N_HEADS + 1, dtype=F32) / N_HEADS)
    o_attn = _attention(proj, slopes, lambda_q1, lambda_k1, lambda_q2, lambda_k2, subln_g)
    o_pool = _pool(proj, pool_w[0].astype(BF16), pool_scale[0])
    x1 = _outproj(o_attn, o_pool, w_out[0], x2d, mod3)

    rw = jnp.zeros((d, ROUTE_LANES), F32)
    rw = rw.at[:, :N_EXPERT_GROUPS].set(router_group_w[0])
    rw = rw.at[:, EXPERT_LANE0:EXPERT_LANE0 + N_EXPERTS].set(router_expert_w[0])
    rw_hi = rw.astype(BF16)
    rw_lo = (rw - rw_hi.astype(F32)).astype(BF16)
    rbias = jnp.full((1, ROUTE_LANES), NEG_BIG, F32)
    rbias = rbias.at[0, :N_EXPERT_GROUPS].set(router_group_b[0])
    rbias = rbias.at[0, EXPERT_LANE0:EXPERT_LANE0 + N_EXPERTS].set(router_expert_b[0].reshape(-1))
    h2, route, cnt = _norm2_route(x1, norm2_g[0], mod3, rw_hi, rw_lo, rbias)

    dest, meta, idx_tab, n_rows = _dispatch_plan(route, cnt, t)
    y_sorted = _ffn(meta, idx_tab, h2, expert_w_gate[0], expert_w_up[0], expert_w_down[0], n_rows)
    out = _combine(dest[:, 0], dest[:, 1], x1, route, mod3, final_norm_g, y_sorted)
    return out.reshape(b_, s_, d)
```
